```python
import jax, jax.numpy as jnp
from jax import lax
import numpy as np

D_MODEL = 1024
BATCH = 2
SEQ = 8192
DEPTH = 2

CTX_LEN = 256
GRID_W = 64
HEAD_DIM = 64
A_HEADS = 6
A_KV_HEADS = 2
B_HEADS = 6
B_KV_HEADS = 2
C_GROUPS = 4
C_GROUP_DIM = 64
A_Q = A_HEADS * HEAD_DIM
A_KV = A_KV_HEADS * HEAD_DIM
B_Q = B_HEADS * HEAD_DIM
B_KV = B_KV_HEADS * HEAD_DIM
C_WIDTH = C_GROUPS * C_GROUP_DIM
MIX_WIDTH = A_Q + B_Q + C_WIDTH
D_IN = A_Q + 2 * A_KV + B_Q + 2 * B_KV + C_WIDTH
WINDOW = 128
BLOCK = 128
D_FF = 2816
ROPE_THETA = 10000.0
NORM_EPS = 1e-6
N_MOD = 9
NEG_INF = -1e30

kernel_name = "hybrid_window_axial_fourier_macaron_dit"


def rms_norm(x, g):
    xf = x.astype(jnp.float32)
    y = xf * lax.rsqrt(jnp.mean(xf * xf, axis=-1, keepdims=True) + NORM_EPS)
    return (y * g.astype(jnp.float32)).astype(x.dtype)


def modulate(x, shift, scale):
    return x * (1 + scale) + shift


def swiglu(x, w_gate, w_up, w_down):
    return (jax.nn.silu(x @ w_gate) * (x @ w_up)) @ w_down


def axial_rope_angles(length):
    t = jnp.arange(length)
    row = (t // GRID_W).astype(jnp.float32)
    col = (t % GRID_W).astype(jnp.float32)
    half = HEAD_DIM // 2
    inv = ROPE_THETA ** (-jnp.arange(0, half, 2, dtype=jnp.float32) / half)
    return row[:, None] * inv, col[:, None] * inv


def _rotate_half(x, ang):
    n2 = x.shape[-1] // 2
    cos = jnp.cos(ang)[:, None, :]
    sin = jnp.sin(ang)[:, None, :]
    x1, x2 = x[..., :n2], x[..., n2:]
    return jnp.concatenate([x1 * cos - x2 * sin, x2 * cos + x1 * sin], axis=-1)


def apply_axial_rope(x, ang_row, ang_col):
    xf = x.astype(jnp.float32)
    half = HEAD_DIM // 2
    out = jnp.concatenate([_rotate_half(xf[..., :half], ang_row),
                           _rotate_half(xf[..., half:], ang_col)], axis=-1)
    return out.astype(x.dtype)


def split_mixer_inputs(u):
    bn, length = u.shape[:2]
    sizes = (A_Q, A_KV, A_KV, B_Q, B_KV, B_KV, C_WIDTH)
    idx = [int(i) for i in np.cumsum(sizes)[:-1]]
    qa, ka, va, qb, kb, vb, uf = jnp.split(u, idx, axis=-1)
    heads = lambda t: t.reshape(bn, length, -1, HEAD_DIM)
    return heads(qa), heads(ka), heads(va), heads(qb), heads(kb), heads(vb), uf


def window_attention_latent(q, k, v, k_ctx, v_ctx, sink):
    bn, length = q.shape[:2]
    nb = length // BLOCK
    g = A_HEADS // A_KV_HEADS
    span = BLOCK + 2 * WINDOW
    scale = HEAD_DIM ** -0.5
    qb = q.reshape(bn, nb, BLOCK, A_KV_HEADS, g, HEAD_DIM)
    pad = ((0, 0), (WINDOW, WINDOW), (0, 0), (0, 0))
    kp, vp = jnp.pad(k, pad), jnp.pad(v, pad)

    def band(t):
        return jnp.concatenate(
            [t[:, s:s + length].reshape(bn, nb, BLOCK, A_KV_HEADS, HEAD_DIM) for s in (0, BLOCK, 2 * BLOCK)],
            axis=2)

    kb, vb = band(kp), band(vp)
    s_band = jnp.einsum('bnqkgd,bnskd->bnkgqs', qb, kb, preferred_element_type=jnp.float32) * scale
    s_ctx = jnp.einsum('bnqkgd,bskd->bnkgqs', qb, k_ctx, preferred_element_type=jnp.float32) * scale
    qpos = jnp.arange(nb)[:, None, None] * BLOCK + jnp.arange(BLOCK)[None, :, None]
    kpos = jnp.arange(nb)[:, None, None] * BLOCK - WINDOW + jnp.arange(span)[None, None, :]
    valid = (jnp.abs(kpos - qpos) <= WINDOW) & (kpos >= 0) & (kpos < length)
    s_band = jnp.where(valid[None, :, None, None], s_band, NEG_INF)
    sink_l = jnp.broadcast_to(sink.astype(jnp.float32).reshape(1, 1, A_KV_HEADS, g, 1, 1),
                              s_band.shape[:-1] + (1,))
    p = jax.nn.softmax(jnp.concatenate([s_band, s_ctx, sink_l], axis=-1), axis=-1)
    lc = k_ctx.shape[1]
    p_band = p[..., :span].astype(v.dtype)
    p_ctx = p[..., span:span + lc].astype(v.dtype)
    o = (jnp.einsum('bnkgqs,bnskd->bnqkgd', p_band, vb)
         + jnp.einsum('bnkgqs,bskd->bnqkgd', p_ctx, v_ctx))
    return o.reshape(bn, length, A_Q)


def global_attention_latent(q, k, v, k_ctx, v_ctx):
    bn, length = q.shape[:2]
    nb = length // BLOCK
    g = B_HEADS // B_KV_HEADS
    scale = HEAD_DIM ** -0.5
    qb = q.reshape(bn, nb, BLOCK, B_KV_HEADS, g, HEAD_DIM).transpose(1, 0, 2, 3, 4, 5)

    def one_block(qblk):
        s_lat = jnp.einsum('bqkgd,bskd->bkgqs', qblk, k, preferred_element_type=jnp.float32) * scale
        s_ctx = jnp.einsum('bqkgd,bskd->bkgqs', qblk, k_ctx, preferred_element_type=jnp.float32) * scale
        p = jax.nn.softmax(jnp.concatenate([s_lat, s_ctx], axis=-1), axis=-1).astype(v.dtype)
        return (jnp.einsum('bkgqs,bskd->bqkgd', p[..., :length], v)
                + jnp.einsum('bkgqs,bskd->bqkgd', p[..., length:], v_ctx))

    o = lax.map(one_block, qb)
    return o.transpose(1, 0, 2, 3, 4, 5).reshape(bn, length, B_Q)


def ctx_attention(q, k, v, sink=None):
    bn, lc, hq, _ = q.shape
    hk = k.shape[2]
    g = hq // hk
    qg = q.reshape(bn, lc, hk, g, HEAD_DIM)
    s = jnp.einsum('bqkgd,bskd->bkgqs', qg, k, preferred_element_type=jnp.float32) * (HEAD_DIM ** -0.5)
    if sink is not None:
        sink_l = jnp.broadcast_to(sink.astype(jnp.float32).reshape(1, hk, g, 1, 1), s.shape[:-1] + (1,))
        s = jnp.concatenate([s, sink_l], axis=-1)
    p = jax.nn.softmax(s, axis=-1)[..., :lc].astype(v.dtype)
    o = jnp.einsum('bkgqs,bskd->bqkgd', p, v)
    return o.reshape(bn, lc, hq * HEAD_DIM)


def fourier_mix(u, w_four):
    bn, length = u.shape[:2]
    ug = u.astype(jnp.float32).reshape(bn, length, C_GROUPS, C_GROUP_DIM)
    f = jnp.fft.fft2(ug, axes=(1, 3), norm='ortho').real.astype(u.dtype)
    return jnp.einsum('blgc,gcd->blgd', f, w_four).reshape(bn, length, C_WIDTH)


def trunk_layer(h, hc, c, c_ctx, p, ang_row, ang_col, update_ctx):
    mod = (jax.nn.silu(c) @ p['w_ada'] + p['b_ada'])[:, None, :]
    mod_c = jax.nn.silu(c_ctx) @ p['w_ada'] + p['b_ada']
    s1, sc1, g1, s2, sc2, g2, s3, sc3, g3 = jnp.split(mod, N_MOD, axis=-1)
    s1c, sc1c, g1c, s2c, sc2c, g2c, s3c, sc3c, g3c = jnp.split(mod_c, N_MOD, axis=-1)

    h = h + 0.5 * g1 * swiglu(modulate(rms_norm(h, p['g_ffn1']), s1, sc1), p['w1_gate'], p['w1_up'], p['w1_down'])
    hc = hc + 0.5 * g1c * swiglu(modulate(rms_norm(hc, p['g_ffn1']), s1c, sc1c), p['w1_gate'], p['w1_up'], p['w1_down'])

    qa, ka, va, qb, kb, vb, uf = split_mixer_inputs(modulate(rms_norm(h, p['g_mix']), s2, sc2) @ p['w_in'])
    qa_c, ka_c, va_c, qb_c, kb_c, vb_c, uf_c = split_mixer_inputs(
        modulate(rms_norm(hc, p['g_mix']), s2c, sc2c) @ p['w_in'])
    qb, kb = rms_norm(qb, p['g_qn']), rms_norm(kb, p['g_kn'])
    qb_c, kb_c = rms_norm(qb_c, p['g_qn']), rms_norm(kb_c, p['g_kn'])
    qa, ka = apply_axial_rope(qa, ang_row, ang_col), apply_axial_rope(ka, ang_row, ang_col)
    qb, kb = apply_axial_rope(qb, ang_row, ang_col), apply_axial_rope(kb, ang_row, ang_col)

    o = jnp.concatenate([
        window_attention_latent(qa, ka, va, ka_c, va_c, p['sink']),
        global_attention_latent(qb, kb, vb, kb_c, vb_c),
        fourier_mix(uf, p['w_four']),
    ], axis=-1) @ p['w_out']
    h = h + g2 * o

    h = h + 0.5 * g3 * swiglu(modulate(rms_norm(h, p['g_ffn2']), s3, sc3), p['w2_gate'], p['w2_up'], p['w2_down'])

    if update_ctx:
        oc = jnp.concatenate([
            ctx_attention(qa_c, ka_c, va_c, p['sink']),
            ctx_attention(qb_c, kb_c, vb_c),
            fourier_mix(uf_c, p['w_four']),
        ], axis=-1) @ p['w_out']
        hc = hc + g2c * oc
        hc = hc + 0.5 * g3c * swiglu(modulate(rms_norm(hc, p['g_ffn2']), s3c, sc3c),
                                     p['w2_gate'], p['w2_up'], p['w2_down'])
    return h, hc


def setup_inputs(seed: int = 0) -> dict:
    key = jax.random.key(seed)
    ks = jax.random.split(key, 24)
    f32 = jnp.float32
    nrm = lambda k, shape, s: jax.random.normal(k, shape, f32) * s
    gain = lambda k, shape: 1.0 + 0.02 * jax.random.normal(k, shape, f32)
    D, L = DEPTH, D_MODEL
    return {
        'x': nrm(ks[0], (BATCH, SEQ, D_MODEL), 1.0),
        'c': nrm(ks[1], (BATCH, D_MODEL), 1.0),
        'ctx': nrm(ks[2], (BATCH, CTX_LEN, D_MODEL), 1.0),
        'c_ctx': nrm(ks[3], (D_MODEL,), 1.0),
        'w_ada': nrm(ks[4], (D, L, N_MOD * L), 0.5 * L ** -0.5),
        'b_ada': nrm(ks[5], (D, N_MOD * L), 0.01),
        'g_ffn1': gain(ks[6], (D, L)),
        'g_mix': gain(ks[7], (D, L)),
        'g_ffn2': gain(ks[8], (D, L)),
        'w_in': nrm(ks[9], (D, L, D_IN), L ** -0.5),
        'g_qn': gain(ks[10], (D, HEAD_DIM)),
        'g_kn': gain(ks[11], (D, HEAD_DIM)),
        'sink': nrm(ks[12], (D, A_HEADS), 0.5),
        'w_four': nrm(ks[13], (D, C_GROUPS, C_GROUP_DIM, C_GROUP_DIM), C_GROUP_DIM ** -0.5),
        'w_out': nrm(ks[14], (D, MIX_WIDTH, L), MIX_WIDTH ** -0.5),
        'w1_gate': nrm(ks[15], (D, L, D_FF), L ** -0.5),
        'w1_up': nrm(ks[16], (D, L, D_FF), L ** -0.5),
        'w1_down': nrm(ks[17], (D, D_FF, L), D_FF ** -0.5),
        'w2_gate': nrm(ks[18], (D, L, D_FF), L ** -0.5),
        'w2_up': nrm(ks[19], (D, L, D_FF), L ** -0.5),
        'w2_down': nrm(ks[20], (D, D_FF, L), D_FF ** -0.5),
        'g_final': gain(ks[21], (L,)),
    }


def reference(x, c, ctx, c_ctx, w_ada, b_ada, g_ffn1, g_mix, g_ffn2, w_in, g_qn, g_kn, sink, w_four,
              w_out, w1_gate, w1_up, w1_down, w2_gate, w2_up, w2_down, g_final):
    length = x.shape[1]
    ang_row, ang_col = axial_rope_angles(length)
    h, hc = x, ctx
    for l in range(DEPTH):
        p = {
            'w_ada': w_ada[l], 'b_ada': b_ada[l], 'g_ffn1': g_ffn1[l], 'g_mix': g_mix[l], 'g_ffn2': g_ffn2[l],
            'w_in': w_in[l], 'g_qn': g_qn[l], 'g_kn': g_kn[l], 'sink': sink[l], 'w_four': w_four[l],
            'w_out': w_out[l], 'w1_gate': w1_gate[l], 'w1_up': w1_up[l], 'w1_down': w1_down[l],
            'w2_gate': w2_gate[l], 'w2_up': w2_up[l], 'w2_down': w2_down[l],
        }
        h, hc = trunk_layer(h, hc, c, c_ctx, p, ang_row, ang_col, update_ctx=(l < DEPTH - 1))
    return rms_norm(h, g_final)
```

```python
import functools

import jax
import jax.numpy as jnp
import numpy as np
from jax import lax
from jax.experimental import pallas as pl
from jax.experimental.pallas import tpu as pltpu

HEAD_DIM = 64
A_HEADS = 6
A_KV_HEADS = 2
B_HEADS = 6
B_KV_HEADS = 2
C_GROUPS = 4
C_GROUP_DIM = 64
GRID_W = 64
WINDOW = 128
BLOCK = 128
ROPE_THETA = 10000.0
NORM_EPS = 1e-6
N_MOD = 9
NEG_INF = -1e30

N_HEADS = 6
GROUP = 3
C_WIDTH = C_GROUPS * C_GROUP_DIM
FFT_N1 = 64

LANE = 128
VMEM_LIMIT = 56 * 1024 * 1024

F32 = jnp.float32
BF16 = jnp.bfloat16


def _params(*sem):
    return pltpu.CompilerParams(dimension_semantics=tuple(sem), vmem_limit_bytes=VMEM_LIMIT)


def _resident(shape):
    nd = len(shape)
    return pl.BlockSpec(shape, lambda *_: (0,) * nd, pipeline_mode=pl.Buffered(1))


def _split3(x):
    hi = x.astype(BF16)
    lo = (x - hi.astype(F32)).astype(BF16)
    return hi, lo


def _dot(a, b):
    return jnp.dot(a, b, preferred_element_type=F32)


def _dot3(a_hi, a_lo, b_hi, b_lo):
    return _dot(a_hi, b_hi) + _dot(a_hi, b_lo) + _dot(a_lo, b_hi)


def _silu(x):
    return x / (1.0 + jnp.exp(-x))


def _norm_mod(x, g, shift, scale):
    ms = jnp.mean(x * x, axis=-1, keepdims=True)
    return (x * lax.rsqrt(ms + NORM_EPS)) * (g * (1.0 + scale)) + shift


def _ada_kernel(c_ref, w_ref, b_ref, o_ref):
    s_hi, s_lo = _split3(_silu(c_ref[...]))
    w_hi, w_lo = _split3(w_ref[0])
    o_ref[0] = _dot3(s_hi, s_lo, w_hi, w_lo) + b_ref[0]


def _ada(cvec, w_ada, b_ada):
    depth, d, nd = w_ada.shape
    tn = 1024
    return pl.pallas_call(
        _ada_kernel,
        grid=(depth, nd // tn),
        in_specs=[
            pl.BlockSpec((8, d), lambda l, j: (0, 0)),
            pl.BlockSpec((1, d, tn), lambda l, j: (l, 0, j)),
            pl.BlockSpec((1, 1, tn), lambda l, j: (l, 0, j)),
        ],
        out_specs=pl.BlockSpec((1, 8, tn), lambda l, j: (l, 0, j)),
        out_shape=jax.ShapeDtypeStruct((depth, 8, nd), F32),
        compiler_params=_params("parallel", "parallel"),
        name="ada",
    )(cvec, w_ada, b_ada.reshape(depth, 1, nd))


def _ffn_kernel(x_ref, mod_ref, g_ref, wg_ref, wu_ref, wd_ref, *rest, row0, final):
    if final:
        gf_ref, o_ref = rest
    else:
        (o_ref,) = rest
    x = x_ref[0]
    shift = mod_ref[0, row0:row0 + 1, :]
    scale = mod_ref[0, row0 + 1:row0 + 2, :]
    gate = mod_ref[0, row0 + 2:row0 + 3, :]
    xm = _norm_mod(x, g_ref[...], shift, scale).astype(BF16)
    a = _dot(xm, wg_ref[...])
    u = _dot(xm, wu_ref[...])
    act = (_silu(a) * u).astype(BF16)
    y = x + (0.5 * gate) * _dot(act, wd_ref[...])
    if final:
        ms = jnp.mean(y * y, axis=-1, keepdims=True)
        y = (y * lax.rsqrt(ms + NORM_EPS)) * gf_ref[...]
    o_ref[0] = y


def _ffn(x, mod, g, wg, wu, wd, *, row0, tm, final_g=None):
    groups, t, d = x.shape
    dff = wg.shape[1]
    final = final_g is not None
    in_specs = [
        pl.BlockSpec((1, tm, d), lambda b, i: (b, i, 0)),
        pl.BlockSpec((1, N_MOD, d), lambda b, i: (b, 0, 0)),
        _resident((1, d)),
        _resident((d, dff)),
        _resident((d, dff)),
        _resident((dff, d)),
    ]
    args = [x, mod, g.reshape(1, d), wg, wu, wd]
    if final:
        in_specs.append(_resident((1, d)))
        args.append(final_g.reshape(1, d))
    return pl.pallas_call(
        functools.partial(_ffn_kernel, row0=row0, final=final),
        grid=(groups, t // tm),
        in_specs=in_specs,
        out_specs=pl.BlockSpec((1, tm, d), lambda b, i: (b, i, 0)),
        out_shape=jax.ShapeDtypeStruct((groups, t, d), F32),
        compiler_params=_params("parallel", "parallel"),
        name="ffn",
    )(*args)


def _rope_slab(t, cos, sin, lane):
    fwd = pltpu.roll(t, LANE - 16, axis=1)
    bwd = pltpu.roll(t, 16, axis=1)
    partner = jnp.where((lane % 32) < 16, fwd, bwd)
    return t * cos + partner * sin


def _inproj_kernel(x_ref, mod_ref, g_ref, w_ref, cos_ref, sin_ref, gq_ref, gk_ref, msm_ref,
                   cs_hi_ref, cs_lo_ref, wbd_ref,
                   qa_ref, ka_ref, vat_ref, qb_ref, kb_ref, vbt_ref, z_ref, *, rope):
    x = x_ref[0]
    tm = x.shape[0]
    shift = mod_ref[0, 3:4, :]
    scale = mod_ref[0, 4:5, :]
    xm = _norm_mod(x, g_ref[...], shift, scale).astype(BF16)
    u = _dot(xm, w_ref[...])

    lane = lax.broadcasted_iota(jnp.int32, (tm, LANE), 1)
    low = lane < HEAD_DIM
    if rope:
        cos = cos_ref[...]
        sin = sin_ref[...]

    def qk_norm(t, gain):
        sq_hi, sq_lo = _split3(t * t)
        ms = _dot(sq_hi, msm_ref[...]) + _dot(sq_lo, msm_ref[...])
        return t * lax.rsqrt(ms + NORM_EPS) * gain

    def finish(t):
        return _rope_slab(t, cos, sin, lane) if rope else t

    def emit_q(q_ref, slabs):
        for h in range(N_HEADS):
            t = slabs[h // 2] * (HEAD_DIM ** -0.5)
            if (h % 2) != (h // GROUP):
                t = pltpu.roll(t, HEAD_DIM, axis=1)
            keep = low if (h // GROUP) == 0 else jnp.logical_not(low)
            q_ref[0, h] = jnp.where(keep, t, 0.0).astype(BF16)

    qa = [finish(u[:, LANE * i:LANE * (i + 1)]) for i in range(3)]
    emit_q(qa_ref, qa)
    ka_ref[0] = finish(u[:, 384:512]).astype(BF16)
    vat_ref[0] = u[:, 512:640].T.astype(BF16)

    qb = [finish(qk_norm(u[:, 640 + LANE * i:640 + LANE * (i + 1)], gq_ref[...])) for i in range(3)]
    emit_q(qb_ref, qb)
    kb_ref[0] = finish(qk_norm(u[:, 1024:1152], gk_ref[...])).astype(BF16)
    vbt_ref[0] = u[:, 1152:1280].T.astype(BF16)

    uf_hi, uf_lo = _split3(u[:, 1280:1536])
    f = _dot3(uf_hi, uf_lo, cs_hi_ref[...], cs_lo_ref[...])
    wbd = wbd_ref[...]
    z_ref[0, :, 0:C_WIDTH] = _dot(f[:, 0:C_WIDTH].astype(BF16), wbd)
    z_ref[0, :, C_WIDTH:] = _dot(f[:, C_WIDTH:].astype(BF16), wbd)


def _inproj(x, mod, g, w_in, cos, sin, gq, gk, msm, cs_hi, cs_lo, wbd, *, tm, rope):
    groups, t, d = x.shape
    din = w_in.shape[1]
    tok = lambda b, i: (b, i, 0)
    in_specs = [
        pl.BlockSpec((1, tm, d), tok),
        pl.BlockSpec((1, N_MOD, d), lambda b, i: (b, 0, 0)),
        _resident((1, d)),
        _resident((d, din)),
        pl.BlockSpec((tm, LANE), lambda b, i: (i, 0)),
        pl.BlockSpec((tm, LANE), lambda b, i: (i, 0)),
        _resident((1, LANE)),
        _resident((1, LANE)),
        _resident((LANE, LANE)),
        _resident((C_WIDTH, 2 * C_WIDTH)),
        _resident((C_WIDTH, 2 * C_WIDTH)),
        _resident((C_WIDTH, C_WIDTH)),
    ]
    q_spec = pl.BlockSpec((1, N_HEADS, tm, LANE), lambda b, i: (b, 0, i, 0))
    k_spec = pl.BlockSpec((1, tm, LANE), tok)
    vt_spec = pl.BlockSpec((1, LANE, tm), lambda b, i: (b, 0, i))
    q_shape = jax.ShapeDtypeStruct((groups, N_HEADS, t, LANE), BF16)
    k_shape = jax.ShapeDtypeStruct((groups, t, LANE), BF16)
    vt_shape = jax.ShapeDtypeStruct((groups, LANE, t), BF16)
    return pl.pallas_call(
        functools.partial(_inproj_kernel, rope=rope),
        grid=(groups, t // tm),
        in_specs=in_specs,
        out_specs=[q_spec, k_spec, vt_spec, q_spec, k_spec, vt_spec,
                   pl.BlockSpec((1, tm, 2 * C_WIDTH), tok)],
        out_shape=[q_shape, k_shape, vt_shape, q_shape, k_shape, vt_shape,
                   jax.ShapeDtypeStruct((groups, t, 2 * C_WIDTH), F32)],
        compiler_params=_params("parallel", "parallel"),
        name="inproj",
    )(x, mod, g.reshape(1, d), w_in, cos, sin, gq, gk, msm, cs_hi, cs_lo, wbd)


def _nt(a, b):
    return lax.dot_general(a, b, (((1,), (1,)), ((), ())), preferred_element_type=F32)


def _store_heads(o_ref, ot, tq):
    for pair in range(N_HEADS // 2):
        parts = []
        for h in (2 * pair, 2 * pair + 1):
            j = h % GROUP
            parts.append(ot[h // GROUP][:, j * tq:(j + 1) * tq])
        o_ref[0, :, pair * LANE:(pair + 1) * LANE] = jnp.concatenate(parts, axis=0).T.astype(o_ref.dtype)


def _attn_global_kernel(q_ref, k_ref, vt_ref, o_ref, m_sc, l_sc, acc_sc, *, tk):
    tq = q_ref.shape[2]
    nk = k_ref.shape[1] // tk
    q = q_ref[0].reshape(N_HEADS * tq, LANE)
    m_sc[...] = jnp.full_like(m_sc, NEG_INF)
    l_sc[...] = jnp.zeros_like(l_sc)
    acc_sc[...] = jnp.zeros_like(acc_sc)
    gw = GROUP * tq

    def body(j, carry):
        off = pl.multiple_of(j * tk, tk)
        s = _nt(k_ref[0, pl.ds(off, tk), :], q)
        m_old = m_sc[...]
        m_new = jnp.maximum(m_old, jnp.max(s, axis=0, keepdims=True))
        alpha = jnp.exp(m_old - m_new)
        p = jnp.exp(s - m_new)
        l_sc[...] = l_sc[...] * alpha + jnp.sum(p, axis=0, keepdims=True)
        m_sc[...] = m_new
        pb = p.astype(BF16)
        for g in range(2):
            vt = vt_ref[0, g * HEAD_DIM:(g + 1) * HEAD_DIM, pl.ds(off, tk)]
            acc_sc[g] = acc_sc[g] * alpha[:, g * gw:(g + 1) * gw] + _dot(vt, pb[:, g * gw:(g + 1) * gw])
        return carry

    lax.fori_loop(0, nk, body, 0)
    inv = 1.0 / l_sc[...]
    _store_heads(o_ref, [acc_sc[g] * inv[:, g * gw:(g + 1) * gw] for g in range(2)], tq)


def _key_tile(n_keys, cap=1024):
    return max(t for t in range(LANE, cap + 1, LANE) if n_keys % t == 0)


def _attn_global(q, k, vt, *, tq, tk):
    b, _, l, _ = q.shape
    lk = k.shape[1]
    return pl.pallas_call(
        functools.partial(_attn_global_kernel, tk=tk),
        grid=(b, l // tq),
        in_specs=[
            pl.BlockSpec((1, N_HEADS, tq, LANE), lambda b, i: (b, 0, i, 0)),
            pl.BlockSpec((1, lk, LANE), lambda b, i: (b, 0, 0)),
            pl.BlockSpec((1, LANE, lk), lambda b, i: (b, 0, 0)),
        ],
        out_specs=pl.BlockSpec((1, tq, N_HEADS * HEAD_DIM), lambda b, i: (b, i, 0)),
        out_shape=jax.ShapeDtypeStruct((b, l, N_HEADS * HEAD_DIM), BF16),
        scratch_shapes=[
            pltpu.VMEM((1, N_HEADS * tq), F32),
            pltpu.VMEM((1, N_HEADS * tq), F32),
            pltpu.VMEM((2, HEAD_DIM, GROUP * tq), F32),
        ],
        compiler_params=_params("parallel", "parallel"),
        name="attn_global",
    )(q, k, vt)


def _softmax_pv(scores, vts, extra, tq):
    m = functools.reduce(jnp.maximum, [jnp.max(s, axis=0, keepdims=True) for s in scores])
    if extra is not None:
        m = jnp.maximum(m, extra)
    ps = [jnp.exp(s - m) for s in scores]
    l = functools.reduce(jnp.add, [jnp.sum(p, axis=0, keepdims=True) for p in ps])
    if extra is not None:
        l = l + jnp.exp(extra - m)
    inv = 1.0 / l
    gw = GROUP * tq
    out = []
    for g in range(2):
        acc = None
        for p, vt in zip(ps, vts):
            t = _dot(vt[g * HEAD_DIM:(g + 1) * HEAD_DIM, :], p[:, g * gw:(g + 1) * gw].astype(BF16))
            acc = t if acc is None else acc + t
        out.append(acc * inv[:, g * gw:(g + 1) * gw])
    return out


def _attn_window_kernel(q_ref, k_ref, vt_ref, sink_ref, o_ref, *, length):
    tq = BLOCK
    span = BLOCK + 2 * WINDOW
    n = pl.program_id(1)
    start = pl.multiple_of(jnp.clip(n * BLOCK - WINDOW, 0, length - span), LANE)
    q = q_ref[0].reshape(N_HEADS * tq, LANE)
    s_w = _nt(k_ref[0, pl.ds(start, span), :], q)
    s_c = _nt(k_ref[0, length:, :], q)
    kpos = start + lax.broadcasted_iota(jnp.int32, s_w.shape, 0)
    qpos = n * BLOCK + lax.broadcasted_iota(jnp.int32, s_w.shape, 1) % tq
    s_w = jnp.where(jnp.abs(kpos - qpos) <= WINDOW, s_w, NEG_INF)
    vts = [vt_ref[0, :, pl.ds(start, span)], vt_ref[0, :, length:]]
    _store_heads(o_ref, _softmax_pv([s_w, s_c], vts, sink_ref[...], tq), tq)


def _attn_window(q, k, vt, sink_row):
    b, _, l, _ = q.shape
    lk = k.shape[1]
    return pl.pallas_call(
        functools.partial(_attn_window_kernel, length=l),
        grid=(b, l // BLOCK),
        in_specs=[
            pl.BlockSpec((1, N_HEADS, BLOCK, LANE), lambda b, i: (b, 0, i, 0)),
            pl.BlockSpec((1, lk, LANE), lambda b, i: (b, 0, 0)),
            pl.BlockSpec((1, LANE, lk), lambda b, i: (b, 0, 0)),
            pl.BlockSpec((1, N_HEADS * BLOCK), lambda b, i: (0, 0)),
        ],
        out_specs=pl.BlockSpec((1, BLOCK, N_HEADS * HEAD_DIM), lambda b, i: (b, i, 0)),
        out_shape=jax.ShapeDtypeStruct((b, l, N_HEADS * HEAD_DIM), BF16),
        compiler_params=_params("parallel", "parallel"),
        name="attn_window",
    )(q, k, vt, sink_row)


def _attn_ctx_kernel(q_ref, k_ref, vt_ref, *rest, has_sink):
    if has_sink:
        sink_ref, o_ref = rest
    else:
        (o_ref,) = rest
    tq = q_ref.shape[2]
    q = q_ref[0].reshape(N_HEADS * tq, LANE)
    s = _nt(k_ref[0], q)
    extra = sink_ref[...] if has_sink else None
    _store_heads(o_ref, _softmax_pv([s], [vt_ref[0]], extra, tq), tq)


def _attn_ctx(q, k, vt, sink_row, *, batch):
    lc = q.shape[2] // batch
    in_specs = [
        pl.BlockSpec((1, N_HEADS, lc, LANE), lambda b: (0, 0, b, 0)),
        pl.BlockSpec((1, lc, LANE), lambda b: (0, b, 0)),
        pl.BlockSpec((1, LANE, lc), lambda b: (0, 0, b)),
    ]
    args = [q, k, vt]
    if sink_row is not None:
        in_specs.append(pl.BlockSpec((1, N_HEADS * lc), lambda b: (0, 0)))
        args.append(sink_row)
    return pl.pallas_call(
        functools.partial(_attn_ctx_kernel, has_sink=sink_row is not None),
        grid=(batch,),
        in_specs=in_specs,
        out_specs=pl.BlockSpec((1, lc, N_HEADS * HEAD_DIM), lambda b: (0, b, 0)),
        out_shape=jax.ShapeDtypeStruct((1, batch * lc, N_HEADS * HEAD_DIM), BF16),
        compiler_params=_params("parallel"),
        name="attn_ctx",
    )(*args)


def _fft1_kernel(x_ref, ch_ref, cl_ref, sh_ref, sl_ref, twc_ref, tws_ref, o_ref):
    x_hi, x_lo = _split3(x_ref[0])
    cx = _dot3(ch_ref[...], cl_ref[...], x_hi, x_lo)
    sx = _dot3(sh_ref[...], sl_ref[...], x_hi, x_lo)
    w = 2 * C_WIDTH
    for i in range(twc_ref.shape[0]):
        tc = jnp.concatenate([twc_ref[i]] * (C_WIDTH // LANE), axis=1)
        ts = jnp.concatenate([tws_ref[i]] * (C_WIDTH // LANE), axis=1)
        re, im = slice(i * w, i * w + C_WIDTH), slice(i * w + C_WIDTH, (i + 1) * w)
        ar = cx[:, re] + sx[:, im]
        ai = cx[:, im] - sx[:, re]
        o_ref[0, :, re] = ar * tc + ai * ts
        o_ref[0, :, im] = ai * tc - ar * ts


def _fft2_kernel(x_ref, ch_ref, cl_ref, sh_ref, sl_ref, o_ref):
    for i in range(x_ref.shape[1]):
        r_hi, r_lo = _split3(x_ref[0, i, :, 0:C_WIDTH])
        i_hi, i_lo = _split3(x_ref[0, i, :, C_WIDTH:])
        o_ref[0, :, i * C_WIDTH:(i + 1) * C_WIDTH] = (
            _dot3(ch_ref[...], cl_ref[...], r_hi, r_lo) + _dot3(sh_ref[...], sl_ref[...], i_hi, i_lo))


def _dft_tables(n, norm):
    k = np.arange(n, dtype=np.int64)
    ang = 2.0 * np.pi * ((k[:, None] * k[None, :]) % n).astype(np.float64) / n
    c = jnp.asarray(np.cos(ang) * norm, F32)
    s = jnp.asarray(np.sin(ang) * norm, F32)
    return _split3(c) + _split3(s)


def _fourier_latent(z):
    b, l, w = z.shape
    n1, n2 = FFT_N1, l // FFT_N1
    tn2, tk1 = 8, 8
    prod = jnp.arange(n2)[:, None] * jnp.arange(n1)[None, :]
    ang = prod.astype(F32) * (2.0 * np.pi / l)
    twc = jnp.broadcast_to(jnp.cos(ang)[:, :, None], (n2, n1, LANE))
    tws = jnp.broadcast_to(jnp.sin(ang)[:, :, None], (n2, n1, LANE))
    t1 = _dft_tables(n1, 1.0)
    y = pl.pallas_call(
        _fft1_kernel,
        grid=(b, n2 // tn2),
        in_specs=[pl.BlockSpec((1, n1, tn2 * w), lambda b, j: (b, 0, j))]
        + [_resident((n1, n1))] * 4
        + [pl.BlockSpec((tn2, n1, LANE), lambda b, j: (j, 0, 0))] * 2,
        out_specs=pl.BlockSpec((1, n1, tn2 * w), lambda b, j: (b, 0, j)),
        out_shape=jax.ShapeDtypeStruct((b, n1, n2 * w), F32),
        compiler_params=_params("parallel", "parallel"),
        name="fft_stage1",
    )(z.reshape(b, n1, n2 * w), *t1, twc, tws)
    t2m = _dft_tables(n2, 1.0 / np.sqrt(l))
    out = pl.pallas_call(
        _fft2_kernel,
        grid=(b, n1 // tk1),
        in_specs=[pl.BlockSpec((1, tk1, n2, w), lambda b, j: (b, j, 0, 0))] + [_resident((n2, n2))] * 4,
        out_specs=pl.BlockSpec((1, n2, tk1 * C_WIDTH), lambda b, j: (b, 0, j)),
        out_shape=jax.ShapeDtypeStruct((b, n2, n1 * C_WIDTH), F32),
        compiler_params=_params("parallel", "parallel"),
        name="fft_stage2",
    )(y.reshape(b, n1, n2, w), *t2m)
    return out.reshape(b, l, C_WIDTH)


def _fourier_ctx_kernel(x_ref, ch_ref, cl_ref, sh_ref, sl_ref, o_ref):
    r_hi, r_lo = _split3(x_ref[0, :, 0:C_WIDTH])
    i_hi, i_lo = _split3(x_ref[0, :, C_WIDTH:])
    o_ref[0] = _dot3(ch_ref[...], cl_ref[...], r_hi, r_lo) + _dot3(sh_ref[...], sl_ref[...], i_hi, i_lo)


def _fourier_ctx(z, *, batch):
    lc = z.shape[1] // batch
    tabs = _dft_tables(lc, 1.0 / np.sqrt(lc))
    return pl.pallas_call(
        _fourier_ctx_kernel,
        grid=(batch,),
        in_specs=[pl.BlockSpec((1, lc, 2 * C_WIDTH), lambda b: (0, b, 0))] + [_resident((lc, lc))] * 4,
        out_specs=pl.BlockSpec((1, lc, C_WIDTH), lambda b: (0, b, 0)),
        out_shape=jax.ShapeDtypeStruct((1, batch * lc, C_WIDTH), F32),
        compiler_params=_params("parallel"),
        name="fourier_ctx",
    )(z, *tabs)


def _outproj_kernel(h_ref, oa_ref, ob_ref, oc_ref, mod_ref, w_ref, o_ref):
    na = N_HEADS * HEAD_DIM
    y = (_dot(oa_ref[0], w_ref[0:na, :]) + _dot(ob_ref[0], w_ref[na:2 * na, :])
         + _dot(oc_ref[0].astype(BF16), w_ref[2 * na:, :]))
    o_ref[0] = h_ref[0] + mod_ref[0, 5:6, :] * y


def _outproj(h, oa, ob, oc, mod, w_out, *, tm):
    groups, t, d = h.shape
    na = N_HEADS * HEAD_DIM
    tok = lambda b, i: (b, i, 0)
    return pl.pallas_call(
        _outproj_kernel,
        grid=(groups, t // tm),
        in_specs=[
            pl.BlockSpec((1, tm, d), tok),
            pl.BlockSpec((1, tm, na), tok),
            pl.BlockSpec((1, tm, na), tok),
            pl.BlockSpec((1, tm, C_WIDTH), tok),
            pl.BlockSpec((1, N_MOD, d), lambda b, i: (b, 0, 0)),
            _resident(w_out.shape),
        ],
        out_specs=pl.BlockSpec((1, tm, d), tok),
        out_shape=jax.ShapeDtypeStruct((groups, t, d), F32),
        compiler_params=_params("parallel", "parallel"),
        name="outproj",
    )(h, oa, ob, oc, mod, w_out)


def _rope_tables(length):
    t = jnp.arange(length)
    row = (t // GRID_W).astype(F32)
    col = (t % GRID_W).astype(F32)
    half = HEAD_DIM // 2
    inv = ROPE_THETA ** (-jnp.arange(0, half, 2, dtype=F32) / half)
    ar, ac = row[:, None] * inv, col[:, None] * inv
    cos = jnp.concatenate([jnp.cos(ar), jnp.cos(ar), jnp.cos(ac), jnp.cos(ac)], axis=-1)
    sin = jnp.concatenate([-jnp.sin(ar), jnp.sin(ar), -jnp.sin(ac), jnp.sin(ac)], axis=-1)
    return jnp.tile(cos, (1, 2)), jnp.tile(sin, (1, 2))


def _channel_dft():
    k = np.arange(C_GROUP_DIM)
    ang = 2.0 * np.pi * ((k[:, None] * k[None, :]) % C_GROUP_DIM) / C_GROUP_DIM
    eye = np.eye(C_GROUPS)
    norm = 1.0 / np.sqrt(C_GROUP_DIM)
    cs = np.concatenate([np.kron(eye, np.cos(ang) * norm), np.kron(eye, -np.sin(ang) * norm)], axis=1)
    return _split3(jnp.asarray(cs, F32))


def _block_diag(w):
    g, c, _ = w.shape
    eye = jnp.eye(g, dtype=w.dtype)
    return (eye[:, None, :, None] * w[:, :, None, :]).reshape(g * c, g * c)


def kernel(x, c, ctx, c_ctx, w_ada, b_ada, g_ffn1, g_mix, g_ffn2, w_in, g_qn, g_kn, sink, w_four,
           w_out, w1_gate, w1_up, w1_down, w2_gate, w2_up, w2_down, g_final):
    batch, length, d = x.shape
    lc = ctx.shape[1]
    depth = w_ada.shape[0]
    assert length % (FFT_N1 * 8) == 0 and length >= BLOCK + 2 * WINDOW and batch + 1 <= 8
    tm = 512
    tm_ctx = batch * lc

    cvec = jnp.zeros((8, d), F32).at[:batch].set(c).at[batch].set(c_ctx)
    mod = _ada(cvec, w_ada, b_ada).reshape(depth, 8, N_MOD, d)

    cos, sin = _rope_tables(length)
    ones = jnp.ones((tm_ctx, LANE), F32)
    msm = _block_diag(jnp.full((2, HEAD_DIM, HEAD_DIM), 1.0 / HEAD_DIM, F32)).astype(BF16)
    cs_hi, cs_lo = _channel_dft()

    h = x
    hc = ctx.reshape(1, batch * lc, d)
    for l in range(depth):
        last = l == depth - 1
        mod_l, mod_c = mod[l, :batch], mod[l, batch:batch + 1]
        bf = lambda w: w[l].astype(BF16)
        w1 = (bf(w1_gate), bf(w1_up), bf(w1_down))
        w2 = (bf(w2_gate), bf(w2_up), bf(w2_down))
        win, wout = bf(w_in), bf(w_out)
        gq = jnp.tile(g_qn[l].reshape(1, HEAD_DIM), (1, 2))
        gk = jnp.tile(g_kn[l].reshape(1, HEAD_DIM), (1, 2))
        wbd = _block_diag(w_four[l]).astype(BF16)
        proj = functools.partial(_inproj, g=g_mix[l], w_in=win, gq=gq, gk=gk, msm=msm,
                                 cs_hi=cs_hi, cs_lo=cs_lo, wbd=wbd)

        h = _ffn(h, mod_l, g_ffn1[l], *w1, row0=0, tm=tm)
        hc = _ffn(hc, mod_c, g_ffn1[l], *w1, row0=0, tm=tm_ctx)

        qa, ka, vat, qb, kb, vbt, z = proj(h, mod_l, cos=cos, sin=sin, tm=tm, rope=True)
        qa_c, ka_c, vat_c, qb_c, kb_c, vbt_c, z_c = proj(hc, mod_c, cos=ones, sin=ones, tm=tm_ctx, rope=False)

        cat_k = lambda k, k_c: jnp.concatenate([k, k_c.reshape(batch, lc, LANE)], axis=1)
        cat_v = lambda v, v_c: jnp.concatenate(
            [v, v_c.reshape(LANE, batch, lc).transpose(1, 0, 2)], axis=2)
        sink_l = jnp.repeat(sink[l].astype(F32), BLOCK).reshape(1, N_HEADS * BLOCK)
        oa = _attn_window(qa, cat_k(ka, ka_c), cat_v(vat, vat_c), sink_l)
        ob = _attn_global(qb, cat_k(kb, kb_c), cat_v(vbt, vbt_c), tq=256, tk=_key_tile(length + lc))
        oc = _fourier_latent(z)
        h = _outproj(h, oa, ob, oc, mod_l, wout, tm=tm)
        h = _ffn(h, mod_l, g_ffn2[l], *w2, row0=6, tm=tm, final_g=g_final if last else None)

        if not last:
            sink_c = jnp.repeat(sink[l].astype(F32), lc).reshape(1, N_HEADS * lc)
            oa_c = _attn_ctx(qa_c, ka_c, vat_c, sink_c, batch=batch)
            ob_c = _attn_ctx(qb_c, kb_c, vbt_c, None, batch=batch)
            oc_c = _fourier_ctx(z_c, batch=batch)
            hc = _outproj(hc, oa_c, ob_c, oc_c, mod_c, wout, tm=tm_ctx)
            hc = _ffn(hc, mod_c, g_ffn2[l], *w2, row0=6, tm=tm_ctx)
    return h
```

```python
import functools

import jax
import jax.numpy as jnp
import numpy as np
from jax import lax
from jax.experimental import pallas as pl
from jax.experimental.pallas import tpu as pltpu

HEAD_DIM = 64
A_HEADS = 6
A_KV_HEADS = 2
B_HEADS = 6
B_KV_HEADS = 2
C_GROUPS = 4
C_GROUP_DIM = 64
GRID_W = 64
WINDOW = 128
BLOCK = 128
ROPE_THETA = 10000.0
NORM_EPS = 1e-6
N_MOD = 9
NEG_INF = -1e30

N_HEADS = 6
GROUP = 3
C_WIDTH = C_GROUPS * C_GROUP_DIM
FFT_N1 = 64
DEN_ROWS = 16
LOG2E = 1.4426950408889634

LANE = 128
VMEM_LIMIT = 56 * 1024 * 1024

F32 = jnp.float32
BF16 = jnp.bfloat16


def _params(*sem):
    return pltpu.CompilerParams(dimension_semantics=tuple(sem), vmem_limit_bytes=VMEM_LIMIT)


def _resident(shape):
    nd = len(shape)
    return pl.BlockSpec(shape, lambda *_: (0,) * nd, pipeline_mode=pl.Buffered(1))


def _split3(x):
    hi = x.astype(BF16)
    lo = (x - hi.astype(F32)).astype(BF16)
    return hi, lo


def _dot(a, b):
    return jnp.dot(a, b, preferred_element_type=F32)


def _dot3(a_hi, a_lo, b_hi, b_lo):
    return _dot(a_hi, b_hi) + _dot(a_hi, b_lo) + _dot(a_lo, b_hi)


def _silu(x):
    return x / (1.0 + jnp.exp(-x))


def _norm_mod(x, g, shift, scale):
    ms = jnp.mean(x * x, axis=-1, keepdims=True)
    return (x * lax.rsqrt(ms + NORM_EPS)) * (g * (1.0 + scale)) + shift


def _ada_kernel(c_ref, w_ref, b_ref, o_ref):
    s_hi, s_lo = _split3(_silu(c_ref[...]))
    w_hi, w_lo = _split3(w_ref[0])
    o_ref[0] = _dot3(s_hi, s_lo, w_hi, w_lo) + b_ref[0]


def _ada(cvec, w_ada, b_ada):
    depth, d, nd = w_ada.shape
    tn = 1024
    return pl.pallas_call(
        _ada_kernel,
        grid=(depth, nd // tn),
        in_specs=[
            pl.BlockSpec((8, d), lambda l, j: (0, 0)),
            pl.BlockSpec((1, d, tn), lambda l, j: (l, 0, j)),
            pl.BlockSpec((1, 1, tn), lambda l, j: (l, 0, j)),
        ],
        out_specs=pl.BlockSpec((1, 8, tn), lambda l, j: (l, 0, j)),
        out_shape=jax.ShapeDtypeStruct((depth, 8, nd), F32),
        compiler_params=_params("parallel", "parallel"),
        name="ada",
    )(cvec, w_ada, b_ada.reshape(depth, 1, nd))


def _ffn_kernel(x_ref, mod_ref, g_ref, wg_ref, wu_ref, wd_ref, *rest, row0, final):
    if final:
        gf_ref, o_ref = rest
    else:
        (o_ref,) = rest
    x = x_ref[0]
    shift = mod_ref[0, row0:row0 + 1, :]
    scale = mod_ref[0, row0 + 1:row0 + 2, :]
    gate = mod_ref[0, row0 + 2:row0 + 3, :]
    xm = _norm_mod(x, g_ref[...], shift, scale).astype(BF16)
    a = _dot(xm, wg_ref[...])
    u = _dot(xm, wu_ref[...])
    act = (_silu(a) * u).astype(BF16)
    y = x + (0.5 * gate) * _dot(act, wd_ref[...])
    if final:
        ms = jnp.mean(y * y, axis=-1, keepdims=True)
        y = (y * lax.rsqrt(ms + NORM_EPS)) * gf_ref[...]
    o_ref[0] = y


def _ffn(x, mod, g, wg, wu, wd, *, row0, tm, final_g=None):
    groups, t, d = x.shape
    dff = wg.shape[1]
    final = final_g is not None
    in_specs = [
        pl.BlockSpec((1, tm, d), lambda b, i: (b, i, 0)),
        pl.BlockSpec((1, N_MOD, d), lambda b, i: (b, 0, 0)),
        _resident((1, d)),
        _resident((d, dff)),
        _resident((d, dff)),
        _resident((dff, d)),
    ]
    args = [x, mod, g.reshape(1, d), wg, wu, wd]
    if final:
        in_specs.append(_resident((1, d)))
        args.append(final_g.reshape(1, d))
    return pl.pallas_call(
        functools.partial(_ffn_kernel, row0=row0, final=final),
        grid=(groups, t // tm),
        in_specs=in_specs,
        out_specs=pl.BlockSpec((1, tm, d), lambda b, i: (b, i, 0)),
        out_shape=jax.ShapeDtypeStruct((groups, t, d), F32),
        compiler_params=_params("parallel", "parallel"),
        name="ffn",
    )(*args)


def _rope_slab(t, cos, sin, lane):
    fwd = pltpu.roll(t, LANE - 16, axis=1)
    bwd = pltpu.roll(t, 16, axis=1)
    partner = jnp.where((lane % 32) < 16, fwd, bwd)
    return t * cos + partner * sin


def _inproj_kernel(x_ref, mod_ref, g_ref, w_ref, cos_ref, sin_ref, gq_ref, gk_ref, msm_ref,
                   cs_hi_ref, cs_lo_ref, wbd_ref,
                   qa_ref, ka_ref, vat_ref, qb_ref, kb_ref, vbt_ref, z_ref, *, rope):
    x = x_ref[0]
    tm = x.shape[0]
    shift = mod_ref[0, 3:4, :]
    scale = mod_ref[0, 4:5, :]
    xm = _norm_mod(x, g_ref[...], shift, scale).astype(BF16)
    u = _dot(xm, w_ref[...])

    lane = lax.broadcasted_iota(jnp.int32, (tm, LANE), 1)
    low = lane < HEAD_DIM
    if rope:
        cos = cos_ref[...]
        sin = sin_ref[...]

    def qk_norm(t, gain):
        sq_hi, sq_lo = _split3(t * t)
        ms = _dot(sq_hi, msm_ref[...]) + _dot(sq_lo, msm_ref[...])
        return t * lax.rsqrt(ms + NORM_EPS) * gain

    def finish(t):
        return _rope_slab(t, cos, sin, lane) if rope else t

    def emit_q(q_ref, slabs):
        for h in range(N_HEADS):
            t = slabs[h // 2] * (LOG2E * HEAD_DIM ** -0.5)
            if (h % 2) != (h // GROUP):
                t = pltpu.roll(t, HEAD_DIM, axis=1)
            keep = low if (h // GROUP) == 0 else jnp.logical_not(low)
            q_ref[0, h] = jnp.where(keep, t, 0.0).astype(BF16)

    qa = [finish(u[:, LANE * i:LANE * (i + 1)]) for i in range(3)]
    emit_q(qa_ref, qa)
    ka_ref[0] = finish(u[:, 384:512]).astype(BF16)
    vat_ref[0] = u[:, 512:640].T.astype(BF16)

    qb = [finish(qk_norm(u[:, 640 + LANE * i:640 + LANE * (i + 1)], gq_ref[...])) for i in range(3)]
    emit_q(qb_ref, qb)
    kb_ref[0] = finish(qk_norm(u[:, 1024:1152], gk_ref[...])).astype(BF16)
    vbt_ref[0] = u[:, 1152:1280].T.astype(BF16)

    uf_hi, uf_lo = _split3(u[:, 1280:1536])
    f = _dot3(uf_hi, uf_lo, cs_hi_ref[...], cs_lo_ref[...])
    wbd = wbd_ref[...]
    z_ref[0, :, 0:C_WIDTH] = _dot(f[:, 0:C_WIDTH].astype(BF16), wbd)
    z_ref[0, :, C_WIDTH:] = _dot(f[:, C_WIDTH:].astype(BF16), wbd)


def _inproj(x, mod, g, w_in, cos, sin, gq, gk, msm, cs_hi, cs_lo, wbd, *, tm, rope):
    groups, t, d = x.shape
    din = w_in.shape[1]
    tok = lambda b, i: (b, i, 0)
    in_specs = [
        pl.BlockSpec((1, tm, d), tok),
        pl.BlockSpec((1, N_MOD, d), lambda b, i: (b, 0, 0)),
        _resident((1, d)),
        _resident((d, din)),
        pl.BlockSpec((tm, LANE), lambda b, i: (i, 0)),
        pl.BlockSpec((tm, LANE), lambda b, i: (i, 0)),
        _resident((1, LANE)),
        _resident((1, LANE)),
        _resident((LANE, LANE)),
        _resident((C_WIDTH, 2 * C_WIDTH)),
        _resident((C_WIDTH, 2 * C_WIDTH)),
        _resident((C_WIDTH, C_WIDTH)),
    ]
    q_spec = pl.BlockSpec((1, N_HEADS, tm, LANE), lambda b, i: (b, 0, i, 0))
    k_spec = pl.BlockSpec((1, tm, LANE), tok)
    vt_spec = pl.BlockSpec((1, LANE, tm), lambda b, i: (b, 0, i))
    q_shape = jax.ShapeDtypeStruct((groups, N_HEADS, t, LANE), BF16)
    k_shape = jax.ShapeDtypeStruct((groups, t, LANE), BF16)
    vt_shape = jax.ShapeDtypeStruct((groups, LANE, t), BF16)
    return pl.pallas_call(
        functools.partial(_inproj_kernel, rope=rope),
        grid=(groups, t // tm),
        in_specs=in_specs,
        out_specs=[q_spec, k_spec, vt_spec, q_spec, k_spec, vt_spec,
                   pl.BlockSpec((1, tm, 2 * C_WIDTH), tok)],
        out_shape=[q_shape, k_shape, vt_shape, q_shape, k_shape, vt_shape,
                   jax.ShapeDtypeStruct((groups, t, 2 * C_WIDTH), F32)],
        compiler_params=_params("parallel", "parallel"),
        name="inproj",
    )(x, mod, g.reshape(1, d), w_in, cos, sin, gq, gk, msm, cs_hi, cs_lo, wbd)


def _nt(a, b):
    return lax.dot_general(a, b, (((1,), (1,)), ((), ())), preferred_element_type=F32)


def _store_heads(o_ref, ot):
    for pair in range(N_HEADS // 2):
        slab = jnp.concatenate([ot[2 * pair], ot[2 * pair + 1]], axis=0)
        o_ref[0, :, pair * LANE:(pair + 1) * LANE] = slab.T.astype(o_ref.dtype)


def _attn_global_kernel(q_ref, k_ref, vt_ref, o_ref, s_a, s_b, p_a, p_b, m_sc, alpha_sc, acc_sc, *, tk):
    tq = q_ref.shape[2]
    nk = k_ref.shape[1] // tk
    gw = GROUP * tq
    q = q_ref[0].reshape(N_HEADS * tq, LANE)
    ones = jnp.ones((DEN_ROWS, tk), BF16)
    m_sc[...] = jnp.full_like(m_sc, NEG_INF)
    acc_sc[...] = jnp.zeros_like(acc_sc)

    def scores(t, s_buf):
        s_buf[...] = _nt(k_ref[0, pl.ds(pl.multiple_of(t * tk, tk), tk), :], q)

    def softmax(s_buf, p_buf, slot):
        s = s_buf[...]
        m_old = m_sc[...]
        m_new = jnp.maximum(m_old, jnp.max(s, axis=0, keepdims=True))
        m_sc[...] = m_new
        alpha_sc[slot] = jnp.exp2(m_old - m_new)
        p_buf[...] = jnp.exp2(s - m_new).astype(BF16)

    def values(t, p_buf, slot):
        off = pl.multiple_of(t * tk, tk)
        for g in range(2):
            vt = jnp.concatenate([vt_ref[0, g * HEAD_DIM:(g + 1) * HEAD_DIM, pl.ds(off, tk)], ones], axis=0)
            cols = slice(g * gw, (g + 1) * gw)
            acc_sc[g] = acc_sc[g] * alpha_sc[slot][:, cols] + _dot(vt, p_buf[:, cols])

    bufs = ((s_a, p_a), (s_b, p_b))

    def step(t, par, do_scores=True, do_values=True):
        (s_cur, p_cur), (s_oth, p_oth) = bufs[par], bufs[1 - par]
        if do_scores:
            scores(t + 1, s_oth)
        softmax(s_cur, p_cur, par)
        if do_values:
            values(t - 1, p_oth, 1 - par)

    scores(0, s_a)
    step(0, 0, do_values=False)
    n_pairs = (nk - 2) // 2

    def pair(i, carry):
        t = 1 + 2 * i
        step(t, 1)
        step(t + 1, 0)
        return carry

    lax.fori_loop(0, n_pairs, pair, 0)
    for t in range(1 + 2 * n_pairs, nk):
        step(t, t % 2, do_scores=t + 1 < nk)
    values(nk - 1, bufs[(nk - 1) % 2][1], (nk - 1) % 2)

    _store_heads(o_ref, [acc_sc[h // GROUP, 0:HEAD_DIM, (h % GROUP) * tq:(h % GROUP + 1) * tq]
                         / acc_sc[h // GROUP, HEAD_DIM:HEAD_DIM + 1, (h % GROUP) * tq:(h % GROUP + 1) * tq]
                         for h in range(N_HEADS)])


def _key_tile(n_keys, cap=1024):
    return max(t for t in range(LANE, cap + 1, LANE) if n_keys % t == 0)


def _attn_global(q, k, vt, *, tq, tk):
    b, _, l, _ = q.shape
    lk = k.shape[1]
    return pl.pallas_call(
        functools.partial(_attn_global_kernel, tk=tk),
        grid=(b, l // tq),
        in_specs=[
            pl.BlockSpec((1, N_HEADS, tq, LANE), lambda b, i: (b, 0, i, 0)),
            pl.BlockSpec((1, lk, LANE), lambda b, i: (b, 0, 0)),
            pl.BlockSpec((1, LANE, lk), lambda b, i: (b, 0, 0)),
        ],
        out_specs=pl.BlockSpec((1, tq, N_HEADS * HEAD_DIM), lambda b, i: (b, i, 0)),
        out_shape=jax.ShapeDtypeStruct((b, l, N_HEADS * HEAD_DIM), BF16),
        scratch_shapes=[
            pltpu.VMEM((tk, N_HEADS * tq), F32),
            pltpu.VMEM((tk, N_HEADS * tq), F32),
            pltpu.VMEM((tk, N_HEADS * tq), BF16),
            pltpu.VMEM((tk, N_HEADS * tq), BF16),
            pltpu.VMEM((1, N_HEADS * tq), F32),
            pltpu.VMEM((2, 1, N_HEADS * tq), F32),
            pltpu.VMEM((2, HEAD_DIM + DEN_ROWS, GROUP * tq), F32),
        ],
        compiler_params=_params("parallel", "parallel"),
        name="attn_global",
    )(q, k, vt)


def _softmax_pv(q_ref, ks, vts, extra, mask=None):
    tq = q_ref.shape[2]
    aug = [[jnp.concatenate([vt[g * HEAD_DIM:(g + 1) * HEAD_DIM, :],
                             jnp.ones((DEN_ROWS, vt.shape[1]), BF16)], axis=0) for vt in vts]
           for g in range(2)]
    out = []
    for h in range(N_HEADS):
        ss = [_nt(k, q_ref[0, h]) for k in ks]
        if mask is not None:
            ss[0] = jnp.where(mask, ss[0], NEG_INF)
        m = functools.reduce(jnp.maximum, [jnp.max(s, axis=0, keepdims=True) for s in ss])
        if extra is not None:
            e = extra[:, h * tq:(h + 1) * tq]
            m = jnp.maximum(m, e)
        acc = functools.reduce(
            jnp.add, [_dot(vt, jnp.exp2(s - m).astype(BF16)) for s, vt in zip(ss, aug[h // GROUP])])
        den = acc[HEAD_DIM:HEAD_DIM + 1]
        if extra is not None:
            den = den + jnp.exp2(e - m)
        out.append(acc[0:HEAD_DIM] / den)
    return out


def _attn_window_kernel(q_ref, k_ref, vt_ref, sink_ref, o_ref, *, length):
    span = BLOCK + 2 * WINDOW
    n = pl.program_id(1)
    start = pl.multiple_of(jnp.clip(n * BLOCK - WINDOW, 0, length - span), LANE)
    kpos = start + lax.broadcasted_iota(jnp.int32, (span, BLOCK), 0)
    qpos = n * BLOCK + lax.broadcasted_iota(jnp.int32, (span, BLOCK), 1)
    ks = [k_ref[0, pl.ds(start, span), :], k_ref[0, length:, :]]
    vts = [vt_ref[0, :, pl.ds(start, span)], vt_ref[0, :, length:]]
    _store_heads(o_ref, _softmax_pv(q_ref, ks, vts, sink_ref[...], jnp.abs(kpos - qpos) <= WINDOW))


def _attn_window(q, k, vt, sink_row):
    b, _, l, _ = q.shape
    lk = k.shape[1]
    return pl.pallas_call(
        functools.partial(_attn_window_kernel, length=l),
        grid=(b, l // BLOCK),
        in_specs=[
            pl.BlockSpec((1, N_HEADS, BLOCK, LANE), lambda b, i: (b, 0, i, 0)),
            pl.BlockSpec((1, lk, LANE), lambda b, i: (b, 0, 0)),
            pl.BlockSpec((1, LANE, lk), lambda b, i: (b, 0, 0)),
            pl.BlockSpec((1, N_HEADS * BLOCK), lambda b, i: (0, 0)),
        ],
        out_specs=pl.BlockSpec((1, BLOCK, N_HEADS * HEAD_DIM), lambda b, i: (b, i, 0)),
        out_shape=jax.ShapeDtypeStruct((b, l, N_HEADS * HEAD_DIM), BF16),
        compiler_params=_params("parallel", "parallel"),
        name="attn_window",
    )(q, k, vt, sink_row)


def _attn_ctx_kernel(q_ref, k_ref, vt_ref, *rest, has_sink):
    if has_sink:
        sink_ref, o_ref = rest
    else:
        (o_ref,) = rest
    extra = sink_ref[...] if has_sink else None
    _store_heads(o_ref, _softmax_pv(q_ref, [k_ref[0]], [vt_ref[0]], extra))


def _attn_ctx(q, k, vt, sink_row, *, batch):
    lc = q.shape[2] // batch
    in_specs = [
        pl.BlockSpec((1, N_HEADS, lc, LANE), lambda b: (0, 0, b, 0)),
        pl.BlockSpec((1, lc, LANE), lambda b: (0, b, 0)),
        pl.BlockSpec((1, LANE, lc), lambda b: (0, 0, b)),
    ]
    args = [q, k, vt]
    if sink_row is not None:
        in_specs.append(pl.BlockSpec((1, N_HEADS * lc), lambda b: (0, 0)))
        args.append(sink_row)
    return pl.pallas_call(
        functools.partial(_attn_ctx_kernel, has_sink=sink_row is not None),
        grid=(batch,),
        in_specs=in_specs,
        out_specs=pl.BlockSpec((1, lc, N_HEADS * HEAD_DIM), lambda b: (0, b, 0)),
        out_shape=jax.ShapeDtypeStruct((1, batch * lc, N_HEADS * HEAD_DIM), BF16),
        compiler_params=_params("parallel"),
        name="attn_ctx",
    )(*args)


def _fft1_kernel(x_ref, ch_ref, cl_ref, sh_ref, sl_ref, twc_ref, tws_ref, o_ref):
    x_hi, x_lo = _split3(x_ref[0])
    cx = _dot3(ch_ref[...], cl_ref[...], x_hi, x_lo)
    sx = _dot3(sh_ref[...], sl_ref[...], x_hi, x_lo)
    w = 2 * C_WIDTH
    for i in range(twc_ref.shape[0]):
        tc = jnp.concatenate([twc_ref[i]] * (C_WIDTH // LANE), axis=1)
        ts = jnp.concatenate([tws_ref[i]] * (C_WIDTH // LANE), axis=1)
        re, im = slice(i * w, i * w + C_WIDTH), slice(i * w + C_WIDTH, (i + 1) * w)
        ar = cx[:, re] + sx[:, im]
        ai = cx[:, im] - sx[:, re]
        o_ref[0, :, re] = ar * tc + ai * ts
        o_ref[0, :, im] = ai * tc - ar * ts


def _fft2_kernel(x_ref, ch_ref, cl_ref, sh_ref, sl_ref, o_ref):
    for i in range(x_ref.shape[1]):
        r_hi, r_lo = _split3(x_ref[0, i, :, 0:C_WIDTH])
        i_hi, i_lo = _split3(x_ref[0, i, :, C_WIDTH:])
        o_ref[0, :, i * C_WIDTH:(i + 1) * C_WIDTH] = (
            _dot3(ch_ref[...], cl_ref[...], r_hi, r_lo) + _dot3(sh_ref[...], sl_ref[...], i_hi, i_lo))


def _dft_tables(n, norm):
    k = np.arange(n, dtype=np.int64)
    ang = 2.0 * np.pi * ((k[:, None] * k[None, :]) % n).astype(np.float64) / n
    c = jnp.asarray(np.cos(ang) * norm, F32)
    s = jnp.asarray(np.sin(ang) * norm, F32)
    return _split3(c) + _split3(s)


def _fourier_latent(z):
    b, l, w = z.shape
    n1, n2 = FFT_N1, l // FFT_N1
    tn2, tk1 = 8, 8
    prod = jnp.arange(n2)[:, None] * jnp.arange(n1)[None, :]
    ang = prod.astype(F32) * (2.0 * np.pi / l)
    twc = jnp.broadcast_to(jnp.cos(ang)[:, :, None], (n2, n1, LANE))
    tws = jnp.broadcast_to(jnp.sin(ang)[:, :, None], (n2, n1, LANE))
    t1 = _dft_tables(n1, 1.0)
    y = pl.pallas_call(
        _fft1_kernel,
        grid=(b, n2 // tn2),
        in_specs=[pl.BlockSpec((1, n1, tn2 * w), lambda b, j: (b, 0, j))]
        + [_resident((n1, n1))] * 4
        + [pl.BlockSpec((tn2, n1, LANE), lambda b, j: (j, 0, 0))] * 2,
        out_specs=pl.BlockSpec((1, n1, tn2 * w), lambda b, j: (b, 0, j)),
        out_shape=jax.ShapeDtypeStruct((b, n1, n2 * w), F32),
        compiler_params=_params("parallel", "parallel"),
        name="fft_stage1",
    )(z.reshape(b, n1, n2 * w), *t1, twc, tws)
    t2m = _dft_tables(n2, 1.0 / np.sqrt(l))
    out = pl.pallas_call(
        _fft2_kernel,
        grid=(b, n1 // tk1),
        in_specs=[pl.BlockSpec((1, tk1, n2, w), lambda b, j: (b, j, 0, 0))] + [_resident((n2, n2))] * 4,
        out_specs=pl.BlockSpec((1, n2, tk1 * C_WIDTH), lambda b, j: (b, 0, j)),
        out_shape=jax.ShapeDtypeStruct((b, n2, n1 * C_WIDTH), F32),
        compiler_params=_params("parallel", "parallel"),
        name="fft_stage2",
    )(y.reshape(b, n1, n2, w), *t2m)
    return out.reshape(b, l, C_WIDTH)


def _fourier_ctx_kernel(x_ref, ch_ref, cl_ref, sh_ref, sl_ref, o_ref):
    r_hi, r_lo = _split3(x_ref[0, :, 0:C_WIDTH])
    i_hi, i_lo = _split3(x_ref[0, :, C_WIDTH:])
    o_ref[0] = _dot3(ch_ref[...], cl_ref[...], r_hi, r_lo) + _dot3(sh_ref[...], sl_ref[...], i_hi, i_lo)


def _fourier_ctx(z, *, batch):
    lc = z.shape[1] // batch
    tabs = _dft_tables(lc, 1.0 / np.sqrt(lc))
    return pl.pallas_call(
        _fourier_ctx_kernel,
        grid=(batch,),
        in_specs=[pl.BlockSpec((1, lc, 2 * C_WIDTH), lambda b: (0, b, 0))] + [_resident((lc, lc))] * 4,
        out_specs=pl.BlockSpec((1, lc, C_WIDTH), lambda b: (0, b, 0)),
        out_shape=jax.ShapeDtypeStruct((1, batch * lc, C_WIDTH), F32),
        compiler_params=_params("parallel"),
        name="fourier_ctx",
    )(z, *tabs)


def _outproj_kernel(h_ref, oa_ref, ob_ref, oc_ref, mod_ref, w_ref, o_ref):
    na = N_HEADS * HEAD_DIM
    y = (_dot(oa_ref[0], w_ref[0:na, :]) + _dot(ob_ref[0], w_ref[na:2 * na, :])
         + _dot(oc_ref[0].astype(BF16), w_ref[2 * na:, :]))
    o_ref[0] = h_ref[0] + mod_ref[0, 5:6, :] * y


def _outproj(h, oa, ob, oc, mod, w_out, *, tm):
    groups, t, d = h.shape
    na = N_HEADS * HEAD_DIM
    tok = lambda b, i: (b, i, 0)
    return pl.pallas_call(
        _outproj_kernel,
        grid=(groups, t // tm),
        in_specs=[
            pl.BlockSpec((1, tm, d), tok),
            pl.BlockSpec((1, tm, na), tok),
            pl.BlockSpec((1, tm, na), tok),
            pl.BlockSpec((1, tm, C_WIDTH), tok),
            pl.BlockSpec((1, N_MOD, d), lambda b, i: (b, 0, 0)),
            _resident(w_out.shape),
        ],
        out_specs=pl.BlockSpec((1, tm, d), tok),
        out_shape=jax.ShapeDtypeStruct((groups, t, d), F32),
        compiler_params=_params("parallel", "parallel"),
        name="outproj",
    )(h, oa, ob, oc, mod, w_out)


def _rope_tables(length):
    t = jnp.arange(length)
    row = (t // GRID_W).astype(F32)
    col = (t % GRID_W).astype(F32)
    half = HEAD_DIM // 2
    inv = ROPE_THETA ** (-jnp.arange(0, half, 2, dtype=F32) / half)
    ar, ac = row[:, None] * inv, col[:, None] * inv
    cos = jnp.concatenate([jnp.cos(ar), jnp.cos(ar), jnp.cos(ac), jnp.cos(ac)], axis=-1)
    sin = jnp.concatenate([-jnp.sin(ar), jnp.sin(ar), -jnp.sin(ac), jnp.sin(ac)], axis=-1)
    return jnp.tile(cos, (1, 2)), jnp.tile(sin, (1, 2))


def _channel_dft():
    k = np.arange(C_GROUP_DIM)
    ang = 2.0 * np.pi * ((k[:, None] * k[None, :]) % C_GROUP_DIM) / C_GROUP_DIM
    eye = np.eye(C_GROUPS)
    norm = 1.0 / np.sqrt(C_GROUP_DIM)
    cs = np.concatenate([np.kron(eye, np.cos(ang) * norm), np.kron(eye, -np.sin(ang) * norm)], axis=1)
    return _split3(jnp.asarray(cs, F32))


def _block_diag(w):
    g, c, _ = w.shape
    eye = jnp.eye(g, dtype=w.dtype)
    return (eye[:, None, :, None] * w[:, :, None, :]).reshape(g * c, g * c)


def kernel(x, c, ctx, c_ctx, w_ada, b_ada, g_ffn1, g_mix, g_ffn2, w_in, g_qn, g_kn, sink, w_four,
           w_out, w1_gate, w1_up, w1_down, w2_gate, w2_up, w2_down, g_final):
    batch, length, d = x.shape
    lc = ctx.shape[1]
    depth = w_ada.shape[0]
    assert length % (FFT_N1 * 8) == 0 and length >= BLOCK + 2 * WINDOW and batch + 1 <= 8
    tm = 512
    tm_ctx = batch * lc

    cvec = jnp.zeros((8, d), F32).at[:batch].set(c).at[batch].set(c_ctx)
    mod = _ada(cvec, w_ada, b_ada).reshape(depth, 8, N_MOD, d)

    cos, sin = _rope_tables(length)
    ones = jnp.ones((tm_ctx, LANE), F32)
    msm = _block_diag(jnp.full((2, HEAD_DIM, HEAD_DIM), 1.0 / HEAD_DIM, F32)).astype(BF16)
    cs_hi, cs_lo = _channel_dft()

    h = x
    hc = ctx.reshape(1, batch * lc, d)
    for l in range(depth):
        last = l == depth - 1
        mod_l, mod_c = mod[l, :batch], mod[l, batch:batch + 1]
        bf = lambda w: w[l].astype(BF16)
        w1 = (bf(w1_gate), bf(w1_up), bf(w1_down))
        w2 = (bf(w2_gate), bf(w2_up), bf(w2_down))
        win, wout = bf(w_in), bf(w_out)
        gq = jnp.tile(g_qn[l].reshape(1, HEAD_DIM), (1, 2))
        gk = jnp.tile(g_kn[l].reshape(1, HEAD_DIM), (1, 2))
        wbd = _block_diag(w_four[l]).astype(BF16)
        proj = functools.partial(_inproj, g=g_mix[l], w_in=win, gq=gq, gk=gk, msm=msm,
                                 cs_hi=cs_hi, cs_lo=cs_lo, wbd=wbd)

        h = _ffn(h, mod_l, g_ffn1[l], *w1, row0=0, tm=tm)
        hc = _ffn(hc, mod_c, g_ffn1[l], *w1, row0=0, tm=tm_ctx)

        qa, ka, vat, qb, kb, vbt, z = proj(h, mod_l, cos=cos, sin=sin, tm=tm, rope=True)
        qa_c, ka_c, vat_c, qb_c, kb_c, vbt_c, z_c = proj(hc, mod_c, cos=ones, sin=ones, tm=tm_ctx, rope=False)

        cat_k = lambda k, k_c: jnp.concatenate([k, k_c.reshape(batch, lc, LANE)], axis=1)
        cat_v = lambda v, v_c: jnp.concatenate(
            [v, v_c.reshape(LANE, batch, lc).transpose(1, 0, 2)], axis=2)
        sink_l = jnp.repeat(sink[l].astype(F32) * LOG2E, BLOCK).reshape(1, N_HEADS * BLOCK)
        oa = _attn_window(qa, cat_k(ka, ka_c), cat_v(vat, vat_c), sink_l)
        ob = _attn_global(qb, cat_k(kb, kb_c), cat_v(vbt, vbt_c), tq=256, tk=_key_tile(length + lc))
        oc = _fourier_latent(z)
        h = _outproj(h, oa, ob, oc, mod_l, wout, tm=tm)
        h = _ffn(h, mod_l, g_ffn2[l], *w2, row0=6, tm=tm, final_g=g_final if last else None)

        if not last:
            sink_c = jnp.repeat(sink[l].astype(F32) * LOG2E, lc).reshape(1, N_HEADS * lc)
            oa_c = _attn_ctx(qa_c, ka_c, vat_c, sink_c, batch=batch)
            ob_c = _attn_ctx(qb_c, kb_c, vbt_c, None, batch=batch)
            oc_c = _fourier_ctx(z_c, batch=batch)
            hc = _outproj(hc, oa_c, ob_c, oc_c, mod_c, wout, tm=tm_ctx)
            hc = _ffn(hc, mod_c, g_ffn2[l], *w2, row0=6, tm=tm_ctx)
    return h
```

```python
import functools

import jax
import jax.numpy as jnp
import numpy as np
from jax import lax
from jax.experimental import pallas as pl
from jax.experimental.pallas import tpu as pltpu

HEAD_DIM = 64
A_HEADS = 6
A_KV_HEADS = 2
B_HEADS = 6
B_KV_HEADS = 2
C_GROUPS = 4
C_GROUP_DIM = 64
GRID_W = 64
WINDOW = 128
BLOCK = 128
ROPE_THETA = 10000.0
NORM_EPS = 1e-6
N_MOD = 9
NEG_INF = -1e30

N_HEADS = 6
GROUP = 3
C_WIDTH = C_GROUPS * C_GROUP_DIM
FFT_N1 = 64
DEN_ROWS = 16
LOG2E = 1.4426950408889634

LANE = 128
VMEM_LIMIT = 56 * 1024 * 1024

F32 = jnp.float32
BF16 = jnp.bfloat16


def _params(*sem, flags=None):
    return pltpu.CompilerParams(dimension_semantics=tuple(sem), vmem_limit_bytes=VMEM_LIMIT, flags=flags)


def _resident(shape):
    nd = len(shape)
    return pl.BlockSpec(shape, lambda *_: (0,) * nd, pipeline_mode=pl.Buffered(1))


def _split3(x):
    hi = x.astype(BF16)
    lo = (x - hi.astype(F32)).astype(BF16)
    return hi, lo


def _dot(a, b):
    return jnp.dot(a, b, preferred_element_type=F32)


def _dot3(a_hi, a_lo, b_hi, b_lo):
    return _dot(a_hi, b_hi) + _dot(a_hi, b_lo) + _dot(a_lo, b_hi)


def _silu(x):
    return x / (1.0 + jnp.exp(-x))


def _norm_mod(x, g, shift, scale):
    ms = jnp.mean(x * x, axis=-1, keepdims=True)
    return (x * lax.rsqrt(ms + NORM_EPS)) * (g * (1.0 + scale)) + shift


def _ada_kernel(c_ref, w_ref, b_ref, o_ref):
    s_hi, s_lo = _split3(_silu(c_ref[...]))
    w_hi, w_lo = _split3(w_ref[0])
    o_ref[0] = _dot3(s_hi, s_lo, w_hi, w_lo) + b_ref[0]


def _ada(cvec, w_ada, b_ada):
    depth, d, nd = w_ada.shape
    tn = 1024
    return pl.pallas_call(
        _ada_kernel,
        grid=(depth, nd // tn),
        in_specs=[
            pl.BlockSpec((8, d), lambda l, j: (0, 0)),
            pl.BlockSpec((1, d, tn), lambda l, j: (l, 0, j)),
            pl.BlockSpec((1, 1, tn), lambda l, j: (l, 0, j)),
        ],
        out_specs=pl.BlockSpec((1, 8, tn), lambda l, j: (l, 0, j)),
        out_shape=jax.ShapeDtypeStruct((depth, 8, nd), F32),
        compiler_params=_params("parallel", "parallel"),
        name="ada",
    )(cvec, w_ada, b_ada.reshape(depth, 1, nd))


def _ffn_kernel(x_ref, mod_ref, g_ref, wg_ref, wu_ref, wd_ref, *rest, row0, final):
    if final:
        gf_ref, o_ref = rest
    else:
        (o_ref,) = rest
    x = x_ref[0]
    shift = mod_ref[0, row0:row0 + 1, :]
    scale = mod_ref[0, row0 + 1:row0 + 2, :]
    gate = mod_ref[0, row0 + 2:row0 + 3, :]
    xm = _norm_mod(x, g_ref[...], shift, scale).astype(BF16)
    a = _dot(xm, wg_ref[...])
    u = _dot(xm, wu_ref[...])
    act = (_silu(a) * u).astype(BF16)
    y = x + (0.5 * gate) * _dot(act, wd_ref[...])
    if final:
        ms = jnp.mean(y * y, axis=-1, keepdims=True)
        y = (y * lax.rsqrt(ms + NORM_EPS)) * gf_ref[...]
    o_ref[0] = y


def _ffn(x, mod, g, wg, wu, wd, *, row0, tm, final_g=None):
    groups, t, d = x.shape
    dff = wg.shape[1]
    final = final_g is not None
    in_specs = [
        pl.BlockSpec((1, tm, d), lambda b, i: (b, i, 0)),
        pl.BlockSpec((1, N_MOD, d), lambda b, i: (b, 0, 0)),
        _resident((1, d)),
        _resident((d, dff)),
        _resident((d, dff)),
        _resident((dff, d)),
    ]
    args = [x, mod, g.reshape(1, d), wg, wu, wd]
    if final:
        in_specs.append(_resident((1, d)))
        args.append(final_g.reshape(1, d))
    return pl.pallas_call(
        functools.partial(_ffn_kernel, row0=row0, final=final),
        grid=(groups, t // tm),
        in_specs=in_specs,
        out_specs=pl.BlockSpec((1, tm, d), lambda b, i: (b, i, 0)),
        out_shape=jax.ShapeDtypeStruct((groups, t, d), F32),
        compiler_params=_params("parallel", "parallel"),
        name="ffn",
    )(*args)


def _rope_slab(t, cos, sin, lane):
    fwd = pltpu.roll(t, LANE - 16, axis=1)
    bwd = pltpu.roll(t, 16, axis=1)
    partner = jnp.where((lane % 32) < 16, fwd, bwd)
    return t * cos + partner * sin


def _inproj_kernel(x_ref, mod_ref, g_ref, w_ref, cos_ref, sin_ref, gq_ref, gk_ref, msm_ref,
                   cs_hi_ref, cs_lo_ref, wbd_ref,
                   qa_ref, ka_ref, vat_ref, qb_ref, kb_ref, vbt_ref, z_ref, *, rope):
    x = x_ref[0]
    tm = x.shape[0]
    shift = mod_ref[0, 3:4, :]
    scale = mod_ref[0, 4:5, :]
    xm = _norm_mod(x, g_ref[...], shift, scale).astype(BF16)
    u = _dot(xm, w_ref[...])

    lane = lax.broadcasted_iota(jnp.int32, (tm, LANE), 1)
    low = lane < HEAD_DIM
    if rope:
        cos = cos_ref[...]
        sin = sin_ref[...]

    def qk_norm(t, gain):
        sq_hi, sq_lo = _split3(t * t)
        ms = _dot(sq_hi, msm_ref[...]) + _dot(sq_lo, msm_ref[...])
        return t * lax.rsqrt(ms + NORM_EPS) * gain

    def finish(t):
        return _rope_slab(t, cos, sin, lane) if rope else t

    def emit_q(q_ref, slabs):
        for h in range(N_HEADS):
            t = slabs[h // 2] * (LOG2E * HEAD_DIM ** -0.5)
            if (h % 2) != (h // GROUP):
                t = pltpu.roll(t, HEAD_DIM, axis=1)
            keep = low if (h // GROUP) == 0 else jnp.logical_not(low)
            q_ref[0, h] = jnp.where(keep, t, 0.0).astype(BF16)

    qa = [finish(u[:, LANE * i:LANE * (i + 1)]) for i in range(3)]
    emit_q(qa_ref, qa)
    ka_ref[0] = finish(u[:, 384:512]).astype(BF16)
    vat_ref[0] = u[:, 512:640].T.astype(BF16)

    qb = [finish(qk_norm(u[:, 640 + LANE * i:640 + LANE * (i + 1)], gq_ref[...])) for i in range(3)]
    emit_q(qb_ref, qb)
    kb_ref[0] = finish(qk_norm(u[:, 1024:1152], gk_ref[...])).astype(BF16)
    vbt_ref[0] = u[:, 1152:1280].T.astype(BF16)

    uf_hi, uf_lo = _split3(u[:, 1280:1536])
    f = _dot3(uf_hi, uf_lo, cs_hi_ref[...], cs_lo_ref[...])
    wbd = wbd_ref[...]
    z_ref[0, :, 0:C_WIDTH] = _dot(f[:, 0:C_WIDTH].astype(BF16), wbd)
    z_ref[0, :, C_WIDTH:] = _dot(f[:, C_WIDTH:].astype(BF16), wbd)


def _inproj(x, mod, g, w_in, cos, sin, gq, gk, msm, cs_hi, cs_lo, wbd, *, tm, rope):
    groups, t, d = x.shape
    din = w_in.shape[1]
    tok = lambda b, i: (b, i, 0)
    in_specs = [
        pl.BlockSpec((1, tm, d), tok),
        pl.BlockSpec((1, N_MOD, d), lambda b, i: (b, 0, 0)),
        _resident((1, d)),
        _resident((d, din)),
        pl.BlockSpec((tm, LANE), lambda b, i: (i, 0)),
        pl.BlockSpec((tm, LANE), lambda b, i: (i, 0)),
        _resident((1, LANE)),
        _resident((1, LANE)),
        _resident((LANE, LANE)),
        _resident((C_WIDTH, 2 * C_WIDTH)),
        _resident((C_WIDTH, 2 * C_WIDTH)),
        _resident((C_WIDTH, C_WIDTH)),
    ]
    q_spec = pl.BlockSpec((1, N_HEADS, tm, LANE), lambda b, i: (b, 0, i, 0))
    k_spec = pl.BlockSpec((1, tm, LANE), tok)
    vt_spec = pl.BlockSpec((1, LANE, tm), lambda b, i: (b, 0, i))
    q_shape = jax.ShapeDtypeStruct((groups, N_HEADS, t, LANE), BF16)
    k_shape = jax.ShapeDtypeStruct((groups, t, LANE), BF16)
    vt_shape = jax.ShapeDtypeStruct((groups, LANE, t), BF16)
    return pl.pallas_call(
        functools.partial(_inproj_kernel, rope=rope),
        grid=(groups, t // tm),
        in_specs=in_specs,
        out_specs=[q_spec, k_spec, vt_spec, q_spec, k_spec, vt_spec,
                   pl.BlockSpec((1, tm, 2 * C_WIDTH), tok)],
        out_shape=[q_shape, k_shape, vt_shape, q_shape, k_shape, vt_shape,
                   jax.ShapeDtypeStruct((groups, t, 2 * C_WIDTH), F32)],
        compiler_params=_params("parallel", "parallel"),
        name="inproj",
    )(x, mod, g.reshape(1, d), w_in, cos, sin, gq, gk, msm, cs_hi, cs_lo, wbd)


def _nt(a, b):
    return lax.dot_general(a, b, (((1,), (1,)), ((), ())), preferred_element_type=F32)


def _store_heads(o_ref, ot):
    for pair in range(N_HEADS // 2):
        slab = jnp.concatenate([ot[2 * pair], ot[2 * pair + 1]], axis=0)
        o_ref[0, :, pair * LANE:(pair + 1) * LANE] = slab.T.astype(o_ref.dtype)


def _attn_latent_kernel(qa_ref, qb_ref, ka_ref, vat_ref, kb_ref, vbt_ref, sink_ref, oa_ref, ob_ref,
                        s_a, s_b, p_a, p_b, m_sc, bmax_sc, alpha_sc, acc_sc, *, tk, lc):
    tq = qb_ref.shape[2]
    lk = kb_ref.shape[1]
    length = lk - lc
    nk = lk // tk
    span = tq + 2 * WINDOW
    na = span + lc
    gw = GROUP * tq
    qa = qa_ref[0].reshape(N_HEADS * tq, LANE)
    qb = qb_ref[0].reshape(N_HEADS * tq, LANE)
    tile = pl.program_id(1)
    start = pl.multiple_of(jnp.clip(tile * tq - WINDOW, 0, length - span), LANE)
    m_sc[...] = jnp.full_like(m_sc, NEG_INF)
    acc_sc[...] = jnp.zeros_like(acc_sc)

    def with_ones(vt):
        return jnp.concatenate([vt, jnp.ones((DEN_ROWS, vt.shape[1]), BF16)], axis=0)

    def rows(g):
        return slice(g * HEAD_DIM, (g + 1) * HEAD_DIM)

    def cols(g):
        return slice(g * gw, (g + 1) * gw)

    def scores_win(s_buf, slot):
        s_w = _nt(ka_ref[0, pl.ds(start, span), :], qa)
        kpos = start + lax.broadcasted_iota(jnp.int32, s_w.shape, 0)
        qpos = tile * tq + lax.broadcasted_iota(jnp.int32, s_w.shape, 1) % tq
        s_w = jnp.where(jnp.abs(kpos - qpos) <= WINDOW, s_w, NEG_INF)
        s_c = _nt(ka_ref[0, length:, :], qa)
        s_buf[0:span] = s_w
        s_buf[span:na] = s_c
        bmax_sc[slot] = jnp.maximum(jnp.max(s_w, axis=0, keepdims=True), jnp.max(s_c, axis=0, keepdims=True))

    def softmax_win(s_buf, p_buf, slot):
        sink = sink_ref[...]
        m = jnp.maximum(sink, bmax_sc[slot])
        alpha_sc[slot] = jnp.exp2(sink - m)
        p_buf[0:na] = jnp.exp2(s_buf[0:na] - m).astype(BF16)

    def values_win(p_buf, slot):
        heads = []
        for g in range(2):
            vt = jnp.concatenate([vat_ref[0, rows(g), pl.ds(start, span)], vat_ref[0, rows(g), length:]], axis=1)
            acc = _dot(with_ones(vt), p_buf[0:na, cols(g)])
            o = acc[0:HEAD_DIM] / (acc[HEAD_DIM:HEAD_DIM + 1] + alpha_sc[slot][:, cols(g)])
            heads += [o[:, j * tq:(j + 1) * tq] for j in range(GROUP)]
        _store_heads(oa_ref, heads)

    def scores(blk, s_buf, slot):
        s = _nt(kb_ref[0, pl.ds(pl.multiple_of(blk * tk, tk), tk), :], qb)
        s_buf[0:tk] = s
        bmax_sc[slot] = jnp.max(s, axis=0, keepdims=True)

    def softmax(s_buf, p_buf, slot):
        m_old = m_sc[...]
        m_new = jnp.maximum(m_old, bmax_sc[slot])
        m_sc[...] = m_new
        alpha_sc[slot] = jnp.exp2(m_old - m_new)
        p_buf[0:tk] = jnp.exp2(s_buf[0:tk] - m_new).astype(BF16)

    def values(blk, p_buf, slot):
        off = pl.multiple_of(blk * tk, tk)
        for g in range(2):
            vt = with_ones(vbt_ref[0, rows(g), pl.ds(off, tk)])
            acc_sc[g] = acc_sc[g] * alpha_sc[slot][:, cols(g)] + _dot(vt, p_buf[0:tk, cols(g)])

    bufs = ((s_a, p_a), (s_b, p_b))
    n_steps = nk + 1

    def step(t, par, do_scores=True):
        (s_cur, p_cur), (s_oth, p_oth) = bufs[par], bufs[1 - par]
        if do_scores:
            scores(t, s_oth, 1 - par)
        softmax(s_cur, p_cur, par)
        values(t - 2, p_oth, 1 - par)

    scores_win(s_a, 0)
    scores(0, s_b, 1)
    softmax_win(s_a, p_a, 0)
    if nk >= 2:
        scores(1, s_a, 0)
    softmax(s_b, p_b, 1)
    values_win(p_a, 0)
    n_pairs = (n_steps - 3) // 2

    def pair(i, carry):
        t = 2 + 2 * i
        step(t, 0)
        step(t + 1, 1)
        return carry

    lax.fori_loop(0, n_pairs, pair, 0)
    for t in range(2 + 2 * n_pairs, n_steps):
        step(t, t % 2, do_scores=t + 1 < n_steps)
    last = (n_steps - 1) % 2
    values(nk - 1, bufs[last][1], last)

    _store_heads(ob_ref, [acc_sc[h // GROUP, 0:HEAD_DIM, (h % GROUP) * tq:(h % GROUP + 1) * tq]
                          / acc_sc[h // GROUP, HEAD_DIM:HEAD_DIM + 1, (h % GROUP) * tq:(h % GROUP + 1) * tq]
                          for h in range(N_HEADS)])


def _key_tile(n_keys, cap=1024):
    return max(t for t in range(LANE, cap + 1, LANE) if n_keys % t == 0)


def _attn_latent(qa, ka, vat, qb, kb, vbt, sink_row, *, tq, tk, lc):
    b, _, l, _ = qb.shape
    lk = kb.shape[1]
    buf_rows = max(tk, tq + 2 * WINDOW + lc)
    q_spec = pl.BlockSpec((1, N_HEADS, tq, LANE), lambda b, i: (b, 0, i, 0))
    k_spec = pl.BlockSpec((1, lk, LANE), lambda b, i: (b, 0, 0))
    vt_spec = pl.BlockSpec((1, LANE, lk), lambda b, i: (b, 0, 0))
    o_spec = pl.BlockSpec((1, tq, N_HEADS * HEAD_DIM), lambda b, i: (b, i, 0))
    o_shape = jax.ShapeDtypeStruct((b, l, N_HEADS * HEAD_DIM), BF16)
    return pl.pallas_call(
        functools.partial(_attn_latent_kernel, tk=tk, lc=lc),
        grid=(b, l // tq),
        in_specs=[q_spec, q_spec, k_spec, vt_spec, k_spec, vt_spec,
                  pl.BlockSpec((1, N_HEADS * tq), lambda b, i: (0, 0))],
        out_specs=[o_spec, o_spec],
        out_shape=[o_shape, o_shape],
        scratch_shapes=[
            pltpu.VMEM((buf_rows, N_HEADS * tq), F32),
            pltpu.VMEM((buf_rows, N_HEADS * tq), F32),
            pltpu.VMEM((buf_rows, N_HEADS * tq), BF16),
            pltpu.VMEM((buf_rows, N_HEADS * tq), BF16),
            pltpu.VMEM((1, N_HEADS * tq), F32),
            pltpu.VMEM((2, 1, N_HEADS * tq), F32),
            pltpu.VMEM((2, 1, N_HEADS * tq), F32),
            pltpu.VMEM((2, HEAD_DIM + DEN_ROWS, GROUP * tq), F32),
        ],
        compiler_params=_params("parallel", "parallel"),
        name="attn_latent",
    )(qa, qb, ka, vat, kb, vbt, sink_row)


def _softmax_pv(q_ref, ks, vts, extra, mask=None):
    tq = q_ref.shape[2]
    aug = [[jnp.concatenate([vt[g * HEAD_DIM:(g + 1) * HEAD_DIM, :],
                             jnp.ones((DEN_ROWS, vt.shape[1]), BF16)], axis=0) for vt in vts]
           for g in range(2)]
    out = []
    for h in range(N_HEADS):
        ss = [_nt(k, q_ref[0, h]) for k in ks]
        if mask is not None:
            ss[0] = jnp.where(mask, ss[0], NEG_INF)
        m = functools.reduce(jnp.maximum, [jnp.max(s, axis=0, keepdims=True) for s in ss])
        if extra is not None:
            e = extra[:, h * tq:(h + 1) * tq]
            m = jnp.maximum(m, e)
        acc = functools.reduce(
            jnp.add, [_dot(vt, jnp.exp2(s - m).astype(BF16)) for s, vt in zip(ss, aug[h // GROUP])])
        den = acc[HEAD_DIM:HEAD_DIM + 1]
        if extra is not None:
            den = den + jnp.exp2(e - m)
        out.append(acc[0:HEAD_DIM] / den)
    return out


def _attn_ctx_kernel(q_ref, k_ref, vt_ref, *rest, has_sink):
    if has_sink:
        sink_ref, o_ref = rest
    else:
        (o_ref,) = rest
    extra = sink_ref[...] if has_sink else None
    _store_heads(o_ref, _softmax_pv(q_ref, [k_ref[0]], [vt_ref[0]], extra))


def _attn_ctx(q, k, vt, sink_row, *, batch):
    lc = q.shape[2] // batch
    in_specs = [
        pl.BlockSpec((1, N_HEADS, lc, LANE), lambda b: (0, 0, b, 0)),
        pl.BlockSpec((1, lc, LANE), lambda b: (0, b, 0)),
        pl.BlockSpec((1, LANE, lc), lambda b: (0, 0, b)),
    ]
    args = [q, k, vt]
    if sink_row is not None:
        in_specs.append(pl.BlockSpec((1, N_HEADS * lc), lambda b: (0, 0)))
        args.append(sink_row)
    return pl.pallas_call(
        functools.partial(_attn_ctx_kernel, has_sink=sink_row is not None),
        grid=(batch,),
        in_specs=in_specs,
        out_specs=pl.BlockSpec((1, lc, N_HEADS * HEAD_DIM), lambda b: (0, b, 0)),
        out_shape=jax.ShapeDtypeStruct((1, batch * lc, N_HEADS * HEAD_DIM), BF16),
        compiler_params=_params("parallel"),
        name="attn_ctx",
    )(*args)


def _fft1_kernel(x_ref, ch_ref, cl_ref, sh_ref, sl_ref, twc_ref, tws_ref, o_ref):
    tn2 = x_ref.shape[2]
    xt = jnp.transpose(x_ref[0], (1, 0, 2))
    x_hi, x_lo = _split3(jnp.concatenate([xt[i] for i in range(tn2)], axis=1))
    cx = _dot3(ch_ref[...], cl_ref[...], x_hi, x_lo)
    sx = _dot3(sh_ref[...], sl_ref[...], x_hi, x_lo)
    w = 2 * C_WIDTH
    outs = []
    for i in range(tn2):
        tc = jnp.concatenate([twc_ref[i]] * (C_WIDTH // LANE), axis=1)
        ts = jnp.concatenate([tws_ref[i]] * (C_WIDTH // LANE), axis=1)
        re, im = slice(i * w, i * w + C_WIDTH), slice(i * w + C_WIDTH, (i + 1) * w)
        ar = cx[:, re] + sx[:, im]
        ai = cx[:, im] - sx[:, re]
        outs.append(jnp.concatenate([ar * tc + ai * ts, ai * tc - ar * ts], axis=1))
    o_ref[0] = jnp.transpose(jnp.stack(outs, axis=0), (1, 0, 2))


def _fft2_kernel(x_ref, ch_ref, cl_ref, sh_ref, sl_ref, o_ref):
    outs = []
    for i in range(x_ref.shape[1]):
        r_hi, r_lo = _split3(x_ref[0, i, :, 0:C_WIDTH])
        i_hi, i_lo = _split3(x_ref[0, i, :, C_WIDTH:])
        outs.append(_dot3(ch_ref[...], cl_ref[...], r_hi, r_lo) + _dot3(sh_ref[...], sl_ref[...], i_hi, i_lo))
    o_ref[0] = jnp.transpose(jnp.stack(outs, axis=0), (1, 0, 2))


def _dft_tables(n, norm):
    k = np.arange(n, dtype=np.int64)
    ang = 2.0 * np.pi * ((k[:, None] * k[None, :]) % n).astype(np.float64) / n
    c = jnp.asarray(np.cos(ang) * norm, F32)
    s = jnp.asarray(np.sin(ang) * norm, F32)
    return _split3(c) + _split3(s)


def _fourier_latent(z):
    b, l, w = z.shape
    n1, n2 = FFT_N1, l // FFT_N1
    tn2, tk1 = 8, 8
    prod = jnp.arange(n2)[:, None] * jnp.arange(n1)[None, :]
    ang = prod.astype(F32) * (2.0 * np.pi / l)
    twc = jnp.broadcast_to(jnp.cos(ang)[:, :, None], (n2, n1, LANE))
    tws = jnp.broadcast_to(jnp.sin(ang)[:, :, None], (n2, n1, LANE))
    t1 = _dft_tables(n1, 1.0)
    y = pl.pallas_call(
        _fft1_kernel,
        grid=(b, n2 // tn2),
        in_specs=[pl.BlockSpec((1, n1, tn2, w), lambda b, j: (b, 0, j, 0))]
        + [_resident((n1, n1))] * 4
        + [pl.BlockSpec((tn2, n1, LANE), lambda b, j: (j, 0, 0))] * 2,
        out_specs=pl.BlockSpec((1, n1, tn2, w), lambda b, j: (b, 0, j, 0)),
        out_shape=jax.ShapeDtypeStruct((b, n1, n2, w), F32),
        compiler_params=_params("parallel", "parallel"),
        name="fft_stage1",
    )(z.reshape(b, n1, n2, w), *t1, twc, tws)
    t2m = _dft_tables(n2, 1.0 / np.sqrt(l))
    out = pl.pallas_call(
        _fft2_kernel,
        grid=(b, n1 // tk1),
        in_specs=[pl.BlockSpec((1, tk1, n2, w), lambda b, j: (b, j, 0, 0))] + [_resident((n2, n2))] * 4,
        out_specs=pl.BlockSpec((1, n2, tk1, C_WIDTH), lambda b, j: (b, 0, j, 0)),
        out_shape=jax.ShapeDtypeStruct((b, n2, n1, C_WIDTH), F32),
        compiler_params=_params("parallel", "parallel"),
        name="fft_stage2",
    )(y, *t2m)
    return out.reshape(b, l, C_WIDTH)


def _fourier_ctx_kernel(x_ref, ch_ref, cl_ref, sh_ref, sl_ref, o_ref):
    r_hi, r_lo = _split3(x_ref[0, :, 0:C_WIDTH])
    i_hi, i_lo = _split3(x_ref[0, :, C_WIDTH:])
    o_ref[0] = _dot3(ch_ref[...], cl_ref[...], r_hi, r_lo) + _dot3(sh_ref[...], sl_ref[...], i_hi, i_lo)


def _fourier_ctx(z, *, batch):
    lc = z.shape[1] // batch
    tabs = _dft_tables(lc, 1.0 / np.sqrt(lc))
    return pl.pallas_call(
        _fourier_ctx_kernel,
        grid=(batch,),
        in_specs=[pl.BlockSpec((1, lc, 2 * C_WIDTH), lambda b: (0, b, 0))] + [_resident((lc, lc))] * 4,
        out_specs=pl.BlockSpec((1, lc, C_WIDTH), lambda b: (0, b, 0)),
        out_shape=jax.ShapeDtypeStruct((1, batch * lc, C_WIDTH), F32),
        compiler_params=_params("parallel"),
        name="fourier_ctx",
    )(z, *tabs)


def _outproj_kernel(h_ref, oa_ref, ob_ref, oc_ref, mod_ref, w_ref, o_ref):
    na = N_HEADS * HEAD_DIM
    y = (_dot(oa_ref[0], w_ref[0:na, :]) + _dot(ob_ref[0], w_ref[na:2 * na, :])
         + _dot(oc_ref[0].astype(BF16), w_ref[2 * na:, :]))
    o_ref[0] = h_ref[0] + mod_ref[0, 5:6, :] * y


def _outproj(h, oa, ob, oc, mod, w_out, *, tm):
    groups, t, d = h.shape
    na = N_HEADS * HEAD_DIM
    tok = lambda b, i: (b, i, 0)
    return pl.pallas_call(
        _outproj_kernel,
        grid=(groups, t // tm),
        in_specs=[
            pl.BlockSpec((1, tm, d), tok),
            pl.BlockSpec((1, tm, na), tok),
            pl.BlockSpec((1, tm, na), tok),
            pl.BlockSpec((1, tm, C_WIDTH), tok),
            pl.BlockSpec((1, N_MOD, d), lambda b, i: (b, 0, 0)),
            _resident(w_out.shape),
        ],
        out_specs=pl.BlockSpec((1, tm, d), tok),
        out_shape=jax.ShapeDtypeStruct((groups, t, d), F32),
        compiler_params=_params("parallel", "parallel"),
        name="outproj",
    )(h, oa, ob, oc, mod, w_out)


def _rope_tables(length):
    t = jnp.arange(length)
    row = (t // GRID_W).astype(F32)
    col = (t % GRID_W).astype(F32)
    half = HEAD_DIM // 2
    inv = ROPE_THETA ** (-jnp.arange(0, half, 2, dtype=F32) / half)
    ar, ac = row[:, None] * inv, col[:, None] * inv
    cos = jnp.concatenate([jnp.cos(ar), jnp.cos(ar), jnp.cos(ac), jnp.cos(ac)], axis=-1)
    sin = jnp.concatenate([-jnp.sin(ar), jnp.sin(ar), -jnp.sin(ac), jnp.sin(ac)], axis=-1)
    return jnp.tile(cos, (1, 2)), jnp.tile(sin, (1, 2))


def _channel_dft():
    k = np.arange(C_GROUP_DIM)
    ang = 2.0 * np.pi * ((k[:, None] * k[None, :]) % C_GROUP_DIM) / C_GROUP_DIM
    eye = np.eye(C_GROUPS)
    norm = 1.0 / np.sqrt(C_GROUP_DIM)
    cs = np.concatenate([np.kron(eye, np.cos(ang) * norm), np.kron(eye, -np.sin(ang) * norm)], axis=1)
    return _split3(jnp.asarray(cs, F32))


def _block_diag(w):
    g, c, _ = w.shape
    eye = jnp.eye(g, dtype=w.dtype)
    return (eye[:, None, :, None] * w[:, :, None, :]).reshape(g * c, g * c)


def kernel(x, c, ctx, c_ctx, w_ada, b_ada, g_ffn1, g_mix, g_ffn2, w_in, g_qn, g_kn, sink, w_four,
           w_out, w1_gate, w1_up, w1_down, w2_gate, w2_up, w2_down, g_final):
    batch, length, d = x.shape
    lc = ctx.shape[1]
    depth = w_ada.shape[0]
    assert length % (FFT_N1 * 8) == 0 and length >= BLOCK + 2 * WINDOW and batch + 1 <= 8
    tm = 512
    tm_ctx = batch * lc

    cvec = jnp.zeros((8, d), F32).at[:batch].set(c).at[batch].set(c_ctx)
    mod = _ada(cvec, w_ada, b_ada).reshape(depth, 8, N_MOD, d)

    cos, sin = _rope_tables(length)
    ones = jnp.ones((tm_ctx, LANE), F32)
    msm = _block_diag(jnp.full((2, HEAD_DIM, HEAD_DIM), 1.0 / HEAD_DIM, F32)).astype(BF16)
    cs_hi, cs_lo = _channel_dft()

    h = x
    hc = ctx.reshape(1, batch * lc, d)
    for l in range(depth):
        last = l == depth - 1
        mod_l, mod_c = mod[l, :batch], mod[l, batch:batch + 1]
        bf = lambda w: w[l].astype(BF16)
        w1 = (bf(w1_gate), bf(w1_up), bf(w1_down))
        w2 = (bf(w2_gate), bf(w2_up), bf(w2_down))
        win, wout = bf(w_in), bf(w_out)
        gq = jnp.tile(g_qn[l].reshape(1, HEAD_DIM), (1, 2))
        gk = jnp.tile(g_kn[l].reshape(1, HEAD_DIM), (1, 2))
        wbd = _block_diag(w_four[l]).astype(BF16)
        proj = functools.partial(_inproj, g=g_mix[l], w_in=win, gq=gq, gk=gk, msm=msm,
                                 cs_hi=cs_hi, cs_lo=cs_lo, wbd=wbd)

        h = _ffn(h, mod_l, g_ffn1[l], *w1, row0=0, tm=tm)
        hc = _ffn(hc, mod_c, g_ffn1[l], *w1, row0=0, tm=tm_ctx)

        qa, ka, vat, qb, kb, vbt, z = proj(h, mod_l, cos=cos, sin=sin, tm=tm, rope=True)
        qa_c, ka_c, vat_c, qb_c, kb_c, vbt_c, z_c = proj(hc, mod_c, cos=ones, sin=ones, tm=tm_ctx, rope=False)

        cat_k = lambda k, k_c: jnp.concatenate([k, k_c.reshape(batch, lc, LANE)], axis=1)
        cat_v = lambda v, v_c: jnp.concatenate(
            [v, v_c.reshape(LANE, batch, lc).transpose(1, 0, 2)], axis=2)
        tq = 256
        sink_l = jnp.repeat(sink[l].astype(F32) * LOG2E, tq).reshape(1, N_HEADS * tq)
        oa, ob = _attn_latent(qa, cat_k(ka, ka_c), cat_v(vat, vat_c), qb, cat_k(kb, kb_c), cat_v(vbt, vbt_c),
                              sink_l, tq=tq, tk=_key_tile(length + lc), lc=lc)
        oc = _fourier_latent(z)
        h = _outproj(h, oa, ob, oc, mod_l, wout, tm=tm)
        h = _ffn(h, mod_l, g_ffn2[l], *w2, row0=6, tm=tm, final_g=g_final if last else None)

        if not last:
            sink_c = jnp.repeat(sink[l].astype(F32) * LOG2E, lc).reshape(1, N_HEADS * lc)
            oa_c = _attn_ctx(qa_c, ka_c, vat_c, sink_c, batch=batch)
            ob_c = _attn_ctx(qb_c, kb_c, vbt_c, None, batch=batch)
            oc_c = _fourier_ctx(z_c, batch=batch)
            hc = _outproj(hc, oa_c, ob_c, oc_c, mod_c, wout, tm=tm_ctx)
            hc = _ffn(hc, mod_c, g_ffn2[l], *w2, row0=6, tm=tm_ctx)
    return h
```

```python
import functools

import jax
import jax.numpy as jnp
import numpy as np
from jax import lax
from jax.experimental import pallas as pl
from jax.experimental.pallas import tpu as pltpu

HEAD_DIM = 64
A_HEADS = 6
A_KV_HEADS = 2
B_HEADS = 6
B_KV_HEADS = 2
C_GROUPS = 4
C_GROUP_DIM = 64
GRID_W = 64
WINDOW = 128
BLOCK = 128
ROPE_THETA = 10000.0
NORM_EPS = 1e-6
N_MOD = 9
NEG_INF = -1e30

N_HEADS = 6
GROUP = 3
C_WIDTH = C_GROUPS * C_GROUP_DIM
FFT_N1 = 64
DEN_ROWS = 16
LOG2E = 1.4426950408889634

LANE = 128
VMEM_LIMIT = 56 * 1024 * 1024

F32 = jnp.float32
BF16 = jnp.bfloat16


def _params(*sem, flags=None):
    return pltpu.CompilerParams(dimension_semantics=tuple(sem), vmem_limit_bytes=VMEM_LIMIT, flags=flags)


def _resident(shape):
    nd = len(shape)
    return pl.BlockSpec(shape, lambda *_: (0,) * nd, pipeline_mode=pl.Buffered(1))


def _split3(x):
    hi = x.astype(BF16)
    lo = (x - hi.astype(F32)).astype(BF16)
    return hi, lo


def _dot(a, b):
    return jnp.dot(a, b, preferred_element_type=F32)


def _dot3(a_hi, a_lo, b_hi, b_lo):
    return _dot(a_hi, b_hi) + _dot(a_hi, b_lo) + _dot(a_lo, b_hi)


def _silu(x):
    return x / (1.0 + jnp.exp(-x))


def _norm_mod(x, g, shift, scale):
    ms = jnp.mean(x * x, axis=-1, keepdims=True)
    return (x * lax.rsqrt(ms + NORM_EPS)) * (g * (1.0 + scale)) + shift


def _ada_kernel(c_ref, w_ref, b_ref, o_ref):
    s_hi, s_lo = _split3(_silu(c_ref[...]))
    w_hi, w_lo = _split3(w_ref[0])
    o_ref[0] = _dot3(s_hi, s_lo, w_hi, w_lo) + b_ref[0]


def _ada(cvec, w_ada, b_ada):
    depth, d, nd = w_ada.shape
    tn = 1024
    return pl.pallas_call(
        _ada_kernel,
        grid=(depth, nd // tn),
        in_specs=[
            pl.BlockSpec((8, d), lambda l, j: (0, 0)),
            pl.BlockSpec((1, d, tn), lambda l, j: (l, 0, j)),
            pl.BlockSpec((1, 1, tn), lambda l, j: (l, 0, j)),
        ],
        out_specs=pl.BlockSpec((1, 8, tn), lambda l, j: (l, 0, j)),
        out_shape=jax.ShapeDtypeStruct((depth, 8, nd), F32),
        compiler_params=_params("parallel", "parallel"),
        name="ada",
    )(cvec, w_ada, b_ada.reshape(depth, 1, nd))


def _mix_out(oa_ref, ob_ref, oc_ref, w_ref):
    na = N_HEADS * HEAD_DIM
    return (_dot(oa_ref[0], w_ref[0:na, :]) + _dot(ob_ref[0], w_ref[na:2 * na, :])
            + _dot(oc_ref[0].astype(BF16), w_ref[2 * na:, :]))


def _ffn_kernel(x_ref, mod_ref, g_ref, wg_ref, wu_ref, wd_ref, *rest, row0, final, mixed):
    rest = list(rest)
    o_ref = rest.pop()
    x = x_ref[0]
    if mixed:
        oa_ref, ob_ref, oc_ref, wo_ref = rest[:4]
        x = x + mod_ref[0, 5:6, :] * _mix_out(oa_ref, ob_ref, oc_ref, wo_ref)
    shift = mod_ref[0, row0:row0 + 1, :]
    scale = mod_ref[0, row0 + 1:row0 + 2, :]
    gate = mod_ref[0, row0 + 2:row0 + 3, :]
    xm = _norm_mod(x, g_ref[...], shift, scale).astype(BF16)
    a = _dot(xm, wg_ref[...])
    u = _dot(xm, wu_ref[...])
    act = (_silu(a) * u).astype(BF16)
    y = x + (0.5 * gate) * _dot(act, wd_ref[...])
    if final:
        ms = jnp.mean(y * y, axis=-1, keepdims=True)
        y = (y * lax.rsqrt(ms + NORM_EPS)) * rest[-1][...]
    o_ref[0] = y


def _ffn(x, mod, g, wg, wu, wd, *, row0, tm, mix=None, final_g=None):
    groups, t, d = x.shape
    dff = wg.shape[1]
    final = final_g is not None
    tok = lambda b, i: (b, i, 0)
    in_specs = [
        pl.BlockSpec((1, tm, d), tok),
        pl.BlockSpec((1, N_MOD, d), lambda b, i: (b, 0, 0)),
        _resident((1, d)),
        _resident((d, dff)),
        _resident((d, dff)),
        _resident((dff, d)),
    ]
    args = [x, mod, g.reshape(1, d), wg, wu, wd]
    if mix is not None:
        oa, ob, oc, w_out = mix
        in_specs += [pl.BlockSpec((1, tm, oa.shape[2]), tok), pl.BlockSpec((1, tm, ob.shape[2]), tok),
                     pl.BlockSpec((1, tm, oc.shape[2]), tok), _resident(w_out.shape)]
        args += [oa, ob, oc, w_out]
    if final:
        in_specs.append(_resident((1, d)))
        args.append(final_g.reshape(1, d))
    return pl.pallas_call(
        functools.partial(_ffn_kernel, row0=row0, final=final, mixed=mix is not None),
        grid=(groups, t // tm),
        in_specs=in_specs,
        out_specs=pl.BlockSpec((1, tm, d), lambda b, i: (b, i, 0)),
        out_shape=jax.ShapeDtypeStruct((groups, t, d), F32),
        compiler_params=_params("parallel", "parallel"),
        name="ffn",
    )(*args)


def _rope_slab(t, cos, sin, lane):
    fwd = pltpu.roll(t, LANE - 16, axis=1)
    bwd = pltpu.roll(t, 16, axis=1)
    partner = jnp.where((lane % 32) < 16, fwd, bwd)
    return t * cos + partner * sin


def _inproj_kernel(x_ref, mod_ref, g_ref, w_ref, cos_ref, sin_ref, gq_ref, gk_ref, msm_ref,
                   cs_hi_ref, cs_lo_ref, wbd_ref,
                   qa_ref, ka_ref, vat_ref, qb_ref, kb_ref, vbt_ref, z_ref, *, rope):
    x = x_ref[0]
    tm = x.shape[0]
    shift = mod_ref[0, 3:4, :]
    scale = mod_ref[0, 4:5, :]
    xm = _norm_mod(x, g_ref[...], shift, scale).astype(BF16)
    u = _dot(xm, w_ref[...])

    lane = lax.broadcasted_iota(jnp.int32, (tm, LANE), 1)
    low = lane < HEAD_DIM
    if rope:
        cos = cos_ref[...]
        sin = sin_ref[...]

    def qk_norm(t, gain):
        sq_hi, sq_lo = _split3(t * t)
        ms = _dot(sq_hi, msm_ref[...]) + _dot(sq_lo, msm_ref[...])
        return t * lax.rsqrt(ms + NORM_EPS) * gain

    def finish(t):
        return _rope_slab(t, cos, sin, lane) if rope else t

    def emit_q(q_ref, slabs):
        for h in range(N_HEADS):
            t = slabs[h // 2] * (LOG2E * HEAD_DIM ** -0.5)
            if (h % 2) != (h // GROUP):
                t = pltpu.roll(t, HEAD_DIM, axis=1)
            keep = low if (h // GROUP) == 0 else jnp.logical_not(low)
            q_ref[0, h] = jnp.where(keep, t, 0.0).astype(BF16)

    qa = [finish(u[:, LANE * i:LANE * (i + 1)]) for i in range(3)]
    emit_q(qa_ref, qa)
    ka_ref[0] = finish(u[:, 384:512]).astype(BF16)
    vat_ref[0] = u[:, 512:640].T.astype(BF16)

    qb = [finish(qk_norm(u[:, 640 + LANE * i:640 + LANE * (i + 1)], gq_ref[...])) for i in range(3)]
    emit_q(qb_ref, qb)
    kb_ref[0] = finish(qk_norm(u[:, 1024:1152], gk_ref[...])).astype(BF16)
    vbt_ref[0] = u[:, 1152:1280].T.astype(BF16)

    uf_hi, uf_lo = _split3(u[:, 1280:1536])
    f = _dot3(uf_hi, uf_lo, cs_hi_ref[...], cs_lo_ref[...])
    wbd = wbd_ref[...]
    z_ref[0, :, 0:C_WIDTH] = _dot(f[:, 0:C_WIDTH].astype(BF16), wbd)
    z_ref[0, :, C_WIDTH:] = _dot(f[:, C_WIDTH:].astype(BF16), wbd)


def _inproj(x, mod, g, w_in, cos, sin, gq, gk, msm, cs_hi, cs_lo, wbd, *, tm, rope):
    groups, t, d = x.shape
    din = w_in.shape[1]
    tok = lambda b, i: (b, i, 0)
    in_specs = [
        pl.BlockSpec((1, tm, d), tok),
        pl.BlockSpec((1, N_MOD, d), lambda b, i: (b, 0, 0)),
        _resident((1, d)),
        _resident((d, din)),
        pl.BlockSpec((tm, LANE), lambda b, i: (i, 0)),
        pl.BlockSpec((tm, LANE), lambda b, i: (i, 0)),
        _resident((1, LANE)),
        _resident((1, LANE)),
        _resident((LANE, LANE)),
        _resident((C_WIDTH, 2 * C_WIDTH)),
        _resident((C_WIDTH, 2 * C_WIDTH)),
        _resident((C_WIDTH, C_WIDTH)),
    ]
    q_spec = pl.BlockSpec((1, N_HEADS, tm, LANE), lambda b, i: (b, 0, i, 0))
    k_spec = pl.BlockSpec((1, tm, LANE), tok)
    vt_spec = pl.BlockSpec((1, LANE, tm), lambda b, i: (b, 0, i))
    q_shape = jax.ShapeDtypeStruct((groups, N_HEADS, t, LANE), BF16)
    k_shape = jax.ShapeDtypeStruct((groups, t, LANE), BF16)
    vt_shape = jax.ShapeDtypeStruct((groups, LANE, t), BF16)
    return pl.pallas_call(
        functools.partial(_inproj_kernel, rope=rope),
        grid=(groups, t // tm),
        in_specs=in_specs,
        out_specs=[q_spec, k_spec, vt_spec, q_spec, k_spec, vt_spec,
                   pl.BlockSpec((1, tm, 2 * C_WIDTH), tok)],
        out_shape=[q_shape, k_shape, vt_shape, q_shape, k_shape, vt_shape,
                   jax.ShapeDtypeStruct((groups, t, 2 * C_WIDTH), F32)],
        compiler_params=_params("parallel", "parallel"),
        name="inproj",
    )(x, mod, g.reshape(1, d), w_in, cos, sin, gq, gk, msm, cs_hi, cs_lo, wbd)


def _nt(a, b):
    return lax.dot_general(a, b, (((1,), (1,)), ((), ())), preferred_element_type=F32)


def _store_heads(o_ref, ot):
    for pair in range(N_HEADS // 2):
        slab = jnp.concatenate([ot[2 * pair], ot[2 * pair + 1]], axis=0)
        o_ref[0, :, pair * LANE:(pair + 1) * LANE] = slab.T.astype(o_ref.dtype)


def _attn_latent_kernel(qa_ref, qb_ref, ka_ref, vat_ref, kb_ref, vbt_ref, kac_ref, vatc_ref, kbc_ref, vbtc_ref,
                        sink_ref, oa_ref, ob_ref, s_a, s_b, p_a, p_b, m_sc, bmax_sc, alpha_sc, acc_sc, *, tk):
    tq = qb_ref.shape[2]
    length = kb_ref.shape[1]
    lc = kbc_ref.shape[1]
    nk = (length + lc) // tk
    tail = length - (nk - 1) * tk
    span = tq + 2 * WINDOW
    na = span + lc
    gw = GROUP * tq
    qa = qa_ref[0].reshape(N_HEADS * tq, LANE)
    qb = qb_ref[0].reshape(N_HEADS * tq, LANE)
    tile = pl.program_id(1)
    start = pl.multiple_of(jnp.clip(tile * tq - WINDOW, 0, length - span), LANE)
    m_sc[...] = jnp.full_like(m_sc, NEG_INF)
    acc_sc[...] = jnp.zeros_like(acc_sc)

    def with_ones(vt):
        return jnp.concatenate([vt, jnp.ones((DEN_ROWS, vt.shape[1]), BF16)], axis=0)

    def rows(g):
        return slice(g * HEAD_DIM, (g + 1) * HEAD_DIM)

    def cols(g):
        return slice(g * gw, (g + 1) * gw)

    def scores_win(s_buf, slot):
        s_w = _nt(ka_ref[0, pl.ds(start, span), :], qa)
        kpos = start + lax.broadcasted_iota(jnp.int32, s_w.shape, 0)
        qpos = tile * tq + lax.broadcasted_iota(jnp.int32, s_w.shape, 1) % tq
        s_w = jnp.where(jnp.abs(kpos - qpos) <= WINDOW, s_w, NEG_INF)
        s_c = _nt(kac_ref[0], qa)
        s_buf[0:span] = s_w
        s_buf[span:na] = s_c
        bmax_sc[slot] = jnp.maximum(jnp.max(s_w, axis=0, keepdims=True), jnp.max(s_c, axis=0, keepdims=True))

    def softmax_win(s_buf, p_buf, slot):
        sink = sink_ref[...]
        m = jnp.maximum(sink, bmax_sc[slot])
        alpha_sc[slot] = jnp.exp2(sink - m)
        p_buf[0:na] = jnp.exp2(s_buf[0:na] - m).astype(BF16)

    def values_win(p_buf, slot):
        heads = []
        for g in range(2):
            vt = jnp.concatenate([vat_ref[0, rows(g), pl.ds(start, span)], vatc_ref[0, rows(g), :]], axis=1)
            acc = _dot(with_ones(vt), p_buf[0:na, cols(g)])
            o = acc[0:HEAD_DIM] / (acc[HEAD_DIM:HEAD_DIM + 1] + alpha_sc[slot][:, cols(g)])
            heads += [o[:, j * tq:(j + 1) * tq] for j in range(GROUP)]
        _store_heads(oa_ref, heads)

    def key_block(blk):
        if isinstance(blk, int) and blk == nk - 1:
            return jnp.concatenate([kb_ref[0, length - tail:, :], kbc_ref[0]], axis=0)
        return kb_ref[0, pl.ds(pl.multiple_of(blk * tk, tk), tk), :]

    def value_block(blk, g):
        if isinstance(blk, int) and blk == nk - 1:
            return jnp.concatenate([vbt_ref[0, rows(g), length - tail:], vbtc_ref[0, rows(g), :]], axis=1)
        return vbt_ref[0, rows(g), pl.ds(pl.multiple_of(blk * tk, tk), tk)]

    def scores(blk, s_buf, slot):
        s = _nt(key_block(blk), qb)
        s_buf[0:tk] = s
        bmax_sc[slot] = jnp.max(s, axis=0, keepdims=True)

    def softmax(s_buf, p_buf, slot):
        m_old = m_sc[...]
        m_new = jnp.maximum(m_old, bmax_sc[slot])
        m_sc[...] = m_new
        alpha_sc[slot] = jnp.exp2(m_old - m_new)
        p_buf[0:tk] = jnp.exp2(s_buf[0:tk] - m_new).astype(BF16)

    def values(blk, p_buf, slot):
        for g in range(2):
            vt = with_ones(value_block(blk, g))
            acc_sc[g] = acc_sc[g] * alpha_sc[slot][:, cols(g)] + _dot(vt, p_buf[0:tk, cols(g)])

    bufs = ((s_a, p_a), (s_b, p_b))
    n_steps = nk + 1

    def step(t, par, do_scores=True):
        (s_cur, p_cur), (s_oth, p_oth) = bufs[par], bufs[1 - par]
        if do_scores:
            scores(t, s_oth, 1 - par)
        softmax(s_cur, p_cur, par)
        values(t - 2, p_oth, 1 - par)

    scores_win(s_a, 0)
    scores(0, s_b, 1)
    softmax_win(s_a, p_a, 0)
    if nk >= 2:
        scores(1, s_a, 0)
    softmax(s_b, p_b, 1)
    values_win(p_a, 0)
    n_pairs = max(0, (nk - 3) // 2)

    def pair(i, carry):
        t = 2 + 2 * i
        step(t, 0)
        step(t + 1, 1)
        return carry

    lax.fori_loop(0, n_pairs, pair, 0)
    for t in range(2 + 2 * n_pairs, n_steps):
        step(t, t % 2, do_scores=t + 1 < n_steps)
    last = (n_steps - 1) % 2
    values(nk - 1, bufs[last][1], last)

    _store_heads(ob_ref, [acc_sc[h // GROUP, 0:HEAD_DIM, (h % GROUP) * tq:(h % GROUP + 1) * tq]
                          / acc_sc[h // GROUP, HEAD_DIM:HEAD_DIM + 1, (h % GROUP) * tq:(h % GROUP + 1) * tq]
                          for h in range(N_HEADS)])


def _key_tile(n_keys, cap=1024):
    return max(t for t in range(LANE, cap + 1, LANE) if n_keys % t == 0)


def _attn_latent(qa, ka, vat, qb, kb, vbt, ka_c, vat_c, kb_c, vbt_c, sink_row, *, tq, tk):
    b, _, l, _ = qb.shape
    lc = ka_c.shape[1] // b
    assert (l + lc) % tk == 0 and (l + lc) // tk * tk - tk <= l and tk > lc
    buf_rows = max(tk, tq + 2 * WINDOW + lc)
    q_spec = pl.BlockSpec((1, N_HEADS, tq, LANE), lambda b, i: (b, 0, i, 0))
    k_spec = pl.BlockSpec((1, l, LANE), lambda b, i: (b, 0, 0), pipeline_mode=pl.Buffered(1))
    vt_spec = pl.BlockSpec((1, LANE, l), lambda b, i: (b, 0, 0), pipeline_mode=pl.Buffered(1))
    kc_spec = pl.BlockSpec((1, lc, LANE), lambda b, i: (0, b, 0), pipeline_mode=pl.Buffered(1))
    vtc_spec = pl.BlockSpec((1, LANE, lc), lambda b, i: (0, 0, b), pipeline_mode=pl.Buffered(1))
    o_spec = pl.BlockSpec((1, tq, N_HEADS * HEAD_DIM), lambda b, i: (b, i, 0))
    o_shape = jax.ShapeDtypeStruct((b, l, N_HEADS * HEAD_DIM), BF16)
    return pl.pallas_call(
        functools.partial(_attn_latent_kernel, tk=tk),
        grid=(b, l // tq),
        in_specs=[q_spec, q_spec, k_spec, vt_spec, k_spec, vt_spec, kc_spec, vtc_spec, kc_spec, vtc_spec,
                  pl.BlockSpec((1, N_HEADS * tq), lambda b, i: (0, 0))],
        out_specs=[o_spec, o_spec],
        out_shape=[o_shape, o_shape],
        scratch_shapes=[
            pltpu.VMEM((buf_rows, N_HEADS * tq), F32),
            pltpu.VMEM((buf_rows, N_HEADS * tq), F32),
            pltpu.VMEM((buf_rows, N_HEADS * tq), BF16),
            pltpu.VMEM((buf_rows, N_HEADS * tq), BF16),
            pltpu.VMEM((1, N_HEADS * tq), F32),
            pltpu.VMEM((2, 1, N_HEADS * tq), F32),
            pltpu.VMEM((2, 1, N_HEADS * tq), F32),
            pltpu.VMEM((2, HEAD_DIM + DEN_ROWS, GROUP * tq), F32),
        ],
        compiler_params=_params("parallel", "parallel"),
        name="attn_latent",
    )(qa, qb, ka, vat, kb, vbt, ka_c, vat_c, kb_c, vbt_c, sink_row)


def _softmax_pv(q_ref, ks, vts, extra, mask=None):
    tq = q_ref.shape[2]
    aug = [[jnp.concatenate([vt[g * HEAD_DIM:(g + 1) * HEAD_DIM, :],
                             jnp.ones((DEN_ROWS, vt.shape[1]), BF16)], axis=0) for vt in vts]
           for g in range(2)]
    out = []
    for h in range(N_HEADS):
        ss = [_nt(k, q_ref[0, h]) for k in ks]
        if mask is not None:
            ss[0] = jnp.where(mask, ss[0], NEG_INF)
        m = functools.reduce(jnp.maximum, [jnp.max(s, axis=0, keepdims=True) for s in ss])
        if extra is not None:
            e = extra[:, h * tq:(h + 1) * tq]
            m = jnp.maximum(m, e)
        acc = functools.reduce(
            jnp.add, [_dot(vt, jnp.exp2(s - m).astype(BF16)) for s, vt in zip(ss, aug[h // GROUP])])
        den = acc[HEAD_DIM:HEAD_DIM + 1]
        if extra is not None:
            den = den + jnp.exp2(e - m)
        out.append(acc[0:HEAD_DIM] / den)
    return out


def _attn_ctx_kernel(q_ref, k_ref, vt_ref, *rest, has_sink):
    if has_sink:
        sink_ref, o_ref = rest
    else:
        (o_ref,) = rest
    extra = sink_ref[...] if has_sink else None
    _store_heads(o_ref, _softmax_pv(q_ref, [k_ref[0]], [vt_ref[0]], extra))


def _attn_ctx(q, k, vt, sink_row, *, batch):
    lc = q.shape[2] // batch
    in_specs = [
        pl.BlockSpec((1, N_HEADS, lc, LANE), lambda b: (0, 0, b, 0)),
        pl.BlockSpec((1, lc, LANE), lambda b: (0, b, 0)),
        pl.BlockSpec((1, LANE, lc), lambda b: (0, 0, b)),
    ]
    args = [q, k, vt]
    if sink_row is not None:
        in_specs.append(pl.BlockSpec((1, N_HEADS * lc), lambda b: (0, 0)))
        args.append(sink_row)
    return pl.pallas_call(
        functools.partial(_attn_ctx_kernel, has_sink=sink_row is not None),
        grid=(batch,),
        in_specs=in_specs,
        out_specs=pl.BlockSpec((1, lc, N_HEADS * HEAD_DIM), lambda b: (0, b, 0)),
        out_shape=jax.ShapeDtypeStruct((1, batch * lc, N_HEADS * HEAD_DIM), BF16),
        compiler_params=_params("parallel"),
        name="attn_ctx",
    )(*args)


def _fft1_kernel(x_ref, ch_ref, cl_ref, sh_ref, sl_ref, twc_ref, tws_ref, o_ref):
    tn2 = x_ref.shape[2]
    xt = jnp.transpose(x_ref[0], (1, 0, 2))
    x_hi, x_lo = _split3(jnp.concatenate([xt[i] for i in range(tn2)], axis=1))
    cx = _dot3(ch_ref[...], cl_ref[...], x_hi, x_lo)
    sx = _dot3(sh_ref[...], sl_ref[...], x_hi, x_lo)
    w = 2 * C_WIDTH
    outs = []
    for i in range(tn2):
        tc = jnp.concatenate([twc_ref[i]] * (C_WIDTH // LANE), axis=1)
        ts = jnp.concatenate([tws_ref[i]] * (C_WIDTH // LANE), axis=1)
        re, im = slice(i * w, i * w + C_WIDTH), slice(i * w + C_WIDTH, (i + 1) * w)
        ar = cx[:, re] + sx[:, im]
        ai = cx[:, im] - sx[:, re]
        outs.append(jnp.concatenate([ar * tc + ai * ts, ai * tc - ar * ts], axis=1))
    o_ref[0] = jnp.transpose(jnp.stack(outs, axis=0), (1, 0, 2))


def _fft2_kernel(x_ref, ch_ref, cl_ref, sh_ref, sl_ref, o_ref):
    outs = []
    for i in range(x_ref.shape[1]):
        r_hi, r_lo = _split3(x_ref[0, i, :, 0:C_WIDTH])
        i_hi, i_lo = _split3(x_ref[0, i, :, C_WIDTH:])
        outs.append(_dot3(ch_ref[...], cl_ref[...], r_hi, r_lo) + _dot3(sh_ref[...], sl_ref[...], i_hi, i_lo))
    o_ref[0] = jnp.transpose(jnp.stack(outs, axis=0), (1, 0, 2))


def _dft_tables(n, norm):
    k = np.arange(n, dtype=np.int64)
    ang = 2.0 * np.pi * ((k[:, None] * k[None, :]) % n).astype(np.float64) / n
    c = jnp.asarray(np.cos(ang) * norm, F32)
    s = jnp.asarray(np.sin(ang) * norm, F32)
    return _split3(c) + _split3(s)


def _fourier_latent(z):
    b, l, w = z.shape
    n1, n2 = FFT_N1, l // FFT_N1
    tn2, tk1 = 8, 8
    prod = jnp.arange(n2)[:, None] * jnp.arange(n1)[None, :]
    ang = prod.astype(F32) * (2.0 * np.pi / l)
    twc = jnp.broadcast_to(jnp.cos(ang)[:, :, None], (n2, n1, LANE))
    tws = jnp.broadcast_to(jnp.sin(ang)[:, :, None], (n2, n1, LANE))
    t1 = _dft_tables(n1, 1.0)
    y = pl.pallas_call(
        _fft1_kernel,
        grid=(b, n2 // tn2),
        in_specs=[pl.BlockSpec((1, n1, tn2, w), lambda b, j: (b, 0, j, 0))]
        + [_resident((n1, n1))] * 4
        + [pl.BlockSpec((tn2, n1, LANE), lambda b, j: (j, 0, 0))] * 2,
        out_specs=pl.BlockSpec((1, n1, tn2, w), lambda b, j: (b, 0, j, 0)),
        out_shape=jax.ShapeDtypeStruct((b, n1, n2, w), F32),
        compiler_params=_params("parallel", "parallel"),
        name="fft_stage1",
    )(z.reshape(b, n1, n2, w), *t1, twc, tws)
    t2m = _dft_tables(n2, 1.0 / np.sqrt(l))
    out = pl.pallas_call(
        _fft2_kernel,
        grid=(b, n1 // tk1),
        in_specs=[pl.BlockSpec((1, tk1, n2, w), lambda b, j: (b, j, 0, 0))] + [_resident((n2, n2))] * 4,
        out_specs=pl.BlockSpec((1, n2, tk1, C_WIDTH), lambda b, j: (b, 0, j, 0)),
        out_shape=jax.ShapeDtypeStruct((b, n2, n1, C_WIDTH), F32),
        compiler_params=_params("parallel", "parallel"),
        name="fft_stage2",
    )(y, *t2m)
    return out.reshape(b, l, C_WIDTH)


def _fourier_ctx_kernel(x_ref, ch_ref, cl_ref, sh_ref, sl_ref, o_ref):
    r_hi, r_lo = _split3(x_ref[0, :, 0:C_WIDTH])
    i_hi, i_lo = _split3(x_ref[0, :, C_WIDTH:])
    o_ref[0] = _dot3(ch_ref[...], cl_ref[...], r_hi, r_lo) + _dot3(sh_ref[...], sl_ref[...], i_hi, i_lo)


def _fourier_ctx(z, *, batch):
    lc = z.shape[1] // batch
    tabs = _dft_tables(lc, 1.0 / np.sqrt(lc))
    return pl.pallas_call(
        _fourier_ctx_kernel,
        grid=(batch,),
        in_specs=[pl.BlockSpec((1, lc, 2 * C_WIDTH), lambda b: (0, b, 0))] + [_resident((lc, lc))] * 4,
        out_specs=pl.BlockSpec((1, lc, C_WIDTH), lambda b: (0, b, 0)),
        out_shape=jax.ShapeDtypeStruct((1, batch * lc, C_WIDTH), F32),
        compiler_params=_params("parallel"),
        name="fourier_ctx",
    )(z, *tabs)


def _rope_tables(length):
    t = jnp.arange(length)
    row = (t // GRID_W).astype(F32)
    col = (t % GRID_W).astype(F32)
    half = HEAD_DIM // 2
    inv = ROPE_THETA ** (-jnp.arange(0, half, 2, dtype=F32) / half)
    ar, ac = row[:, None] * inv, col[:, None] * inv
    cos = jnp.concatenate([jnp.cos(ar), jnp.cos(ar), jnp.cos(ac), jnp.cos(ac)], axis=-1)
    sin = jnp.concatenate([-jnp.sin(ar), jnp.sin(ar), -jnp.sin(ac), jnp.sin(ac)], axis=-1)
    return jnp.tile(cos, (1, 2)), jnp.tile(sin, (1, 2))


def _channel_dft():
    k = np.arange(C_GROUP_DIM)
    ang = 2.0 * np.pi * ((k[:, None] * k[None, :]) % C_GROUP_DIM) / C_GROUP_DIM
    eye = np.eye(C_GROUPS)
    norm = 1.0 / np.sqrt(C_GROUP_DIM)
    cs = np.concatenate([np.kron(eye, np.cos(ang) * norm), np.kron(eye, -np.sin(ang) * norm)], axis=1)
    return _split3(jnp.asarray(cs, F32))


def _block_diag(w):
    g, c, _ = w.shape
    eye = jnp.eye(g, dtype=w.dtype)
    return (eye[:, None, :, None] * w[:, :, None, :]).reshape(g * c, g * c)


def kernel(x, c, ctx, c_ctx, w_ada, b_ada, g_ffn1, g_mix, g_ffn2, w_in, g_qn, g_kn, sink, w_four,
           w_out, w1_gate, w1_up, w1_down, w2_gate, w2_up, w2_down, g_final):
    batch, length, d = x.shape
    lc = ctx.shape[1]
    depth = w_ada.shape[0]
    assert length % (FFT_N1 * 8) == 0 and length >= BLOCK + 2 * WINDOW and batch + 1 <= 8
    tm = 512
    tm_ctx = batch * lc

    cvec = jnp.zeros((8, d), F32).at[:batch].set(c).at[batch].set(c_ctx)
    mod = _ada(cvec, w_ada, b_ada).reshape(depth, 8, N_MOD, d)

    cos, sin = _rope_tables(length)
    ones = jnp.ones((tm_ctx, LANE), F32)
    msm = _block_diag(jnp.full((2, HEAD_DIM, HEAD_DIM), 1.0 / HEAD_DIM, F32)).astype(BF16)
    cs_hi, cs_lo = _channel_dft()

    h = x
    hc = ctx.reshape(1, batch * lc, d)
    for l in range(depth):
        last = l == depth - 1
        mod_l, mod_c = mod[l, :batch], mod[l, batch:batch + 1]
        bf = lambda w: w[l].astype(BF16)
        w1 = (bf(w1_gate), bf(w1_up), bf(w1_down))
        w2 = (bf(w2_gate), bf(w2_up), bf(w2_down))
        win, wout = bf(w_in), bf(w_out)
        gq = jnp.tile(g_qn[l].reshape(1, HEAD_DIM), (1, 2))
        gk = jnp.tile(g_kn[l].reshape(1, HEAD_DIM), (1, 2))
        wbd = _block_diag(w_four[l]).astype(BF16)
        proj = functools.partial(_inproj, g=g_mix[l], w_in=win, gq=gq, gk=gk, msm=msm,
                                 cs_hi=cs_hi, cs_lo=cs_lo, wbd=wbd)

        h = _ffn(h, mod_l, g_ffn1[l], *w1, row0=0, tm=tm)
        hc = _ffn(hc, mod_c, g_ffn1[l], *w1, row0=0, tm=tm_ctx)

        qa, ka, vat, qb, kb, vbt, z = proj(h, mod_l, cos=cos, sin=sin, tm=tm, rope=True)
        qa_c, ka_c, vat_c, qb_c, kb_c, vbt_c, z_c = proj(hc, mod_c, cos=ones, sin=ones, tm=tm_ctx, rope=False)

        tq = 512
        sink_l = jnp.repeat(sink[l].astype(F32) * LOG2E, tq).reshape(1, N_HEADS * tq)
        oa, ob = _attn_latent(qa, ka, vat, qb, kb, vbt, ka_c, vat_c, kb_c, vbt_c, sink_l,
                              tq=tq, tk=_key_tile(length + lc))
        oc = _fourier_latent(z)
        h = _ffn(h, mod_l, g_ffn2[l], *w2, row0=6, tm=tm, mix=(oa, ob, oc, wout),
                 final_g=g_final if last else None)

        if not last:
            sink_c = jnp.repeat(sink[l].astype(F32) * LOG2E, lc).reshape(1, N_HEADS * lc)
            oa_c = _attn_ctx(qa_c, ka_c, vat_c, sink_c, batch=batch)
            ob_c = _attn_ctx(qb_c, kb_c, vbt_c, None, batch=batch)
            oc_c = _fourier_ctx(z_c, batch=batch)
            hc = _ffn(hc, mod_c, g_ffn2[l], *w2, row0=6, tm=tm_ctx, mix=(oa_c, ob_c, oc_c, wout))
    return h
```

```python
import functools

import jax
import jax.numpy as jnp
import numpy as np
from jax import lax
from jax.experimental import pallas as pl
from jax.experimental.pallas import tpu as pltpu

HEAD_DIM = 64
A_HEADS = 6
A_KV_HEADS = 2
B_HEADS = 6
B_KV_HEADS = 2
C_GROUPS = 4
C_GROUP_DIM = 64
GRID_W = 64
WINDOW = 128
BLOCK = 128
ROPE_THETA = 10000.0
NORM_EPS = 1e-6
N_MOD = 9
NEG_INF = -1e30

N_HEADS = 6
GROUP = 3
C_WIDTH = C_GROUPS * C_GROUP_DIM
FFT_N1 = 64
DEN_ROWS = 16
LOG2E = 1.4426950408889634

LANE = 128
VMEM_LIMIT = 56 * 1024 * 1024

F32 = jnp.float32
BF16 = jnp.bfloat16


def _params(*sem, flags=None):
    return pltpu.CompilerParams(dimension_semantics=tuple(sem), vmem_limit_bytes=VMEM_LIMIT, flags=flags)


def _resident(shape):
    nd = len(shape)
    return pl.BlockSpec(shape, lambda *_: (0,) * nd, pipeline_mode=pl.Buffered(1))


def _split3(x):
    hi = x.astype(BF16)
    lo = (x - hi.astype(F32)).astype(BF16)
    return hi, lo


def _dot(a, b):
    return jnp.dot(a, b, preferred_element_type=F32)


def _dot3(a_hi, a_lo, b_hi, b_lo):
    return _dot(a_hi, b_hi) + _dot(a_hi, b_lo) + _dot(a_lo, b_hi)


def _silu(x):
    return x / (1.0 + jnp.exp(-x))


def _norm_mod(x, g, shift, scale):
    ms = jnp.mean(x * x, axis=-1, keepdims=True)
    return (x * lax.rsqrt(ms + NORM_EPS)) * (g * (1.0 + scale)) + shift


def _ada_kernel(c_ref, w_ref, b_ref, o_ref):
    s_hi, s_lo = _split3(_silu(c_ref[...]))
    w_hi, w_lo = _split3(w_ref[0])
    o_ref[0] = _dot3(s_hi, s_lo, w_hi, w_lo) + b_ref[0]


def _ada(cvec, w_ada, b_ada):
    depth, d, nd = w_ada.shape
    tn = 1024
    return pl.pallas_call(
        _ada_kernel,
        grid=(depth, nd // tn),
        in_specs=[
            pl.BlockSpec((8, d), lambda l, j: (0, 0)),
            pl.BlockSpec((1, d, tn), lambda l, j: (l, 0, j)),
            pl.BlockSpec((1, 1, tn), lambda l, j: (l, 0, j)),
        ],
        out_specs=pl.BlockSpec((1, 8, tn), lambda l, j: (l, 0, j)),
        out_shape=jax.ShapeDtypeStruct((depth, 8, nd), F32),
        compiler_params=_params("parallel", "parallel"),
        name="ada",
    )(cvec, w_ada, b_ada.reshape(depth, 1, nd))


def _mix_out(oa_ref, ob_ref, oc_ref, w_ref):
    na = N_HEADS * HEAD_DIM
    return (_dot(oa_ref[0], w_ref[0:na, :]) + _dot(ob_ref[0], w_ref[na:2 * na, :])
            + _dot(oc_ref[0].astype(BF16), w_ref[2 * na:, :]))


def _ffn_kernel(x_ref, mod_ref, g_ref, wg_ref, wu_ref, wd_ref, *rest, row0, final, mixed):
    rest = list(rest)
    o_ref = rest.pop()
    x = x_ref[0]
    if mixed:
        oa_ref, ob_ref, oc_ref, wo_ref = rest[:4]
        x = x + mod_ref[0, 5:6, :] * _mix_out(oa_ref, ob_ref, oc_ref, wo_ref)
    shift = mod_ref[0, row0:row0 + 1, :]
    scale = mod_ref[0, row0 + 1:row0 + 2, :]
    gate = mod_ref[0, row0 + 2:row0 + 3, :]
    xm = _norm_mod(x, g_ref[...], shift, scale).astype(BF16)
    a = _dot(xm, wg_ref[...])
    u = _dot(xm, wu_ref[...])
    act = (_silu(a) * u).astype(BF16)
    y = x + (0.5 * gate) * _dot(act, wd_ref[...])
    if final:
        ms = jnp.mean(y * y, axis=-1, keepdims=True)
        y = (y * lax.rsqrt(ms + NORM_EPS)) * rest[-1][...]
    o_ref[0] = y


def _ffn(x, mod, g, wg, wu, wd, *, row0, tm, mix=None, final_g=None):
    groups, t, d = x.shape
    dff = wg.shape[1]
    final = final_g is not None
    tok = lambda b, i: (b, i, 0)
    in_specs = [
        pl.BlockSpec((1, tm, d), tok),
        pl.BlockSpec((1, N_MOD, d), lambda b, i: (b, 0, 0)),
        _resident((1, d)),
        _resident((d, dff)),
        _resident((d, dff)),
        _resident((dff, d)),
    ]
    args = [x, mod, g.reshape(1, d), wg, wu, wd]
    if mix is not None:
        oa, ob, oc, w_out = mix
        in_specs += [pl.BlockSpec((1, tm, oa.shape[2]), tok), pl.BlockSpec((1, tm, ob.shape[2]), tok),
                     pl.BlockSpec((1, tm, oc.shape[2]), tok), _resident(w_out.shape)]
        args += [oa, ob, oc, w_out]
    if final:
        in_specs.append(_resident((1, d)))
        args.append(final_g.reshape(1, d))
    return pl.pallas_call(
        functools.partial(_ffn_kernel, row0=row0, final=final, mixed=mix is not None),
        grid=(groups, t // tm),
        in_specs=in_specs,
        out_specs=pl.BlockSpec((1, tm, d), lambda b, i: (b, i, 0)),
        out_shape=jax.ShapeDtypeStruct((groups, t, d), F32),
        compiler_params=_params("parallel", "parallel"),
        name="ffn",
    )(*args)


def _rope_slab(t, cos, sin, lane):
    fwd = pltpu.roll(t, LANE - 16, axis=1)
    bwd = pltpu.roll(t, 16, axis=1)
    partner = jnp.where((lane % 32) < 16, fwd, bwd)
    return t * cos + partner * sin


def _wfold_kernel(cs_hi_ref, cs_lo_ref, w_ref, o_ref):
    w_hi, w_lo = _split3(w_ref[0])
    for half in range(2):
        sl = slice(half * C_WIDTH, (half + 1) * C_WIDTH)
        o_ref[0, :, sl] = _dot3(cs_hi_ref[:, sl], cs_lo_ref[:, sl], w_hi, w_lo).astype(BF16)


def _wfold(cs_hi, cs_lo, wbd):
    depth = wbd.shape[0]
    return pl.pallas_call(
        _wfold_kernel,
        grid=(depth,),
        in_specs=[_resident(cs_hi.shape), _resident(cs_lo.shape),
                  pl.BlockSpec((1, C_WIDTH, C_WIDTH), lambda l: (l, 0, 0))],
        out_specs=pl.BlockSpec((1, C_WIDTH, 2 * C_WIDTH), lambda l: (l, 0, 0)),
        out_shape=jax.ShapeDtypeStruct((depth, C_WIDTH, 2 * C_WIDTH), BF16),
        compiler_params=_params("parallel"),
        name="wfold",
    )(cs_hi, cs_lo, wbd)


def _inproj_kernel(x_ref, mod_ref, g_ref, w_ref, cos_ref, sin_ref, gq_ref, gk_ref, msm_ref, wc_ref,
                   qa_ref, ka_ref, vat_ref, qb_ref, kb_ref, vbt_ref, z_ref, *, rope):
    x = x_ref[0]
    tm = x.shape[0]
    shift = mod_ref[0, 3:4, :]
    scale = mod_ref[0, 4:5, :]
    xm = _norm_mod(x, g_ref[...], shift, scale).astype(BF16)

    u_all = _dot(xm, w_ref[...])

    def u(lo, hi):
        return u_all[:, lo:hi]

    lane = lax.broadcasted_iota(jnp.int32, (tm, LANE), 1)
    low = lane < HEAD_DIM
    if rope:
        cos = cos_ref[...]
        sin = sin_ref[...]

    def qk_norm(t, gain):
        ms = _dot((t * t).astype(BF16), msm_ref[...])
        return t * lax.rsqrt(ms + NORM_EPS) * gain

    def finish(t):
        return _rope_slab(t, cos, sin, lane) if rope else t

    def emit_q(q_ref, slabs):
        for h in range(N_HEADS):
            t = slabs[h // 2] * (LOG2E * HEAD_DIM ** -0.5)
            if (h % 2) != (h // GROUP):
                t = pltpu.roll(t, HEAD_DIM, axis=1)
            keep = low if (h // GROUP) == 0 else jnp.logical_not(low)
            q_ref[0, h] = jnp.where(keep, t, 0.0).astype(BF16)

    qa = [finish(u(LANE * i, LANE * (i + 1))) for i in range(3)]
    emit_q(qa_ref, qa)
    ka_ref[0] = finish(u(384, 512)).astype(BF16)
    vat_ref[0] = u(512, 640).T.astype(BF16)

    qb = [finish(qk_norm(u(640 + LANE * i, 640 + LANE * (i + 1)), gq_ref[...])) for i in range(3)]
    emit_q(qb_ref, qb)
    kb_ref[0] = finish(qk_norm(u(1024, 1152), gk_ref[...])).astype(BF16)
    vbt_ref[0] = u(1152, 1280).T.astype(BF16)

    z_ref[0] = _dot(u(1280, 1536).astype(BF16), wc_ref[...])


def _inproj(x, mod, g, w_in, cos, sin, gq, gk, msm, wc, *, tm, rope):
    groups, t, d = x.shape
    din = w_in.shape[1]
    tok = lambda b, i: (b, i, 0)
    in_specs = [
        pl.BlockSpec((1, tm, d), tok),
        pl.BlockSpec((1, N_MOD, d), lambda b, i: (b, 0, 0)),
        _resident((1, d)),
        _resident((d, din)),
        pl.BlockSpec((tm, LANE), lambda b, i: (i, 0)),
        pl.BlockSpec((tm, LANE), lambda b, i: (i, 0)),
        _resident((1, LANE)),
        _resident((1, LANE)),
        _resident((LANE, LANE)),
        _resident((C_WIDTH, 2 * C_WIDTH)),
    ]
    q_spec = pl.BlockSpec((1, N_HEADS, tm, LANE), lambda b, i: (b, 0, i, 0))
    k_spec = pl.BlockSpec((1, tm, LANE), tok)
    vt_spec = pl.BlockSpec((1, LANE, tm), lambda b, i: (b, 0, i))
    q_shape = jax.ShapeDtypeStruct((groups, N_HEADS, t, LANE), BF16)
    k_shape = jax.ShapeDtypeStruct((groups, t, LANE), BF16)
    vt_shape = jax.ShapeDtypeStruct((groups, LANE, t), BF16)
    return pl.pallas_call(
        functools.partial(_inproj_kernel, rope=rope),
        grid=(groups, t // tm),
        in_specs=in_specs,
        out_specs=[q_spec, k_spec, vt_spec, q_spec, k_spec, vt_spec,
                   pl.BlockSpec((1, tm, 2 * C_WIDTH), tok)],
        out_shape=[q_shape, k_shape, vt_shape, q_shape, k_shape, vt_shape,
                   jax.ShapeDtypeStruct((groups, t, 2 * C_WIDTH), F32)],
        compiler_params=_params("parallel", "parallel"),
        name="inproj",
    )(x, mod, g.reshape(1, d), w_in, cos, sin, gq, gk, msm, wc)


def _nt(a, b):
    return lax.dot_general(a, b, (((1,), (1,)), ((), ())), preferred_element_type=F32)


def _store_heads(o_ref, ot):
    for pair in range(N_HEADS // 2):
        slab = jnp.concatenate([ot[2 * pair], ot[2 * pair + 1]], axis=0)
        o_ref[0, :, pair * LANE:(pair + 1) * LANE] = slab.T.astype(o_ref.dtype)


def _attn_latent_kernel(qa_ref, qb_ref, ka_ref, vat_ref, kb_ref, vbt_ref, kac_ref, vatc_ref, kbc_ref, vbtc_ref,
                        sink_ref, oa_ref, ob_ref, s_a, s_b, p_a, p_b, m_sc, bmax_sc, alpha_sc, acc_sc, *, tk):
    tq = qb_ref.shape[2]
    length = kb_ref.shape[1]
    lc = kbc_ref.shape[1]
    nk = (length + lc) // tk
    tail = length - (nk - 1) * tk
    span = tq + 2 * WINDOW
    na = span + lc
    gw = GROUP * tq
    qa = qa_ref[0].reshape(N_HEADS * tq, LANE)
    qb = qb_ref[0].reshape(N_HEADS * tq, LANE)
    tile = pl.program_id(1)
    start = pl.multiple_of(jnp.clip(tile * tq - WINDOW, 0, length - span), LANE)
    m_sc[...] = jnp.full_like(m_sc, NEG_INF)
    acc_sc[...] = jnp.zeros_like(acc_sc)

    def with_ones(vt):
        return jnp.concatenate([vt, jnp.ones((DEN_ROWS, vt.shape[1]), BF16)], axis=0)

    def rows(g):
        return slice(g * HEAD_DIM, (g + 1) * HEAD_DIM)

    def cols(g):
        return slice(g * gw, (g + 1) * gw)

    def scores_win(s_buf, slot):
        s_w = _nt(ka_ref[0, pl.ds(start, span), :], qa)
        kpos = start + lax.broadcasted_iota(jnp.int32, s_w.shape, 0)
        qpos = tile * tq + lax.broadcasted_iota(jnp.int32, s_w.shape, 1) % tq
        s_w = jnp.where(jnp.abs(kpos - qpos) <= WINDOW, s_w, NEG_INF)
        s_c = _nt(kac_ref[0], qa)
        s_buf[0:span] = s_w
        s_buf[span:na] = s_c
        bmax_sc[slot] = jnp.maximum(jnp.max(s_w, axis=0, keepdims=True), jnp.max(s_c, axis=0, keepdims=True))

    def softmax_win(s_buf, p_buf, slot):
        sink = sink_ref[...]
        m = jnp.maximum(sink, bmax_sc[slot])
        alpha_sc[slot] = jnp.exp2(sink - m)
        p_buf[0:na] = jnp.exp2(s_buf[0:na] - m).astype(BF16)

    def values_win(p_buf, slot):
        heads = []
        for g in range(2):
            vt = jnp.concatenate([vat_ref[0, rows(g), pl.ds(start, span)], vatc_ref[0, rows(g), :]], axis=1)
            acc = _dot(with_ones(vt), p_buf[0:na, cols(g)])
            o = acc[0:HEAD_DIM] / (acc[HEAD_DIM:HEAD_DIM + 1] + alpha_sc[slot][:, cols(g)])
            heads += [o[:, j * tq:(j + 1) * tq] for j in range(GROUP)]
        _store_heads(oa_ref, heads)

    def key_block(blk):
        if isinstance(blk, int) and blk == nk - 1:
            return jnp.concatenate([kb_ref[0, length - tail:, :], kbc_ref[0]], axis=0)
        return kb_ref[0, pl.ds(pl.multiple_of(blk * tk, tk), tk), :]

    def value_block(blk, g):
        if isinstance(blk, int) and blk == nk - 1:
            return jnp.concatenate([vbt_ref[0, rows(g), length - tail:], vbtc_ref[0, rows(g), :]], axis=1)
        return vbt_ref[0, rows(g), pl.ds(pl.multiple_of(blk * tk, tk), tk)]

    def scores(blk, s_buf, slot):
        s = _nt(key_block(blk), qb)
        s_buf[0:tk] = s
        bmax_sc[slot] = jnp.max(s, axis=0, keepdims=True)

    def softmax(s_buf, p_buf, slot):
        m_old = m_sc[...]
        m_new = jnp.maximum(m_old, bmax_sc[slot])
        m_sc[...] = m_new
        alpha_sc[slot] = jnp.exp2(m_old - m_new)
        p_buf[0:tk] = jnp.exp2(s_buf[0:tk] - m_new).astype(BF16)

    def values(blk, p_buf, slot):
        for g in range(2):
            vt = with_ones(value_block(blk, g))
            acc_sc[g] = acc_sc[g] * alpha_sc[slot][:, cols(g)] + _dot(vt, p_buf[0:tk, cols(g)])

    bufs = ((s_a, p_a), (s_b, p_b))
    n_steps = nk + 1

    def step(t, par, do_scores=True):
        (s_cur, p_cur), (s_oth, p_oth) = bufs[par], bufs[1 - par]
        if do_scores:
            scores(t, s_oth, 1 - par)
        softmax(s_cur, p_cur, par)
        values(t - 2, p_oth, 1 - par)

    scores_win(s_a, 0)
    scores(0, s_b, 1)
    softmax_win(s_a, p_a, 0)
    if nk >= 2:
        scores(1, s_a, 0)
    softmax(s_b, p_b, 1)
    values_win(p_a, 0)
    n_pairs = max(0, (nk - 3) // 2)

    def pair(i, carry):
        t = 2 + 2 * i
        step(t, 0)
        step(t + 1, 1)
        return carry

    lax.fori_loop(0, n_pairs, pair, 0)
    for t in range(2 + 2 * n_pairs, n_steps):
        step(t, t % 2, do_scores=t + 1 < n_steps)
    last = (n_steps - 1) % 2
    values(nk - 1, bufs[last][1], last)

    _store_heads(ob_ref, [acc_sc[h // GROUP, 0:HEAD_DIM, (h % GROUP) * tq:(h % GROUP + 1) * tq]
                          / acc_sc[h // GROUP, HEAD_DIM:HEAD_DIM + 1, (h % GROUP) * tq:(h % GROUP + 1) * tq]
                          for h in range(N_HEADS)])


def _key_tile(n_keys, cap=1024):
    return max(t for t in range(LANE, cap + 1, LANE) if n_keys % t == 0)


def _attn_latent(qa, ka, vat, qb, kb, vbt, ka_c, vat_c, kb_c, vbt_c, sink_row, *, tq, tk):
    b, _, l, _ = qb.shape
    lc = ka_c.shape[1] // b
    assert (l + lc) % tk == 0 and (l + lc) // tk * tk - tk <= l and tk > lc
    buf_rows = max(tk, tq + 2 * WINDOW + lc)
    q_spec = pl.BlockSpec((1, N_HEADS, tq, LANE), lambda b, i: (b, 0, i, 0))
    k_spec = pl.BlockSpec((1, l, LANE), lambda b, i: (b, 0, 0), pipeline_mode=pl.Buffered(1))
    vt_spec = pl.BlockSpec((1, LANE, l), lambda b, i: (b, 0, 0), pipeline_mode=pl.Buffered(1))
    kc_spec = pl.BlockSpec((1, lc, LANE), lambda b, i: (0, b, 0), pipeline_mode=pl.Buffered(1))
    vtc_spec = pl.BlockSpec((1, LANE, lc), lambda b, i: (0, 0, b), pipeline_mode=pl.Buffered(1))
    o_spec = pl.BlockSpec((1, tq, N_HEADS * HEAD_DIM), lambda b, i: (b, i, 0))
    o_shape = jax.ShapeDtypeStruct((b, l, N_HEADS * HEAD_DIM), BF16)
    return pl.pallas_call(
        functools.partial(_attn_latent_kernel, tk=tk),
        grid=(b, l // tq),
        in_specs=[q_spec, q_spec, k_spec, vt_spec, k_spec, vt_spec, kc_spec, vtc_spec, kc_spec, vtc_spec,
                  pl.BlockSpec((1, N_HEADS * tq), lambda b, i: (0, 0))],
        out_specs=[o_spec, o_spec],
        out_shape=[o_shape, o_shape],
        scratch_shapes=[
            pltpu.VMEM((buf_rows, N_HEADS * tq), F32),
            pltpu.VMEM((buf_rows, N_HEADS * tq), F32),
            pltpu.VMEM((buf_rows, N_HEADS * tq), BF16),
            pltpu.VMEM((buf_rows, N_HEADS * tq), BF16),
            pltpu.VMEM((1, N_HEADS * tq), F32),
            pltpu.VMEM((2, 1, N_HEADS * tq), F32),
            pltpu.VMEM((2, 1, N_HEADS * tq), F32),
            pltpu.VMEM((2, HEAD_DIM + DEN_ROWS, GROUP * tq), F32),
        ],
        compiler_params=_params("parallel", "parallel"),
        name="attn_latent",
    )(qa, qb, ka, vat, kb, vbt, ka_c, vat_c, kb_c, vbt_c, sink_row)


def _softmax_pv(q_ref, ks, vts, extra, mask=None):
    tq = q_ref.shape[2]
    aug = [[jnp.concatenate([vt[g * HEAD_DIM:(g + 1) * HEAD_DIM, :],
                             jnp.ones((DEN_ROWS, vt.shape[1]), BF16)], axis=0) for vt in vts]
           for g in range(2)]
    out = []
    for h in range(N_HEADS):
        ss = [_nt(k, q_ref[0, h]) for k in ks]
        if mask is not None:
            ss[0] = jnp.where(mask, ss[0], NEG_INF)
        m = functools.reduce(jnp.maximum, [jnp.max(s, axis=0, keepdims=True) for s in ss])
        if extra is not None:
            e = extra[:, h * tq:(h + 1) * tq]
            m = jnp.maximum(m, e)
        acc = functools.reduce(
            jnp.add, [_dot(vt, jnp.exp2(s - m).astype(BF16)) for s, vt in zip(ss, aug[h // GROUP])])
        den = acc[HEAD_DIM:HEAD_DIM + 1]
        if extra is not None:
            den = den + jnp.exp2(e - m)
        out.append(acc[0:HEAD_DIM] / den)
    return out


def _attn_ctx_kernel(q_ref, k_ref, vt_ref, *rest, has_sink):
    if has_sink:
        sink_ref, o_ref = rest
    else:
        (o_ref,) = rest
    extra = sink_ref[...] if has_sink else None
    _store_heads(o_ref, _softmax_pv(q_ref, [k_ref[0]], [vt_ref[0]], extra))


def _attn_ctx(q, k, vt, sink_row, *, batch):
    lc = q.shape[2] // batch
    in_specs = [
        pl.BlockSpec((1, N_HEADS, lc, LANE), lambda b: (0, 0, b, 0)),
        pl.BlockSpec((1, lc, LANE), lambda b: (0, b, 0)),
        pl.BlockSpec((1, LANE, lc), lambda b: (0, 0, b)),
    ]
    args = [q, k, vt]
    if sink_row is not None:
        in_specs.append(pl.BlockSpec((1, N_HEADS * lc), lambda b: (0, 0)))
        args.append(sink_row)
    return pl.pallas_call(
        functools.partial(_attn_ctx_kernel, has_sink=sink_row is not None),
        grid=(batch,),
        in_specs=in_specs,
        out_specs=pl.BlockSpec((1, lc, N_HEADS * HEAD_DIM), lambda b: (0, b, 0)),
        out_shape=jax.ShapeDtypeStruct((1, batch * lc, N_HEADS * HEAD_DIM), BF16),
        compiler_params=_params("parallel"),
        name="attn_ctx",
    )(*args)


def _stack3_lhs(hi, lo):
    return jnp.concatenate([hi, hi, lo], axis=1)


def _stack3_rhs(x):
    hi, lo = _split3(x)
    return jnp.concatenate([hi, lo, hi], axis=0)


def _fft1_kernel(x_ref, t_ref, twc_ref, tws_ref, o_ref):
    tn2 = x_ref.shape[2]
    n1 = x_ref.shape[1]
    xt = jnp.transpose(x_ref[0], (1, 0, 2))
    y = _dot(t_ref[...], _stack3_rhs(jnp.concatenate([xt[i] for i in range(tn2)], axis=1)))
    cx, sx = y[0:n1], y[n1:]
    w = 2 * C_WIDTH
    outs = []
    for i in range(tn2):
        tc = jnp.concatenate([twc_ref[i]] * (C_WIDTH // LANE), axis=1)
        ts = jnp.concatenate([tws_ref[i]] * (C_WIDTH // LANE), axis=1)
        re, im = slice(i * w, i * w + C_WIDTH), slice(i * w + C_WIDTH, (i + 1) * w)
        ar = cx[:, re] + sx[:, im]
        ai = cx[:, im] - sx[:, re]
        outs.append(jnp.concatenate([ar * tc + ai * ts, ai * tc - ar * ts], axis=1))
    o_ref[0] = jnp.transpose(jnp.stack(outs, axis=0), (1, 0, 2))


def _real_dft(t_ref, x):
    rhs = jnp.concatenate([_stack3_rhs(x[:, 0:C_WIDTH]), _stack3_rhs(x[:, C_WIDTH:])], axis=0)
    return _dot(t_ref[...], rhs)


def _fft2_kernel(x_ref, t_ref, o_ref):
    outs = [_real_dft(t_ref, x_ref[0, i]) for i in range(x_ref.shape[1])]
    o_ref[0] = jnp.transpose(jnp.stack(outs, axis=0), (1, 0, 2))


def _dft_tables(n, norm):
    k = np.arange(n, dtype=np.int64)
    ang = 2.0 * np.pi * ((k[:, None] * k[None, :]) % n).astype(np.float64) / n
    c = jnp.asarray(np.cos(ang) * norm, F32)
    s = jnp.asarray(np.sin(ang) * norm, F32)
    return _stack3_lhs(*_split3(c)), _stack3_lhs(*_split3(s))


def _fourier_latent(z):
    b, l, w = z.shape
    n1, n2 = FFT_N1, l // FFT_N1
    tn2, tk1 = 8, 8
    prod = jnp.arange(n2)[:, None] * jnp.arange(n1)[None, :]
    ang = prod.astype(F32) * (2.0 * np.pi / l)
    twc = jnp.broadcast_to(jnp.cos(ang)[:, :, None], (n2, n1, LANE))
    tws = jnp.broadcast_to(jnp.sin(ang)[:, :, None], (n2, n1, LANE))
    t1 = jnp.concatenate(_dft_tables(n1, 1.0), axis=0)
    y = pl.pallas_call(
        _fft1_kernel,
        grid=(b, n2 // tn2),
        in_specs=[pl.BlockSpec((1, n1, tn2, w), lambda b, j: (b, 0, j, 0))]
        + [_resident(t1.shape)]
        + [pl.BlockSpec((tn2, n1, LANE), lambda b, j: (j, 0, 0))] * 2,
        out_specs=pl.BlockSpec((1, n1, tn2, w), lambda b, j: (b, 0, j, 0)),
        out_shape=jax.ShapeDtypeStruct((b, n1, n2, w), F32),
        compiler_params=_params("parallel", "parallel"),
        name="fft_stage1",
    )(z.reshape(b, n1, n2, w), t1, twc, tws)
    t2m = jnp.concatenate(_dft_tables(n2, 1.0 / np.sqrt(l)), axis=1)
    out = pl.pallas_call(
        _fft2_kernel,
        grid=(b, n1 // tk1),
        in_specs=[pl.BlockSpec((1, tk1, n2, w), lambda b, j: (b, j, 0, 0)), _resident(t2m.shape)],
        out_specs=pl.BlockSpec((1, n2, tk1, C_WIDTH), lambda b, j: (b, 0, j, 0)),
        out_shape=jax.ShapeDtypeStruct((b, n2, n1, C_WIDTH), F32),
        compiler_params=_params("parallel", "parallel"),
        name="fft_stage2",
    )(y, t2m)
    return out.reshape(b, l, C_WIDTH)


def _fourier_ctx_kernel(x_ref, t_ref, o_ref):
    o_ref[0] = _real_dft(t_ref, x_ref[0])


def _fourier_ctx(z, *, batch):
    lc = z.shape[1] // batch
    tabs = jnp.concatenate(_dft_tables(lc, 1.0 / np.sqrt(lc)), axis=1)
    return pl.pallas_call(
        _fourier_ctx_kernel,
        grid=(batch,),
        in_specs=[pl.BlockSpec((1, lc, 2 * C_WIDTH), lambda b: (0, b, 0)), _resident(tabs.shape)],
        out_specs=pl.BlockSpec((1, lc, C_WIDTH), lambda b: (0, b, 0)),
        out_shape=jax.ShapeDtypeStruct((1, batch * lc, C_WIDTH), F32),
        compiler_params=_params("parallel"),
        name="fourier_ctx",
    )(z, tabs)


def _rope_tables(length):
    t = jnp.arange(length)
    row = (t // GRID_W).astype(F32)
    col = (t % GRID_W).astype(F32)
    half = HEAD_DIM // 2
    inv = ROPE_THETA ** (-jnp.arange(0, half, 2, dtype=F32) / half)
    ar, ac = row[:, None] * inv, col[:, None] * inv
    cos = jnp.concatenate([jnp.cos(ar), jnp.cos(ar), jnp.cos(ac), jnp.cos(ac)], axis=-1)
    sin = jnp.concatenate([-jnp.sin(ar), jnp.sin(ar), -jnp.sin(ac), jnp.sin(ac)], axis=-1)
    return jnp.tile(cos, (1, 2)), jnp.tile(sin, (1, 2))


def _channel_dft():
    k = np.arange(C_GROUP_DIM)
    ang = 2.0 * np.pi * ((k[:, None] * k[None, :]) % C_GROUP_DIM) / C_GROUP_DIM
    eye = np.eye(C_GROUPS)
    norm = 1.0 / np.sqrt(C_GROUP_DIM)
    cs = np.concatenate([np.kron(eye, np.cos(ang) * norm), np.kron(eye, -np.sin(ang) * norm)], axis=1)
    return _split3(jnp.asarray(cs, F32))


def _block_diag(w):
    g, c, _ = w.shape
    eye = jnp.eye(g, dtype=w.dtype)
    return (eye[:, None, :, None] * w[:, :, None, :]).reshape(g * c, g * c)


def kernel(x, c, ctx, c_ctx, w_ada, b_ada, g_ffn1, g_mix, g_ffn2, w_in, g_qn, g_kn, sink, w_four,
           w_out, w1_gate, w1_up, w1_down, w2_gate, w2_up, w2_down, g_final):
    batch, length, d = x.shape
    lc = ctx.shape[1]
    depth = w_ada.shape[0]
    assert length % (FFT_N1 * 8) == 0 and length >= BLOCK + 2 * WINDOW and batch + 1 <= 8
    tm = 512
    tm_ctx = batch * lc

    cvec = jnp.zeros((8, d), F32).at[:batch].set(c).at[batch].set(c_ctx)
    mod = _ada(cvec, w_ada, b_ada).reshape(depth, 8, N_MOD, d)

    cos, sin = _rope_tables(length)
    ones = jnp.ones((tm_ctx, LANE), F32)
    msm = _block_diag(jnp.full((2, HEAD_DIM, HEAD_DIM), 1.0 / HEAD_DIM, F32)).astype(BF16)
    wc = _wfold(*_channel_dft(), jax.vmap(_block_diag)(w_four))

    h = x
    hc = ctx.reshape(1, batch * lc, d)
    for l in range(depth):
        last = l == depth - 1
        mod_l, mod_c = mod[l, :batch], mod[l, batch:batch + 1]
        bf = lambda w: w[l].astype(BF16)
        w1 = (bf(w1_gate), bf(w1_up), bf(w1_down))
        w2 = (bf(w2_gate), bf(w2_up), bf(w2_down))
        win, wout = bf(w_in), bf(w_out)
        gq = jnp.tile(g_qn[l].reshape(1, HEAD_DIM), (1, 2))
        gk = jnp.tile(g_kn[l].reshape(1, HEAD_DIM), (1, 2))
        proj = functools.partial(_inproj, g=g_mix[l], w_in=win, gq=gq, gk=gk, msm=msm, wc=wc[l])

        h = _ffn(h, mod_l, g_ffn1[l], *w1, row0=0, tm=tm)
        hc = _ffn(hc, mod_c, g_ffn1[l], *w1, row0=0, tm=tm_ctx)

        qa, ka, vat, qb, kb, vbt, z = proj(h, mod_l, cos=cos, sin=sin, tm=tm, rope=True)
        qa_c, ka_c, vat_c, qb_c, kb_c, vbt_c, z_c = proj(hc, mod_c, cos=ones, sin=ones, tm=tm_ctx, rope=False)

        tq = 512
        sink_l = jnp.repeat(sink[l].astype(F32) * LOG2E, tq).reshape(1, N_HEADS * tq)
        oa, ob = _attn_latent(qa, ka, vat, qb, kb, vbt, ka_c, vat_c, kb_c, vbt_c, sink_l,
                              tq=tq, tk=_key_tile(length + lc))
        oc = _fourier_latent(z)
        h = _ffn(h, mod_l, g_ffn2[l], *w2, row0=6, tm=tm, mix=(oa, ob, oc, wout),
                 final_g=g_final if last else None)

        if not last:
            sink_c = jnp.repeat(sink[l].astype(F32) * LOG2E, lc).reshape(1, N_HEADS * lc)
            oa_c = _attn_ctx(qa_c, ka_c, vat_c, sink_c, batch=batch)
            ob_c = _attn_ctx(qb_c, kb_c, vbt_c, None, batch=batch)
            oc_c = _fourier_ctx(z_c, batch=batch)
            hc = _ffn(hc, mod_c, g_ffn2[l], *w2, row0=6, tm=tm_ctx, mix=(oa_c, ob_c, oc_c, wout))
    return h
```

```python
import functools

import jax
import jax.numpy as jnp
import numpy as np
from jax import lax
from jax.experimental import pallas as pl
from jax.experimental.pallas import tpu as pltpu

HEAD_DIM = 64
A_HEADS = 6
A_KV_HEADS = 2
B_HEADS = 6
B_KV_HEADS = 2
C_GROUPS = 4
C_GROUP_DIM = 64
GRID_W = 64
WINDOW = 128
BLOCK = 128
ROPE_THETA = 10000.0
NORM_EPS = 1e-6
N_MOD = 9
NEG_INF = -1e30

N_HEADS = 6
GROUP = 3
C_WIDTH = C_GROUPS * C_GROUP_DIM
FFT_N1 = 64
DEN_ROWS = 16
LOG2E = 1.4426950408889634

LANE = 128
VMEM_LIMIT = 60 * 1024 * 1024

F32 = jnp.float32
BF16 = jnp.bfloat16


def _params(*sem, flags=None):
    return pltpu.CompilerParams(dimension_semantics=tuple(sem), vmem_limit_bytes=VMEM_LIMIT, flags=flags)


def _resident(shape):
    nd = len(shape)
    return pl.BlockSpec(shape, lambda *_: (0,) * nd, pipeline_mode=pl.Buffered(1))


def _split3(x):
    hi = x.astype(BF16)
    lo = (x - hi.astype(F32)).astype(BF16)
    return hi, lo


def _dot(a, b):
    return jnp.dot(a, b, preferred_element_type=F32)


def _dot3(a_hi, a_lo, b_hi, b_lo):
    return _dot(a_hi, b_hi) + _dot(a_hi, b_lo) + _dot(a_lo, b_hi)


def _silu(x):
    return x / (1.0 + jnp.exp(-x))


def _norm_mod(x, g, shift, scale):
    ms = jnp.mean(x * x, axis=-1, keepdims=True)
    return (x * lax.rsqrt(ms + NORM_EPS)) * (g * (1.0 + scale)) + shift


def _ada_kernel(c_ref, w_ref, b_ref, o_ref):
    s_hi, s_lo = _split3(_silu(c_ref[...]))
    w_hi, w_lo = _split3(w_ref[0])
    o_ref[0] = _dot3(s_hi, s_lo, w_hi, w_lo) + b_ref[0]


def _ada(cvec, w_ada, b_ada):
    depth, d, nd = w_ada.shape
    tn = 1024
    return pl.pallas_call(
        _ada_kernel,
        grid=(depth, nd // tn),
        in_specs=[
            pl.BlockSpec((8, d), lambda l, j: (0, 0)),
            pl.BlockSpec((1, d, tn), lambda l, j: (l, 0, j)),
            pl.BlockSpec((1, 1, tn), lambda l, j: (l, 0, j)),
        ],
        out_specs=pl.BlockSpec((1, 8, tn), lambda l, j: (l, 0, j)),
        out_shape=jax.ShapeDtypeStruct((depth, 8, nd), F32),
        compiler_params=_params("parallel", "parallel"),
        name="ada",
    )(cvec, w_ada, b_ada.reshape(depth, 1, nd))


def _mix_out(oa_ref, ob_ref, oc_ref, w_ref):
    na = N_HEADS * HEAD_DIM
    return (_dot(oa_ref[0], w_ref[0:na, :]) + _dot(ob_ref[0], w_ref[na:2 * na, :])
            + _dot(oc_ref[0].astype(BF16), w_ref[2 * na:, :]))


def _ffn_kernel(x_ref, mod_ref, g_ref, wg_ref, wu_ref, wd_ref, *rest, row0, final, mixed):
    rest = list(rest)
    o_ref = rest.pop()
    x = x_ref[0]
    if mixed:
        oa_ref, ob_ref, oc_ref, wo_ref = rest[:4]
        x = x + mod_ref[0, 5:6, :] * _mix_out(oa_ref, ob_ref, oc_ref, wo_ref)
    shift = mod_ref[0, row0:row0 + 1, :]
    scale = mod_ref[0, row0 + 1:row0 + 2, :]
    gate = mod_ref[0, row0 + 2:row0 + 3, :]
    xm = _norm_mod(x, g_ref[...], shift, scale).astype(BF16)
    a = _dot(xm, wg_ref[...])
    u = _dot(xm, wu_ref[...])
    act = (_silu(a) * u).astype(BF16)
    y = x + (0.5 * gate) * _dot(act, wd_ref[...])
    if final:
        ms = jnp.mean(y * y, axis=-1, keepdims=True)
        y = (y * lax.rsqrt(ms + NORM_EPS)) * rest[-1][...]
    o_ref[0] = y


def _ffn(x, mod, g, wg, wu, wd, *, row0, tm, mix=None, final_g=None):
    groups, t, d = x.shape
    dff = wg.shape[1]
    final = final_g is not None
    tok = lambda b, i: (b, i, 0)
    in_specs = [
        pl.BlockSpec((1, tm, d), tok),
        pl.BlockSpec((1, N_MOD, d), lambda b, i: (b, 0, 0)),
        _resident((1, d)),
        _resident((d, dff)),
        _resident((d, dff)),
        _resident((dff, d)),
    ]
    args = [x, mod, g.reshape(1, d), wg, wu, wd]
    if mix is not None:
        oa, ob, oc, w_out = mix
        in_specs += [pl.BlockSpec((1, tm, oa.shape[2]), tok), pl.BlockSpec((1, tm, ob.shape[2]), tok),
                     pl.BlockSpec((1, tm, oc.shape[2]), tok), _resident(w_out.shape)]
        args += [oa, ob, oc, w_out]
    if final:
        in_specs.append(_resident((1, d)))
        args.append(final_g.reshape(1, d))
    return pl.pallas_call(
        functools.partial(_ffn_kernel, row0=row0, final=final, mixed=mix is not None),
        grid=(groups, t // tm),
        in_specs=in_specs,
        out_specs=pl.BlockSpec((1, tm, d), lambda b, i: (b, i, 0)),
        out_shape=jax.ShapeDtypeStruct((groups, t, d), F32),
        compiler_params=_params("parallel", "parallel"),
        name="ffn",
    )(*args)


def _rope_slab(t, cos, sin, lane):
    fwd = pltpu.roll(t, LANE - 16, axis=1)
    bwd = pltpu.roll(t, 16, axis=1)
    partner = jnp.where((lane % 32) < 16, fwd, bwd)
    return t * cos + partner * sin


def _wfold_kernel(cs_hi_ref, cs_lo_ref, w_ref, o_ref):
    w_hi, w_lo = _split3(w_ref[0])
    for half in range(2):
        sl = slice(half * C_WIDTH, (half + 1) * C_WIDTH)
        o_ref[0, :, sl] = _dot3(cs_hi_ref[:, sl], cs_lo_ref[:, sl], w_hi, w_lo).astype(BF16)


def _wfold(cs_hi, cs_lo, wbd):
    depth = wbd.shape[0]
    return pl.pallas_call(
        _wfold_kernel,
        grid=(depth,),
        in_specs=[_resident(cs_hi.shape), _resident(cs_lo.shape),
                  pl.BlockSpec((1, C_WIDTH, C_WIDTH), lambda l: (l, 0, 0))],
        out_specs=pl.BlockSpec((1, C_WIDTH, 2 * C_WIDTH), lambda l: (l, 0, 0)),
        out_shape=jax.ShapeDtypeStruct((depth, C_WIDTH, 2 * C_WIDTH), BF16),
        compiler_params=_params("parallel"),
        name="wfold",
    )(cs_hi, cs_lo, wbd)


def _inproj_kernel(x_ref, mod_ref, g_ref, w_ref, cos_ref, sin_ref, gq_ref, gk_ref, msm_ref, wc_ref,
                   qa_ref, ka_ref, vat_ref, qb_ref, kb_ref, vbt_ref, z_ref, *, rope):
    x = x_ref[0]
    tm = x.shape[0]
    shift = mod_ref[0, 3:4, :]
    scale = mod_ref[0, 4:5, :]
    xm = _norm_mod(x, g_ref[...], shift, scale).astype(BF16)

    u_all = _dot(xm, w_ref[...])

    def u(lo, hi):
        return u_all[:, lo:hi]

    lane = lax.broadcasted_iota(jnp.int32, (tm, LANE), 1)
    low = lane < HEAD_DIM
    if rope:
        cos = cos_ref[...]
        sin = sin_ref[...]

    def qk_norm(t, gain):
        ms = _dot((t * t).astype(BF16), msm_ref[...])
        return t * lax.rsqrt(ms + NORM_EPS) * gain

    def finish(t):
        return _rope_slab(t, cos, sin, lane) if rope else t

    def emit_q(q_ref, slabs):
        for h in range(N_HEADS):
            t = slabs[h // 2] * (LOG2E * HEAD_DIM ** -0.5)
            if (h % 2) != (h // GROUP):
                t = pltpu.roll(t, HEAD_DIM, axis=1)
            keep = low if (h // GROUP) == 0 else jnp.logical_not(low)
            q_ref[0, h] = jnp.where(keep, t, 0.0).astype(BF16)

    qa = [finish(u(LANE * i, LANE * (i + 1))) for i in range(3)]
    emit_q(qa_ref, qa)
    ka_ref[0] = finish(u(384, 512)).astype(BF16)
    vat_ref[0] = u(512, 640).T.astype(BF16)

    qb = [finish(qk_norm(u(640 + LANE * i, 640 + LANE * (i + 1)), gq_ref[...])) for i in range(3)]
    emit_q(qb_ref, qb)
    kb_ref[0] = finish(qk_norm(u(1024, 1152), gk_ref[...])).astype(BF16)
    vbt_ref[0] = u(1152, 1280).T.astype(BF16)

    z_ref[0] = _dot(u(1280, 1536).astype(BF16), wc_ref[...])


def _inproj(x, mod, g, w_in, cos, sin, gq, gk, msm, wc, *, tm, rope):
    groups, t, d = x.shape
    din = w_in.shape[1]
    tok = lambda b, i: (b, i, 0)
    in_specs = [
        pl.BlockSpec((1, tm, d), tok),
        pl.BlockSpec((1, N_MOD, d), lambda b, i: (b, 0, 0)),
        _resident((1, d)),
        _resident((d, din)),
        pl.BlockSpec((tm, LANE), lambda b, i: (i, 0)),
        pl.BlockSpec((tm, LANE), lambda b, i: (i, 0)),
        _resident((1, LANE)),
        _resident((1, LANE)),
        _resident((LANE, LANE)),
        _resident((C_WIDTH, 2 * C_WIDTH)),
    ]
    q_spec = pl.BlockSpec((1, N_HEADS, tm, LANE), lambda b, i: (b, 0, i, 0))
    k_spec = pl.BlockSpec((1, tm, LANE), tok)
    vt_spec = pl.BlockSpec((1, LANE, tm), lambda b, i: (b, 0, i))
    q_shape = jax.ShapeDtypeStruct((groups, N_HEADS, t, LANE), BF16)
    k_shape = jax.ShapeDtypeStruct((groups, t, LANE), BF16)
    vt_shape = jax.ShapeDtypeStruct((groups, LANE, t), BF16)
    return pl.pallas_call(
        functools.partial(_inproj_kernel, rope=rope),
        grid=(groups, t // tm),
        in_specs=in_specs,
        out_specs=[q_spec, k_spec, vt_spec, q_spec, k_spec, vt_spec,
                   pl.BlockSpec((1, tm, 2 * C_WIDTH), tok)],
        out_shape=[q_shape, k_shape, vt_shape, q_shape, k_shape, vt_shape,
                   jax.ShapeDtypeStruct((groups, t, 2 * C_WIDTH), F32)],
        compiler_params=_params("parallel", "parallel"),
        name="inproj",
    )(x, mod, g.reshape(1, d), w_in, cos, sin, gq, gk, msm, wc)


def _nt(a, b):
    return lax.dot_general(a, b, (((1,), (1,)), ((), ())), preferred_element_type=F32)


def _store_heads(o_ref, ot):
    for pair in range(N_HEADS // 2):
        slab = jnp.concatenate([ot[2 * pair], ot[2 * pair + 1]], axis=0)
        o_ref[0, :, pair * LANE:(pair + 1) * LANE] = slab.T.astype(o_ref.dtype)


def _attn_latent_kernel(qa_ref, qb_ref, ka_ref, vat_ref, kb_ref, vbt_ref, kac_ref, vatc_ref, kbc_ref, vbtc_ref,
                        sink_ref, oa_ref, ob_ref, s_a, s_b, p_a, p_b, m_sc, bmax_sc, alpha_sc, acc_sc, *, tk):
    tq = qb_ref.shape[2]
    length = kb_ref.shape[1]
    lc = kbc_ref.shape[1]
    nk = (length + lc) // tk
    tail = length - (nk - 1) * tk
    span = tq + 2 * WINDOW
    na = span + lc
    gw = GROUP * tq
    nq = N_HEADS * tq
    qa = qa_ref[0].reshape(N_HEADS * tq, LANE)
    qb = qb_ref[0].reshape(N_HEADS * tq, LANE)
    tile = pl.program_id(1)
    start = pl.multiple_of(jnp.clip(tile * tq - WINDOW, 0, length - span), LANE)
    m_sc[...] = jnp.full_like(m_sc, NEG_INF)
    acc_sc[...] = jnp.zeros_like(acc_sc)

    def with_ones(vt):
        return jnp.concatenate([vt, jnp.ones((DEN_ROWS, vt.shape[1]), BF16)], axis=0)

    def rows(g):
        return slice(g * HEAD_DIM, (g + 1) * HEAD_DIM)

    def cols(g):
        return slice(g * gw, (g + 1) * gw)

    def scores_win(s_buf, slot):
        s_w = _nt(ka_ref[0, pl.ds(start, span), :], qa)
        kpos = start + lax.broadcasted_iota(jnp.int32, s_w.shape, 0)
        qpos = tile * tq + lax.broadcasted_iota(jnp.int32, s_w.shape, 1) % tq
        s_w = jnp.where(jnp.abs(kpos - qpos) <= WINDOW, s_w, NEG_INF)
        s_c = _nt(kac_ref[0], qa)
        s_buf[0:span, 0:nq] = s_w
        s_buf[span:na, 0:nq] = s_c
        bmax_sc[slot] = jnp.maximum(jnp.max(s_w, axis=0, keepdims=True), jnp.max(s_c, axis=0, keepdims=True))

    def softmax_win(s_buf, p_buf, slot):
        sink = sink_ref[...]
        m = jnp.maximum(sink, bmax_sc[slot])
        alpha_sc[slot] = jnp.exp2(sink - m)
        p_buf[0:na, 0:nq] = jnp.exp2(s_buf[0:na, 0:nq] - m).astype(BF16)

    def values_win(p_buf, slot):
        heads = []
        for g in range(2):
            vt = jnp.concatenate([vat_ref[0, rows(g), pl.ds(start, span)], vatc_ref[0, rows(g), :]], axis=1)
            acc = _dot(with_ones(vt), p_buf[0:na, cols(g)])
            o = acc[0:HEAD_DIM] / (acc[HEAD_DIM:HEAD_DIM + 1] + alpha_sc[slot][:, cols(g)])
            heads += [o[:, j * tq:(j + 1) * tq] for j in range(GROUP)]
        _store_heads(oa_ref, heads)

    def key_block(blk):
        if isinstance(blk, int) and blk == nk - 1:
            return jnp.concatenate([kb_ref[0, length - tail:, :], kbc_ref[0]], axis=0)
        return kb_ref[0, pl.ds(pl.multiple_of(blk * tk, tk), tk), :]

    def value_block(blk, g):
        if isinstance(blk, int) and blk == nk - 1:
            return jnp.concatenate([vbt_ref[0, rows(g), length - tail:], vbtc_ref[0, rows(g), :]], axis=1)
        return vbt_ref[0, rows(g), pl.ds(pl.multiple_of(blk * tk, tk), tk)]

    def scores(blk, s_buf, slot):
        s = _nt(key_block(blk), qb)
        s_buf[0:tk, 0:nq] = s
        bmax_sc[slot] = jnp.max(s, axis=0, keepdims=True)

    def softmax(s_buf, p_buf, slot):
        m_old = m_sc[...]
        m_new = jnp.maximum(m_old, bmax_sc[slot])
        m_sc[...] = m_new
        alpha_sc[slot] = jnp.exp2(m_old - m_new)
        p_buf[0:tk, 0:nq] = jnp.exp2(s_buf[0:tk, 0:nq] - m_new).astype(BF16)

    def values(blk, p_buf, slot):
        for g in range(2):
            vt = with_ones(value_block(blk, g))
            acc_sc[g] = acc_sc[g] * alpha_sc[slot][:, cols(g)] + _dot(vt, p_buf[0:tk, cols(g)])

    bufs = ((s_a, p_a), (s_b, p_b))
    n_steps = nk + 1

    def step(t, par, do_scores=True):
        (s_cur, p_cur), (s_oth, p_oth) = bufs[par], bufs[1 - par]
        if do_scores:
            scores(t, s_oth, 1 - par)
        softmax(s_cur, p_cur, par)
        values(t - 2, p_oth, 1 - par)

    scores_win(s_a, 0)
    scores(0, s_b, 1)
    softmax_win(s_a, p_a, 0)
    if nk >= 2:
        scores(1, s_a, 0)
    softmax(s_b, p_b, 1)
    values_win(p_a, 0)
    n_pairs = max(0, (nk - 3) // 2)

    def pair(i, carry):
        t = 2 + 2 * i
        step(t, 0)
        step(t + 1, 1)
        return carry

    lax.fori_loop(0, n_pairs, pair, 0)
    for t in range(2 + 2 * n_pairs, n_steps):
        step(t, t % 2, do_scores=t + 1 < n_steps)
    last = (n_steps - 1) % 2
    values(nk - 1, bufs[last][1], last)

    _store_heads(ob_ref, [acc_sc[h // GROUP, 0:HEAD_DIM, (h % GROUP) * tq:(h % GROUP + 1) * tq]
                          / acc_sc[h // GROUP, HEAD_DIM:HEAD_DIM + 1, (h % GROUP) * tq:(h % GROUP + 1) * tq]
                          for h in range(N_HEADS)])


def _key_tile(n_keys, cap=1024):
    return max(t for t in range(LANE, cap + 1, LANE) if n_keys % t == 0)


def _attn_latent(qa, ka, vat, qb, kb, vbt, ka_c, vat_c, kb_c, vbt_c, sink_row, *, tq, tk):
    b, _, l, _ = qb.shape
    lc = ka_c.shape[1] // b
    assert (l + lc) % tk == 0 and (l + lc) // tk * tk - tk <= l and tk > lc
    buf_rows = max(tk, tq + 2 * WINDOW + lc)
    buf_cols = N_HEADS * tq + LANE
    q_spec = pl.BlockSpec((1, N_HEADS, tq, LANE), lambda b, i: (b, 0, i, 0))
    k_spec = pl.BlockSpec((1, l, LANE), lambda b, i: (b, 0, 0), pipeline_mode=pl.Buffered(1))
    vt_spec = pl.BlockSpec((1, LANE, l), lambda b, i: (b, 0, 0), pipeline_mode=pl.Buffered(1))
    kc_spec = pl.BlockSpec((1, lc, LANE), lambda b, i: (0, b, 0), pipeline_mode=pl.Buffered(1))
    vtc_spec = pl.BlockSpec((1, LANE, lc), lambda b, i: (0, 0, b), pipeline_mode=pl.Buffered(1))
    o_spec = pl.BlockSpec((1, tq, N_HEADS * HEAD_DIM), lambda b, i: (b, i, 0))
    o_shape = jax.ShapeDtypeStruct((b, l, N_HEADS * HEAD_DIM), BF16)
    return pl.pallas_call(
        functools.partial(_attn_latent_kernel, tk=tk),
        grid=(b, l // tq),
        in_specs=[q_spec, q_spec, k_spec, vt_spec, k_spec, vt_spec, kc_spec, vtc_spec, kc_spec, vtc_spec,
                  pl.BlockSpec((1, N_HEADS * tq), lambda b, i: (0, 0))],
        out_specs=[o_spec, o_spec],
        out_shape=[o_shape, o_shape],
        scratch_shapes=[
            pltpu.VMEM((buf_rows, buf_cols), F32),
            pltpu.VMEM((buf_rows, buf_cols), F32),
            pltpu.VMEM((buf_rows, buf_cols), BF16),
            pltpu.VMEM((buf_rows, buf_cols), BF16),
            pltpu.VMEM((1, N_HEADS * tq), F32),
            pltpu.VMEM((2, 1, N_HEADS * tq), F32),
            pltpu.VMEM((2, 1, N_HEADS * tq), F32),
            pltpu.VMEM((2, HEAD_DIM + DEN_ROWS, GROUP * tq), F32),
        ],
        compiler_params=_params("parallel", "parallel"),
        name="attn_latent",
    )(qa, qb, ka, vat, kb, vbt, ka_c, vat_c, kb_c, vbt_c, sink_row)


def _softmax_pv(q_ref, ks, vts, extra, mask=None):
    tq = q_ref.shape[2]
    aug = [[jnp.concatenate([vt[g * HEAD_DIM:(g + 1) * HEAD_DIM, :],
                             jnp.ones((DEN_ROWS, vt.shape[1]), BF16)], axis=0) for vt in vts]
           for g in range(2)]
    out = []
    for h in range(N_HEADS):
        ss = [_nt(k, q_ref[0, h]) for k in ks]
        if mask is not None:
            ss[0] = jnp.where(mask, ss[0], NEG_INF)
        m = functools.reduce(jnp.maximum, [jnp.max(s, axis=0, keepdims=True) for s in ss])
        if extra is not None:
            e = extra[:, h * tq:(h + 1) * tq]
            m = jnp.maximum(m, e)
        acc = functools.reduce(
            jnp.add, [_dot(vt, jnp.exp2(s - m).astype(BF16)) for s, vt in zip(ss, aug[h // GROUP])])
        den = acc[HEAD_DIM:HEAD_DIM + 1]
        if extra is not None:
            den = den + jnp.exp2(e - m)
        out.append(acc[0:HEAD_DIM] / den)
    return out


def _attn_ctx_kernel(q_ref, k_ref, vt_ref, *rest, has_sink):
    if has_sink:
        sink_ref, o_ref = rest
    else:
        (o_ref,) = rest
    extra = sink_ref[...] if has_sink else None
    _store_heads(o_ref, _softmax_pv(q_ref, [k_ref[0]], [vt_ref[0]], extra))


def _attn_ctx(q, k, vt, sink_row, *, batch):
    lc = q.shape[2] // batch
    in_specs = [
        pl.BlockSpec((1, N_HEADS, lc, LANE), lambda b: (0, 0, b, 0)),
        pl.BlockSpec((1, lc, LANE), lambda b: (0, b, 0)),
        pl.BlockSpec((1, LANE, lc), lambda b: (0, 0, b)),
    ]
    args = [q, k, vt]
    if sink_row is not None:
        in_specs.append(pl.BlockSpec((1, N_HEADS * lc), lambda b: (0, 0)))
        args.append(sink_row)
    return pl.pallas_call(
        functools.partial(_attn_ctx_kernel, has_sink=sink_row is not None),
        grid=(batch,),
        in_specs=in_specs,
        out_specs=pl.BlockSpec((1, lc, N_HEADS * HEAD_DIM), lambda b: (0, b, 0)),
        out_shape=jax.ShapeDtypeStruct((1, batch * lc, N_HEADS * HEAD_DIM), BF16),
        compiler_params=_params("parallel"),
        name="attn_ctx",
    )(*args)


def _stack3_lhs(hi, lo):
    return jnp.concatenate([hi, hi, lo], axis=1)


def _stack3_rhs(x):
    hi, lo = _split3(x)
    return jnp.concatenate([hi, lo, hi], axis=0)


def _fft1_kernel(x_ref, t_ref, twc_ref, tws_ref, o_ref):
    tn2 = x_ref.shape[2]
    n1 = x_ref.shape[1]
    xt = jnp.transpose(x_ref[0], (1, 0, 2))
    y = _dot(t_ref[...], _stack3_rhs(jnp.concatenate([xt[i] for i in range(tn2)], axis=1)))
    cx, sx = y[0:n1], y[n1:]
    w = 2 * C_WIDTH
    outs = []
    for i in range(tn2):
        tc = jnp.concatenate([twc_ref[i]] * (C_WIDTH // LANE), axis=1)
        ts = jnp.concatenate([tws_ref[i]] * (C_WIDTH // LANE), axis=1)
        re, im = slice(i * w, i * w + C_WIDTH), slice(i * w + C_WIDTH, (i + 1) * w)
        ar = cx[:, re] + sx[:, im]
        ai = cx[:, im] - sx[:, re]
        outs.append(jnp.concatenate([ar * tc + ai * ts, ai * tc - ar * ts], axis=1))
    o_ref[0] = jnp.transpose(jnp.stack(outs, axis=0), (1, 0, 2))


def _real_dft(t_ref, x):
    rhs = jnp.concatenate([_stack3_rhs(x[:, 0:C_WIDTH]), _stack3_rhs(x[:, C_WIDTH:])], axis=0)
    return _dot(t_ref[...], rhs)


def _fft2_kernel(x_ref, t_ref, o_ref):
    outs = [_real_dft(t_ref, x_ref[0, i]) for i in range(x_ref.shape[1])]
    o_ref[0] = jnp.transpose(jnp.stack(outs, axis=0), (1, 0, 2))


def _dft_tables(n, norm):
    k = np.arange(n, dtype=np.int64)
    ang = 2.0 * np.pi * ((k[:, None] * k[None, :]) % n).astype(np.float64) / n
    c = jnp.asarray(np.cos(ang) * norm, F32)
    s = jnp.asarray(np.sin(ang) * norm, F32)
    return _stack3_lhs(*_split3(c)), _stack3_lhs(*_split3(s))


def _fourier_latent(z):
    b, l, w = z.shape
    n1, n2 = FFT_N1, l // FFT_N1
    tn2, tk1 = 8, 8
    prod = jnp.arange(n2)[:, None] * jnp.arange(n1)[None, :]
    ang = prod.astype(F32) * (2.0 * np.pi / l)
    twc = jnp.broadcast_to(jnp.cos(ang)[:, :, None], (n2, n1, LANE))
    tws = jnp.broadcast_to(jnp.sin(ang)[:, :, None], (n2, n1, LANE))
    t1 = jnp.concatenate(_dft_tables(n1, 1.0), axis=0)
    y = pl.pallas_call(
        _fft1_kernel,
        grid=(b, n2 // tn2),
        in_specs=[pl.BlockSpec((1, n1, tn2, w), lambda b, j: (b, 0, j, 0))]
        + [_resident(t1.shape)]
        + [pl.BlockSpec((tn2, n1, LANE), lambda b, j: (j, 0, 0))] * 2,
        out_specs=pl.BlockSpec((1, n1, tn2, w), lambda b, j: (b, 0, j, 0)),
        out_shape=jax.ShapeDtypeStruct((b, n1, n2, w), F32),
        compiler_params=_params("parallel", "parallel"),
        name="fft_stage1",
    )(z.reshape(b, n1, n2, w), t1, twc, tws)
    t2m = jnp.concatenate(_dft_tables(n2, 1.0 / np.sqrt(l)), axis=1)
    out = pl.pallas_call(
        _fft2_kernel,
        grid=(b, n1 // tk1),
        in_specs=[pl.BlockSpec((1, tk1, n2, w), lambda b, j: (b, j, 0, 0)), _resident(t2m.shape)],
        out_specs=pl.BlockSpec((1, n2, tk1, C_WIDTH), lambda b, j: (b, 0, j, 0)),
        out_shape=jax.ShapeDtypeStruct((b, n2, n1, C_WIDTH), F32),
        compiler_params=_params("parallel", "parallel"),
        name="fft_stage2",
    )(y, t2m)
    return out.reshape(b, l, C_WIDTH)


def _fourier_ctx_kernel(x_ref, t_ref, o_ref):
    o_ref[0] = _real_dft(t_ref, x_ref[0])


def _fourier_ctx(z, *, batch):
    lc = z.shape[1] // batch
    tabs = jnp.concatenate(_dft_tables(lc, 1.0 / np.sqrt(lc)), axis=1)
    return pl.pallas_call(
        _fourier_ctx_kernel,
        grid=(batch,),
        in_specs=[pl.BlockSpec((1, lc, 2 * C_WIDTH), lambda b: (0, b, 0)), _resident(tabs.shape)],
        out_specs=pl.BlockSpec((1, lc, C_WIDTH), lambda b: (0, b, 0)),
        out_shape=jax.ShapeDtypeStruct((1, batch * lc, C_WIDTH), F32),
        compiler_params=_params("parallel"),
        name="fourier_ctx",
    )(z, tabs)


def _rope_tables(length):
    t = jnp.arange(length)
    row = (t // GRID_W).astype(F32)
    col = (t % GRID_W).astype(F32)
    half = HEAD_DIM // 2
    inv = ROPE_THETA ** (-jnp.arange(0, half, 2, dtype=F32) / half)
    ar, ac = row[:, None] * inv, col[:, None] * inv
    cos = jnp.concatenate([jnp.cos(ar), jnp.cos(ar), jnp.cos(ac), jnp.cos(ac)], axis=-1)
    sin = jnp.concatenate([-jnp.sin(ar), jnp.sin(ar), -jnp.sin(ac), jnp.sin(ac)], axis=-1)
    return jnp.tile(cos, (1, 2)), jnp.tile(sin, (1, 2))


def _channel_dft():
    k = np.arange(C_GROUP_DIM)
    ang = 2.0 * np.pi * ((k[:, None] * k[None, :]) % C_GROUP_DIM) / C_GROUP_DIM
    eye = np.eye(C_GROUPS)
    norm = 1.0 / np.sqrt(C_GROUP_DIM)
    cs = np.concatenate([np.kron(eye, np.cos(ang) * norm), np.kron(eye, -np.sin(ang) * norm)], axis=1)
    return _split3(jnp.asarray(cs, F32))


def _block_diag(w):
    g, c, _ = w.shape
    eye = jnp.eye(g, dtype=w.dtype)
    return (eye[:, None, :, None] * w[:, :, None, :]).reshape(g * c, g * c)


def kernel(x, c, ctx, c_ctx, w_ada, b_ada, g_ffn1, g_mix, g_ffn2, w_in, g_qn, g_kn, sink, w_four,
           w_out, w1_gate, w1_up, w1_down, w2_gate, w2_up, w2_down, g_final):
    batch, length, d = x.shape
    lc = ctx.shape[1]
    depth = w_ada.shape[0]
    assert length % (FFT_N1 * 8) == 0 and length >= BLOCK + 2 * WINDOW and batch + 1 <= 8
    tm = 512
    tm_ctx = batch * lc

    cvec = jnp.zeros((8, d), F32).at[:batch].set(c).at[batch].set(c_ctx)
    mod = _ada(cvec, w_ada, b_ada).reshape(depth, 8, N_MOD, d)

    cos, sin = _rope_tables(length)
    ones = jnp.ones((tm_ctx, LANE), F32)
    msm = _block_diag(jnp.full((2, HEAD_DIM, HEAD_DIM), 1.0 / HEAD_DIM, F32)).astype(BF16)
    wc = _wfold(*_channel_dft(), jax.vmap(_block_diag)(w_four))

    h = x
    hc = ctx.reshape(1, batch * lc, d)
    for l in range(depth):
        last = l == depth - 1
        mod_l, mod_c = mod[l, :batch], mod[l, batch:batch + 1]
        bf = lambda w: w[l].astype(BF16)
        w1 = (bf(w1_gate), bf(w1_up), bf(w1_down))
        w2 = (bf(w2_gate), bf(w2_up), bf(w2_down))
        win, wout = bf(w_in), bf(w_out)
        gq = jnp.tile(g_qn[l].reshape(1, HEAD_DIM), (1, 2))
        gk = jnp.tile(g_kn[l].reshape(1, HEAD_DIM), (1, 2))
        proj = functools.partial(_inproj, g=g_mix[l], w_in=win, gq=gq, gk=gk, msm=msm, wc=wc[l])

        h = _ffn(h, mod_l, g_ffn1[l], *w1, row0=0, tm=tm)
        hc = _ffn(hc, mod_c, g_ffn1[l], *w1, row0=0, tm=tm_ctx)

        qa, ka, vat, qb, kb, vbt, z = proj(h, mod_l, cos=cos, sin=sin, tm=tm, rope=True)
        qa_c, ka_c, vat_c, qb_c, kb_c, vbt_c, z_c = proj(hc, mod_c, cos=ones, sin=ones, tm=tm_ctx, rope=False)

        tq = 512
        sink_l = jnp.repeat(sink[l].astype(F32) * LOG2E, tq).reshape(1, N_HEADS * tq)
        oa, ob = _attn_latent(qa, ka, vat, qb, kb, vbt, ka_c, vat_c, kb_c, vbt_c, sink_l,
                              tq=tq, tk=_key_tile(length + lc))
        oc = _fourier_latent(z)
        h = _ffn(h, mod_l, g_ffn2[l], *w2, row0=6, tm=tm, mix=(oa, ob, oc, wout),
                 final_g=g_final if last else None)

        if not last:
            sink_c = jnp.repeat(sink[l].astype(F32) * LOG2E, lc).reshape(1, N_HEADS * lc)
            oa_c = _attn_ctx(qa_c, ka_c, vat_c, sink_c, batch=batch)
            ob_c = _attn_ctx(qb_c, kb_c, vbt_c, None, batch=batch)
            oc_c = _fourier_ctx(z_c, batch=batch)
            hc = _ffn(hc, mod_c, g_ffn2[l], *w2, row0=6, tm=tm_ctx, mix=(oa_c, ob_c, oc_c, wout))
    return h
```

```python
import functools

import jax
import jax.numpy as jnp
import numpy as np
from jax import lax
from jax.experimental import pallas as pl
from jax.experimental.pallas import tpu as pltpu

HEAD_DIM = 64
A_HEADS = 6
A_KV_HEADS = 2
B_HEADS = 6
B_KV_HEADS = 2
C_GROUPS = 4
C_GROUP_DIM = 64
GRID_W = 64
WINDOW = 128
BLOCK = 128
ROPE_THETA = 10000.0
NORM_EPS = 1e-6
N_MOD = 9
NEG_INF = -1e30

N_HEADS = 6
GROUP = 3
C_WIDTH = C_GROUPS * C_GROUP_DIM
FFT_N1 = 64
DEN_ROWS = 16
LOG2E = 1.4426950408889634

LANE = 128
VMEM_LIMIT = 60 * 1024 * 1024

F32 = jnp.float32
BF16 = jnp.bfloat16


def _params(*sem, flags=None):
    return pltpu.CompilerParams(dimension_semantics=tuple(sem), vmem_limit_bytes=VMEM_LIMIT, flags=flags)


def _resident(shape):
    nd = len(shape)
    return pl.BlockSpec(shape, lambda *_: (0,) * nd, pipeline_mode=pl.Buffered(1))


def _layer_resident(stacked, layer):
    nd = stacked.ndim
    return pl.BlockSpec((None,) + stacked.shape[1:], lambda *_: (layer,) + (0,) * (nd - 1),
                        pipeline_mode=pl.Buffered(1))


def _split3(x):
    hi = x.astype(BF16)
    lo = (x - hi.astype(F32)).astype(BF16)
    return hi, lo


def _dot(a, b):
    return jnp.dot(a, b, preferred_element_type=F32)


def _dot3(a_hi, a_lo, b_hi, b_lo):
    return _dot(a_hi, b_hi) + _dot(a_hi, b_lo) + _dot(a_lo, b_hi)


def _silu(x):
    return x / (1.0 + jnp.exp(-x))


def _norm_mod(x, g, shift, scale):
    ms = jnp.mean(x * x, axis=-1, keepdims=True)
    return (x * lax.rsqrt(ms + NORM_EPS)) * (g * (1.0 + scale)) + shift


def _ada_kernel(c_ref, w_ref, b_ref, o_ref):
    s_hi, s_lo = _split3(_silu(c_ref[...]))
    w_hi, w_lo = _split3(w_ref[0])
    o_ref[0] = _dot3(s_hi, s_lo, w_hi, w_lo) + b_ref[0]


def _ada(cvec, w_ada, b_ada):
    depth, d, nd = w_ada.shape
    tn = 1024
    return pl.pallas_call(
        _ada_kernel,
        grid=(depth, nd // tn),
        in_specs=[
            pl.BlockSpec((8, d), lambda l, j: (0, 0)),
            pl.BlockSpec((1, d, tn), lambda l, j: (l, 0, j)),
            pl.BlockSpec((1, 1, tn), lambda l, j: (l, 0, j)),
        ],
        out_specs=pl.BlockSpec((1, 8, tn), lambda l, j: (l, 0, j)),
        out_shape=jax.ShapeDtypeStruct((depth, 8, nd), F32),
        compiler_params=_params("parallel", "parallel"),
        name="ada",
    )(cvec, w_ada, b_ada.reshape(depth, 1, nd))


def _mix_out(oa_ref, ob_ref, oc_ref, w_ref):
    na = N_HEADS * HEAD_DIM
    return (_dot(oa_ref[0], w_ref[0:na, :]) + _dot(ob_ref[0], w_ref[na:2 * na, :])
            + _dot(oc_ref[0].astype(BF16), w_ref[2 * na:, :]))


N_PROJ_IN = 8


def _ffn_kernel(*refs, row0, final, mixed, rope):
    refs = list(refs)
    x_ref, mod_ref, g_ref, wg_ref, wu_ref, wd_ref = refs[:6]
    del refs[:6]
    x = x_ref[0]
    if mixed:
        oa_ref, ob_ref, oc_ref, wo_ref = refs[:4]
        del refs[:4]
        x = x + mod_ref[0, 5:6, :] * _mix_out(oa_ref, ob_ref, oc_ref, wo_ref)
    if final:
        gf_ref = refs.pop(0)
    proj_in = refs[:N_PROJ_IN] if rope is not None else []
    o_ref = refs[len(proj_in)]
    proj_out = refs[len(proj_in) + 1:]
    shift = mod_ref[0, row0:row0 + 1, :]
    scale = mod_ref[0, row0 + 1:row0 + 2, :]
    gate = mod_ref[0, row0 + 2:row0 + 3, :]
    xm = _norm_mod(x, g_ref[...], shift, scale).astype(BF16)
    a = _dot(xm, wg_ref[...])
    u = _dot(xm, wu_ref[...])
    act = (_silu(a) * u).astype(BF16)
    y = x + (0.5 * gate) * _dot(act, wd_ref[...])
    if final:
        ms = jnp.mean(y * y, axis=-1, keepdims=True)
        y = (y * lax.rsqrt(ms + NORM_EPS)) * gf_ref[...]
    o_ref[0] = y
    if rope is not None:
        _proj_body(y, mod_ref, *proj_in, *proj_out, rope=rope)


def _ffn(x, mod, g, wg, wu, wd, *, layer, row0, tm, mix=None, final_g=None, proj=None):
    groups, t, d = x.shape
    final = final_g is not None
    tok = lambda b, i: (b, i, 0)
    in_specs = [
        pl.BlockSpec((1, tm, d), tok),
        pl.BlockSpec((1, N_MOD, d), lambda b, i: (b, 0, 0)),
        _resident((1, d)),
        _layer_resident(wg, layer),
        _layer_resident(wu, layer),
        _layer_resident(wd, layer),
    ]
    args = [x, mod, g.reshape(1, d), wg, wu, wd]
    if mix is not None:
        oa, ob, oc, w_out = mix
        in_specs += [pl.BlockSpec((1, tm, oa.shape[2]), tok), pl.BlockSpec((1, tm, ob.shape[2]), tok),
                     pl.BlockSpec((1, tm, oc.shape[2]), tok), _layer_resident(w_out, layer)]
        args += [oa, ob, oc, w_out]
    if final:
        in_specs.append(_resident((1, d)))
        args.append(final_g.reshape(1, d))
    out_specs = [pl.BlockSpec((1, tm, d), tok)]
    out_shape = [jax.ShapeDtypeStruct((groups, t, d), F32)]
    if proj is not None:
        in_specs += [
            _resident((1, d)),
            _layer_resident(proj["w_in"], layer),
            pl.BlockSpec((tm, LANE), lambda b, i: (i, 0)),
            pl.BlockSpec((tm, LANE), lambda b, i: (i, 0)),
            _resident((1, LANE)),
            _resident((1, LANE)),
            _resident((LANE, LANE)),
            _layer_resident(proj["wc"], layer),
        ]
        args += [proj["g"].reshape(1, d), proj["w_in"], proj["cos"], proj["sin"], proj["gq"], proj["gk"],
                 proj["msm"], proj["wc"]]
        q_spec = pl.BlockSpec((1, N_HEADS, tm, LANE), lambda b, i: (b, 0, i, 0))
        k_spec = pl.BlockSpec((1, tm, LANE), tok)
        vt_spec = pl.BlockSpec((1, LANE, tm), lambda b, i: (b, 0, i))
        q_shape = jax.ShapeDtypeStruct((groups, N_HEADS, t, LANE), BF16)
        k_shape = jax.ShapeDtypeStruct((groups, t, LANE), BF16)
        vt_shape = jax.ShapeDtypeStruct((groups, LANE, t), BF16)
        out_specs += [q_spec, k_spec, vt_spec, q_spec, k_spec, vt_spec, pl.BlockSpec((1, tm, 2 * C_WIDTH), tok)]
        out_shape += [q_shape, k_shape, vt_shape, q_shape, k_shape, vt_shape,
                      jax.ShapeDtypeStruct((groups, t, 2 * C_WIDTH), F32)]
    out = pl.pallas_call(
        functools.partial(_ffn_kernel, row0=row0, final=final, mixed=mix is not None,
                          rope=None if proj is None else proj["rope"]),
        grid=(groups, t // tm),
        in_specs=in_specs,
        out_specs=out_specs,
        out_shape=out_shape,
        compiler_params=_params("parallel", "parallel"),
        name="ffn",
    )(*args)
    return out[0] if proj is None else out


def _rope_slab(t, cos, sin, lane):
    fwd = pltpu.roll(t, LANE - 16, axis=1)
    bwd = pltpu.roll(t, 16, axis=1)
    partner = jnp.where((lane % 32) < 16, fwd, bwd)
    return t * cos + partner * sin


def _wfold_kernel(cs_hi_ref, cs_lo_ref, w_ref, o_ref):
    w_hi, w_lo = _split3(w_ref[0])
    for half in range(2):
        sl = slice(half * C_WIDTH, (half + 1) * C_WIDTH)
        o_ref[0, :, sl] = _dot3(cs_hi_ref[:, sl], cs_lo_ref[:, sl], w_hi, w_lo).astype(BF16)


def _wfold(cs_hi, cs_lo, wbd):
    depth = wbd.shape[0]
    return pl.pallas_call(
        _wfold_kernel,
        grid=(depth,),
        in_specs=[_resident(cs_hi.shape), _resident(cs_lo.shape),
                  pl.BlockSpec((1, C_WIDTH, C_WIDTH), lambda l: (l, 0, 0))],
        out_specs=pl.BlockSpec((1, C_WIDTH, 2 * C_WIDTH), lambda l: (l, 0, 0)),
        out_shape=jax.ShapeDtypeStruct((depth, C_WIDTH, 2 * C_WIDTH), BF16),
        compiler_params=_params("parallel"),
        name="wfold",
    )(cs_hi, cs_lo, wbd)


def _proj_body(x, mod_ref, g_ref, w_ref, cos_ref, sin_ref, gq_ref, gk_ref, msm_ref, wc_ref,
               qa_ref, ka_ref, vat_ref, qb_ref, kb_ref, vbt_ref, z_ref, *, rope):
    tm = x.shape[0]
    shift = mod_ref[0, 3:4, :]
    scale = mod_ref[0, 4:5, :]
    xm = _norm_mod(x, g_ref[...], shift, scale).astype(BF16)

    u_all = _dot(xm, w_ref[...])

    def u(lo, hi):
        return u_all[:, lo:hi]

    lane = lax.broadcasted_iota(jnp.int32, (tm, LANE), 1)
    low = lane < HEAD_DIM
    if rope:
        cos = cos_ref[...]
        sin = sin_ref[...]

    def qk_norm(t, gain):
        ms = _dot((t * t).astype(BF16), msm_ref[...])
        return t * lax.rsqrt(ms + NORM_EPS) * gain

    def finish(t):
        return _rope_slab(t, cos, sin, lane) if rope else t

    def emit_q(q_ref, slabs):
        for h in range(N_HEADS):
            t = slabs[h // 2] * (LOG2E * HEAD_DIM ** -0.5)
            if (h % 2) != (h // GROUP):
                t = pltpu.roll(t, HEAD_DIM, axis=1)
            keep = low if (h // GROUP) == 0 else jnp.logical_not(low)
            q_ref[0, h] = jnp.where(keep, t, 0.0).astype(BF16)

    qa = [finish(u(LANE * i, LANE * (i + 1))) for i in range(3)]
    emit_q(qa_ref, qa)
    ka_ref[0] = finish(u(384, 512)).astype(BF16)
    vat_ref[0] = u(512, 640).T.astype(BF16)

    qb = [finish(qk_norm(u(640 + LANE * i, 640 + LANE * (i + 1)), gq_ref[...])) for i in range(3)]
    emit_q(qb_ref, qb)
    kb_ref[0] = finish(qk_norm(u(1024, 1152), gk_ref[...])).astype(BF16)
    vbt_ref[0] = u(1152, 1280).T.astype(BF16)

    z_ref[0] = _dot(u(1280, 1536).astype(BF16), wc_ref[...])


def _nt(a, b):
    return lax.dot_general(a, b, (((1,), (1,)), ((), ())), preferred_element_type=F32)


def _store_heads(o_ref, ot):
    for pair in range(N_HEADS // 2):
        slab = jnp.concatenate([ot[2 * pair], ot[2 * pair + 1]], axis=0)
        o_ref[0, :, pair * LANE:(pair + 1) * LANE] = slab.T.astype(o_ref.dtype)


def _attn_latent_kernel(qa_ref, qb_ref, ka_ref, vat_ref, kb_ref, vbt_ref, kac_ref, vatc_ref, kbc_ref, vbtc_ref,
                        sink_ref, oa_ref, ob_ref, s_a, s_b, p_a, p_b, m_sc, bmax_sc, alpha_sc, acc_sc, *, tk):
    tq = qb_ref.shape[2]
    length = kb_ref.shape[1]
    lc = kbc_ref.shape[1]
    nk = (length + lc) // tk
    tail = length - (nk - 1) * tk
    span = tq + 2 * WINDOW
    na = span + lc
    gw = GROUP * tq
    nq = N_HEADS * tq
    qa = qa_ref[0].reshape(N_HEADS * tq, LANE)
    qb = qb_ref[0].reshape(N_HEADS * tq, LANE)
    tile = pl.program_id(1)
    start = pl.multiple_of(jnp.clip(tile * tq - WINDOW, 0, length - span), LANE)
    m_sc[...] = jnp.full_like(m_sc, NEG_INF)
    acc_sc[...] = jnp.zeros_like(acc_sc)

    def with_ones(vt):
        return jnp.concatenate([vt, jnp.ones((DEN_ROWS, vt.shape[1]), BF16)], axis=0)

    def rows(g):
        return slice(g * HEAD_DIM, (g + 1) * HEAD_DIM)

    def cols(g):
        return slice(g * gw, (g + 1) * gw)

    def scores_win(s_buf, slot):
        s_w = _nt(ka_ref[0, pl.ds(start, span), :], qa)
        kpos = start + lax.broadcasted_iota(jnp.int32, s_w.shape, 0)
        qpos = tile * tq + lax.broadcasted_iota(jnp.int32, s_w.shape, 1) % tq
        s_w = jnp.where(jnp.abs(kpos - qpos) <= WINDOW, s_w, NEG_INF)
        s_c = _nt(kac_ref[0], qa)
        s_buf[0:span, 0:nq] = s_w
        s_buf[span:na, 0:nq] = s_c
        bmax_sc[slot] = jnp.maximum(jnp.max(s_w, axis=0, keepdims=True), jnp.max(s_c, axis=0, keepdims=True))

    def softmax_win(s_buf, p_buf, slot):
        sink = sink_ref[...]
        m = jnp.maximum(sink, bmax_sc[slot])
        alpha_sc[slot] = jnp.exp2(sink - m)
        p_buf[0:na, 0:nq] = jnp.exp2(s_buf[0:na, 0:nq] - m).astype(BF16)

    def values_win(p_buf, slot):
        heads = []
        for g in range(2):
            vt = jnp.concatenate([vat_ref[0, rows(g), pl.ds(start, span)], vatc_ref[0, rows(g), :]], axis=1)
            acc = _dot(with_ones(vt), p_buf[0:na, cols(g)])
            o = acc[0:HEAD_DIM] / (acc[HEAD_DIM:HEAD_DIM + 1] + alpha_sc[slot][:, cols(g)])
            heads += [o[:, j * tq:(j + 1) * tq] for j in range(GROUP)]
        _store_heads(oa_ref, heads)

    def key_block(blk):
        if isinstance(blk, int) and blk == nk - 1:
            return jnp.concatenate([kb_ref[0, length - tail:, :], kbc_ref[0]], axis=0)
        return kb_ref[0, pl.ds(pl.multiple_of(blk * tk, tk), tk), :]

    def value_block(blk, g):
        if isinstance(blk, int) and blk == nk - 1:
            return jnp.concatenate([vbt_ref[0, rows(g), length - tail:], vbtc_ref[0, rows(g), :]], axis=1)
        return vbt_ref[0, rows(g), pl.ds(pl.multiple_of(blk * tk, tk), tk)]

    def scores(blk, s_buf, slot):
        s = _nt(key_block(blk), qb)
        s_buf[0:tk, 0:nq] = s
        bmax_sc[slot] = jnp.max(s, axis=0, keepdims=True)

    def softmax(s_buf, p_buf, slot):
        m_old = m_sc[...]
        m_new = jnp.maximum(m_old, bmax_sc[slot])
        m_sc[...] = m_new
        alpha_sc[slot] = jnp.exp2(m_old - m_new)
        p_buf[0:tk, 0:nq] = jnp.exp2(s_buf[0:tk, 0:nq] - m_new).astype(BF16)

    def values(blk, p_buf, slot):
        for g in range(2):
            vt = with_ones(value_block(blk, g))
            acc_sc[g] = acc_sc[g] * alpha_sc[slot][:, cols(g)] + _dot(vt, p_buf[0:tk, cols(g)])

    bufs = ((s_a, p_a), (s_b, p_b))
    n_steps = nk + 1

    def step(t, par, do_scores=True):
        (s_cur, p_cur), (s_oth, p_oth) = bufs[par], bufs[1 - par]
        if do_scores:
            scores(t, s_oth, 1 - par)
        softmax(s_cur, p_cur, par)
        values(t - 2, p_oth, 1 - par)

    scores_win(s_a, 0)
    scores(0, s_b, 1)
    softmax_win(s_a, p_a, 0)
    if nk >= 2:
        scores(1, s_a, 0)
    softmax(s_b, p_b, 1)
    values_win(p_a, 0)
    n_pairs = max(0, (nk - 3) // 2)

    def pair(i, carry):
        t = 2 + 2 * i
        step(t, 0)
        step(t + 1, 1)
        return carry

    lax.fori_loop(0, n_pairs, pair, 0)
    for t in range(2 + 2 * n_pairs, n_steps):
        step(t, t % 2, do_scores=t + 1 < n_steps)
    last = (n_steps - 1) % 2
    values(nk - 1, bufs[last][1], last)

    _store_heads(ob_ref, [acc_sc[h // GROUP, 0:HEAD_DIM, (h % GROUP) * tq:(h % GROUP + 1) * tq]
                          / acc_sc[h // GROUP, HEAD_DIM:HEAD_DIM + 1, (h % GROUP) * tq:(h % GROUP + 1) * tq]
                          for h in range(N_HEADS)])


def _key_tile(n_keys, cap=1024):
    return max(t for t in range(LANE, cap + 1, LANE) if n_keys % t == 0)


def _attn_latent(qa, ka, vat, qb, kb, vbt, ka_c, vat_c, kb_c, vbt_c, sink_row, *, tq, tk):
    b, _, l, _ = qb.shape
    lc = ka_c.shape[1] // b
    assert (l + lc) % tk == 0 and (l + lc) // tk * tk - tk <= l and tk > lc
    buf_rows = max(tk, tq + 2 * WINDOW + lc)
    buf_cols = N_HEADS * tq
    q_spec = pl.BlockSpec((1, N_HEADS, tq, LANE), lambda b, i: (b, 0, i, 0))
    k_spec = pl.BlockSpec((1, l, LANE), lambda b, i: (b, 0, 0), pipeline_mode=pl.Buffered(1))
    vt_spec = pl.BlockSpec((1, LANE, l), lambda b, i: (b, 0, 0), pipeline_mode=pl.Buffered(1))
    kc_spec = pl.BlockSpec((1, lc, LANE), lambda b, i: (0, b, 0), pipeline_mode=pl.Buffered(1))
    vtc_spec = pl.BlockSpec((1, LANE, lc), lambda b, i: (0, 0, b), pipeline_mode=pl.Buffered(1))
    o_spec = pl.BlockSpec((1, tq, N_HEADS * HEAD_DIM), lambda b, i: (b, i, 0))
    o_shape = jax.ShapeDtypeStruct((b, l, N_HEADS * HEAD_DIM), BF16)
    return pl.pallas_call(
        functools.partial(_attn_latent_kernel, tk=tk),
        grid=(b, l // tq),
        in_specs=[q_spec, q_spec, k_spec, vt_spec, k_spec, vt_spec, kc_spec, vtc_spec, kc_spec, vtc_spec,
                  pl.BlockSpec((1, N_HEADS * tq), lambda b, i: (0, 0))],
        out_specs=[o_spec, o_spec],
        out_shape=[o_shape, o_shape],
        scratch_shapes=[
            pltpu.VMEM((buf_rows, buf_cols), F32),
            pltpu.VMEM((buf_rows, buf_cols), F32),
            pltpu.VMEM((buf_rows, buf_cols), BF16),
            pltpu.VMEM((buf_rows, buf_cols), BF16),
            pltpu.VMEM((1, N_HEADS * tq), F32),
            pltpu.VMEM((2, 1, N_HEADS * tq), F32),
            pltpu.VMEM((2, 1, N_HEADS * tq), F32),
            pltpu.VMEM((2, HEAD_DIM + DEN_ROWS, GROUP * tq), F32),
        ],
        compiler_params=_params("parallel", "parallel"),
        name="attn_latent",
    )(qa, qb, ka, vat, kb, vbt, ka_c, vat_c, kb_c, vbt_c, sink_row)


def _softmax_pv(q_ref, ks, vts, extra, mask=None):
    tq = q_ref.shape[2]
    aug = [[jnp.concatenate([vt[g * HEAD_DIM:(g + 1) * HEAD_DIM, :],
                             jnp.ones((DEN_ROWS, vt.shape[1]), BF16)], axis=0) for vt in vts]
           for g in range(2)]
    out = []
    for h in range(N_HEADS):
        ss = [_nt(k, q_ref[0, h]) for k in ks]
        if mask is not None:
            ss[0] = jnp.where(mask, ss[0], NEG_INF)
        m = functools.reduce(jnp.maximum, [jnp.max(s, axis=0, keepdims=True) for s in ss])
        if extra is not None:
            e = extra[:, h * tq:(h + 1) * tq]
            m = jnp.maximum(m, e)
        acc = functools.reduce(
            jnp.add, [_dot(vt, jnp.exp2(s - m).astype(BF16)) for s, vt in zip(ss, aug[h // GROUP])])
        den = acc[HEAD_DIM:HEAD_DIM + 1]
        if extra is not None:
            den = den + jnp.exp2(e - m)
        out.append(acc[0:HEAD_DIM] / den)
    return out


def _attn_ctx_kernel(q_ref, k_ref, vt_ref, *rest, has_sink):
    if has_sink:
        sink_ref, o_ref = rest
    else:
        (o_ref,) = rest
    extra = sink_ref[...] if has_sink else None
    _store_heads(o_ref, _softmax_pv(q_ref, [k_ref[0]], [vt_ref[0]], extra))


def _attn_ctx(q, k, vt, sink_row, *, batch):
    lc = q.shape[2] // batch
    in_specs = [
        pl.BlockSpec((1, N_HEADS, lc, LANE), lambda b: (0, 0, b, 0)),
        pl.BlockSpec((1, lc, LANE), lambda b: (0, b, 0)),
        pl.BlockSpec((1, LANE, lc), lambda b: (0, 0, b)),
    ]
    args = [q, k, vt]
    if sink_row is not None:
        in_specs.append(pl.BlockSpec((1, N_HEADS * lc), lambda b: (0, 0)))
        args.append(sink_row)
    return pl.pallas_call(
        functools.partial(_attn_ctx_kernel, has_sink=sink_row is not None),
        grid=(batch,),
        in_specs=in_specs,
        out_specs=pl.BlockSpec((1, lc, N_HEADS * HEAD_DIM), lambda b: (0, b, 0)),
        out_shape=jax.ShapeDtypeStruct((1, batch * lc, N_HEADS * HEAD_DIM), BF16),
        compiler_params=_params("parallel"),
        name="attn_ctx",
    )(*args)


def _stack3_lhs(hi, lo):
    return jnp.concatenate([hi, hi, lo], axis=1)


def _stack3_rhs(x):
    hi, lo = _split3(x)
    return jnp.concatenate([hi, lo, hi], axis=0)


def _fft1_kernel(x_ref, t_ref, twc_ref, tws_ref, o_ref):
    tn2 = x_ref.shape[2]
    n1 = x_ref.shape[1]
    xt = jnp.transpose(x_ref[0], (1, 0, 2))
    y = _dot(t_ref[...], _stack3_rhs(jnp.concatenate([xt[i] for i in range(tn2)], axis=1)))
    cx, sx = y[0:n1], y[n1:]
    w = 2 * C_WIDTH
    outs = []
    for i in range(tn2):
        tc = jnp.concatenate([twc_ref[i]] * (C_WIDTH // LANE), axis=1)
        ts = jnp.concatenate([tws_ref[i]] * (C_WIDTH // LANE), axis=1)
        re, im = slice(i * w, i * w + C_WIDTH), slice(i * w + C_WIDTH, (i + 1) * w)
        ar = cx[:, re] + sx[:, im]
        ai = cx[:, im] - sx[:, re]
        outs.append(jnp.concatenate([ar * tc + ai * ts, ai * tc - ar * ts], axis=1))
    o_ref[0] = jnp.transpose(jnp.stack(outs, axis=0), (1, 0, 2))


def _real_dft(t_ref, x):
    rhs = jnp.concatenate([_stack3_rhs(x[:, 0:C_WIDTH]), _stack3_rhs(x[:, C_WIDTH:])], axis=0)
    return _dot(t_ref[...], rhs)


def _fft2_kernel(x_ref, t_ref, o_ref):
    outs = [_real_dft(t_ref, x_ref[0, i]) for i in range(x_ref.shape[1])]
    o_ref[0] = jnp.transpose(jnp.stack(outs, axis=0), (1, 0, 2))


def _dft_tables(n, norm):
    k = np.arange(n, dtype=np.int64)
    ang = 2.0 * np.pi * ((k[:, None] * k[None, :]) % n).astype(np.float64) / n
    c = jnp.asarray(np.cos(ang) * norm, F32)
    s = jnp.asarray(np.sin(ang) * norm, F32)
    return _stack3_lhs(*_split3(c)), _stack3_lhs(*_split3(s))


def _fourier_latent(z):
    b, l, w = z.shape
    n1, n2 = FFT_N1, l // FFT_N1
    tn2, tk1 = 8, 8
    prod = jnp.arange(n2)[:, None] * jnp.arange(n1)[None, :]
    ang = prod.astype(F32) * (2.0 * np.pi / l)
    twc = jnp.broadcast_to(jnp.cos(ang)[:, :, None], (n2, n1, LANE))
    tws = jnp.broadcast_to(jnp.sin(ang)[:, :, None], (n2, n1, LANE))
    t1 = jnp.concatenate(_dft_tables(n1, 1.0), axis=0)
    y = pl.pallas_call(
        _fft1_kernel,
        grid=(b, n2 // tn2),
        in_specs=[pl.BlockSpec((1, n1, tn2, w), lambda b, j: (b, 0, j, 0))]
        + [_resident(t1.shape)]
        + [pl.BlockSpec((tn2, n1, LANE), lambda b, j: (j, 0, 0))] * 2,
        out_specs=pl.BlockSpec((1, n1, tn2, w), lambda b, j: (b, 0, j, 0)),
        out_shape=jax.ShapeDtypeStruct((b, n1, n2, w), F32),
        compiler_params=_params("parallel", "parallel"),
        name="fft_stage1",
    )(z.reshape(b, n1, n2, w), t1, twc, tws)
    t2m = jnp.concatenate(_dft_tables(n2, 1.0 / np.sqrt(l)), axis=1)
    out = pl.pallas_call(
        _fft2_kernel,
        grid=(b, n1 // tk1),
        in_specs=[pl.BlockSpec((1, tk1, n2, w), lambda b, j: (b, j, 0, 0)), _resident(t2m.shape)],
        out_specs=pl.BlockSpec((1, n2, tk1, C_WIDTH), lambda b, j: (b, 0, j, 0)),
        out_shape=jax.ShapeDtypeStruct((b, n2, n1, C_WIDTH), F32),
        compiler_params=_params("parallel", "parallel"),
        name="fft_stage2",
    )(y, t2m)
    return out.reshape(b, l, C_WIDTH)


def _fourier_ctx_kernel(x_ref, t_ref, o_ref):
    o_ref[0] = _real_dft(t_ref, x_ref[0])


def _fourier_ctx(z, *, batch):
    lc = z.shape[1] // batch
    tabs = jnp.concatenate(_dft_tables(lc, 1.0 / np.sqrt(lc)), axis=1)
    return pl.pallas_call(
        _fourier_ctx_kernel,
        grid=(batch,),
        in_specs=[pl.BlockSpec((1, lc, 2 * C_WIDTH), lambda b: (0, b, 0)), _resident(tabs.shape)],
        out_specs=pl.BlockSpec((1, lc, C_WIDTH), lambda b: (0, b, 0)),
        out_shape=jax.ShapeDtypeStruct((1, batch * lc, C_WIDTH), F32),
        compiler_params=_params("parallel"),
        name="fourier_ctx",
    )(z, tabs)


def _rope_tables(length):
    t = jnp.arange(length)
    row = (t // GRID_W).astype(F32)
    col = (t % GRID_W).astype(F32)
    half = HEAD_DIM // 2
    inv = ROPE_THETA ** (-jnp.arange(0, half, 2, dtype=F32) / half)
    ar, ac = row[:, None] * inv, col[:, None] * inv
    cos = jnp.concatenate([jnp.cos(ar), jnp.cos(ar), jnp.cos(ac), jnp.cos(ac)], axis=-1)
    sin = jnp.concatenate([-jnp.sin(ar), jnp.sin(ar), -jnp.sin(ac), jnp.sin(ac)], axis=-1)
    return jnp.tile(cos, (1, 2)), jnp.tile(sin, (1, 2))


def _channel_dft():
    k = np.arange(C_GROUP_DIM)
    ang = 2.0 * np.pi * ((k[:, None] * k[None, :]) % C_GROUP_DIM) / C_GROUP_DIM
    eye = np.eye(C_GROUPS)
    norm = 1.0 / np.sqrt(C_GROUP_DIM)
    cs = np.concatenate([np.kron(eye, np.cos(ang) * norm), np.kron(eye, -np.sin(ang) * norm)], axis=1)
    return _split3(jnp.asarray(cs, F32))


def _block_diag(w):
    g, c, _ = w.shape
    eye = jnp.eye(g, dtype=w.dtype)
    return (eye[:, None, :, None] * w[:, :, None, :]).reshape(g * c, g * c)


def kernel(x, c, ctx, c_ctx, w_ada, b_ada, g_ffn1, g_mix, g_ffn2, w_in, g_qn, g_kn, sink, w_four,
           w_out, w1_gate, w1_up, w1_down, w2_gate, w2_up, w2_down, g_final):
    batch, length, d = x.shape
    lc = ctx.shape[1]
    depth = w_ada.shape[0]
    assert length % (FFT_N1 * 8) == 0 and length >= BLOCK + 2 * WINDOW and batch + 1 <= 8
    tm = 512
    tm_ctx = batch * lc

    cvec = jnp.zeros((8, d), F32).at[:batch].set(c).at[batch].set(c_ctx)
    mod = _ada(cvec, w_ada, b_ada).reshape(depth, 8, N_MOD, d)

    cos, sin = _rope_tables(length)
    ones = jnp.ones((tm_ctx, LANE), F32)
    msm = _block_diag(jnp.full((2, HEAD_DIM, HEAD_DIM), 1.0 / HEAD_DIM, F32)).astype(BF16)
    wc = _wfold(*_channel_dft(), jax.vmap(_block_diag)(w_four))

    w1 = tuple(w.astype(BF16) for w in (w1_gate, w1_up, w1_down))
    w2 = tuple(w.astype(BF16) for w in (w2_gate, w2_up, w2_down))
    win, wout = w_in.astype(BF16), w_out.astype(BF16)

    h = x
    hc = ctx.reshape(1, batch * lc, d)
    for l in range(depth):
        last = l == depth - 1
        mod_l, mod_c = mod[l, :batch], mod[l, batch:batch + 1]
        gq = jnp.tile(g_qn[l].reshape(1, HEAD_DIM), (1, 2))
        gk = jnp.tile(g_kn[l].reshape(1, HEAD_DIM), (1, 2))
        proj = dict(g=g_mix[l], w_in=win, gq=gq, gk=gk, msm=msm, wc=wc)

        h, qa, ka, vat, qb, kb, vbt, z = _ffn(
            h, mod_l, g_ffn1[l], *w1, layer=l, row0=0, tm=tm, proj=dict(proj, cos=cos, sin=sin, rope=True))
        hc, qa_c, ka_c, vat_c, qb_c, kb_c, vbt_c, z_c = _ffn(
            hc, mod_c, g_ffn1[l], *w1, layer=l, row0=0, tm=tm_ctx, proj=dict(proj, cos=ones, sin=ones, rope=False))

        tq = 512
        sink_l = jnp.repeat(sink[l].astype(F32) * LOG2E, tq).reshape(1, N_HEADS * tq)
        oa, ob = _attn_latent(qa, ka, vat, qb, kb, vbt, ka_c, vat_c, kb_c, vbt_c, sink_l,
                              tq=tq, tk=_key_tile(length + lc))
        oc = _fourier_latent(z)
        h = _ffn(h, mod_l, g_ffn2[l], *w2, layer=l, row0=6, tm=tm, mix=(oa, ob, oc, wout),
                 final_g=g_final if last else None)

        if not last:
            sink_c = jnp.repeat(sink[l].astype(F32) * LOG2E, lc).reshape(1, N_HEADS * lc)
            oa_c = _attn_ctx(qa_c, ka_c, vat_c, sink_c, batch=batch)
            ob_c = _attn_ctx(qb_c, kb_c, vbt_c, None, batch=batch)
            oc_c = _fourier_ctx(z_c, batch=batch)
            hc = _ffn(hc, mod_c, g_ffn2[l], *w2, layer=l, row0=6, tm=tm_ctx, mix=(oa_c, ob_c, oc_c, wout))
    return h
```

```python
import functools

import jax
import jax.numpy as jnp
import numpy as np
from jax import lax
from jax.experimental import pallas as pl
from jax.experimental.pallas import tpu as pltpu

HEAD_DIM = 64
A_HEADS = 6
A_KV_HEADS = 2
B_HEADS = 6
B_KV_HEADS = 2
C_GROUPS = 4
C_GROUP_DIM = 64
GRID_W = 64
WINDOW = 128
BLOCK = 128
ROPE_THETA = 10000.0
NORM_EPS = 1e-6
N_MOD = 9
NEG_INF = -1e30

N_HEADS = 6
GROUP = 3
C_WIDTH = C_GROUPS * C_GROUP_DIM
FFT_N1 = 64
WIN_TILE = 256
DEN_ROWS = 16
LOG2E = 1.4426950408889634

LANE = 128
VMEM_LIMIT = 60 * 1024 * 1024

F32 = jnp.float32
BF16 = jnp.bfloat16


def _params(*sem, flags=None):
    return pltpu.CompilerParams(dimension_semantics=tuple(sem), vmem_limit_bytes=VMEM_LIMIT, flags=flags)


def _resident(shape):
    nd = len(shape)
    return pl.BlockSpec(shape, lambda *_: (0,) * nd, pipeline_mode=pl.Buffered(1))


def _layer_resident(stacked, layer):
    nd = stacked.ndim
    return pl.BlockSpec((None,) + stacked.shape[1:], lambda *_: (layer,) + (0,) * (nd - 1),
                        pipeline_mode=pl.Buffered(1))


def _split3(x):
    hi = x.astype(BF16)
    lo = (x - hi.astype(F32)).astype(BF16)
    return hi, lo


def _dot(a, b):
    return jnp.dot(a, b, preferred_element_type=F32)


def _dot3(a_hi, a_lo, b_hi, b_lo):
    return _dot(a_hi, b_hi) + _dot(a_hi, b_lo) + _dot(a_lo, b_hi)


def _silu(x):
    return x / (1.0 + jnp.exp(-x))


def _norm_mod(x, g, shift, scale):
    ms = jnp.mean(x * x, axis=-1, keepdims=True)
    return (x * lax.rsqrt(ms + NORM_EPS)) * (g * (1.0 + scale)) + shift


def _ada_kernel(c_ref, w_ref, b_ref, o_ref):
    s_hi, s_lo = _split3(_silu(c_ref[...]))
    w_hi, w_lo = _split3(w_ref[0])
    o_ref[0] = _dot3(s_hi, s_lo, w_hi, w_lo) + b_ref[0]


def _ada(cvec, w_ada, b_ada):
    depth, d, nd = w_ada.shape
    tn = 1024
    return pl.pallas_call(
        _ada_kernel,
        grid=(depth, nd // tn),
        in_specs=[
            pl.BlockSpec((8, d), lambda l, j: (0, 0)),
            pl.BlockSpec((1, d, tn), lambda l, j: (l, 0, j)),
            pl.BlockSpec((1, 1, tn), lambda l, j: (l, 0, j)),
        ],
        out_specs=pl.BlockSpec((1, 8, tn), lambda l, j: (l, 0, j)),
        out_shape=jax.ShapeDtypeStruct((depth, 8, nd), F32),
        compiler_params=_params("parallel", "parallel"),
        name="ada",
    )(cvec, w_ada, b_ada.reshape(depth, 1, nd))


def _mix_out(oa_ref, ob_ref, oc_ref, w_ref):
    na = N_HEADS * HEAD_DIM
    return (_dot(oa_ref[0], w_ref[0:na, :]) + _dot(ob_ref[0], w_ref[na:2 * na, :])
            + _dot(oc_ref[0].astype(BF16), w_ref[2 * na:, :]))


N_PROJ_IN = 8


def _ffn_kernel(*refs, row0, final, mixed, rope):
    refs = list(refs)
    x_ref, mod_ref, g_ref, wg_ref, wu_ref, wd_ref = refs[:6]
    del refs[:6]
    x = x_ref[0]
    if mixed:
        oa_ref, ob_ref, oc_ref, wo_ref = refs[:4]
        del refs[:4]
        x = x + mod_ref[0, 5:6, :] * _mix_out(oa_ref, ob_ref, oc_ref, wo_ref)
    if final:
        gf_ref = refs.pop(0)
    proj_in = refs[:N_PROJ_IN] if rope is not None else []
    o_ref = refs[len(proj_in)]
    proj_out = refs[len(proj_in) + 1:]
    shift = mod_ref[0, row0:row0 + 1, :]
    scale = mod_ref[0, row0 + 1:row0 + 2, :]
    gate = mod_ref[0, row0 + 2:row0 + 3, :]
    xm = _norm_mod(x, g_ref[...], shift, scale).astype(BF16)
    a = _dot(xm, wg_ref[...])
    u = _dot(xm, wu_ref[...])
    act = (_silu(a) * u).astype(BF16)
    y = x + (0.5 * gate) * _dot(act, wd_ref[...])
    if final:
        ms = jnp.mean(y * y, axis=-1, keepdims=True)
        y = (y * lax.rsqrt(ms + NORM_EPS)) * gf_ref[...]
    o_ref[0] = y
    if rope is not None:
        _proj_body(y, mod_ref, *proj_in, *proj_out, rope=rope)


def _ffn(x, mod, g, wg, wu, wd, *, layer, row0, tm, mix=None, final_g=None, proj=None):
    groups, t, d = x.shape
    final = final_g is not None
    tok = lambda b, i: (b, i, 0)
    in_specs = [
        pl.BlockSpec((1, tm, d), tok),
        pl.BlockSpec((1, N_MOD, d), lambda b, i: (b, 0, 0)),
        _resident((1, d)),
        _layer_resident(wg, layer),
        _layer_resident(wu, layer),
        _layer_resident(wd, layer),
    ]
    args = [x, mod, g.reshape(1, d), wg, wu, wd]
    if mix is not None:
        oa, ob, oc, w_out = mix
        in_specs += [pl.BlockSpec((1, tm, oa.shape[2]), tok), pl.BlockSpec((1, tm, ob.shape[2]), tok),
                     pl.BlockSpec((1, tm, oc.shape[2]), tok), _layer_resident(w_out, layer)]
        args += [oa, ob, oc, w_out]
    if final:
        in_specs.append(_resident((1, d)))
        args.append(final_g.reshape(1, d))
    out_specs = [pl.BlockSpec((1, tm, d), tok)]
    out_shape = [jax.ShapeDtypeStruct((groups, t, d), F32)]
    if proj is not None:
        in_specs += [
            _resident((1, d)),
            _layer_resident(proj["w_in"], layer),
            pl.BlockSpec((tm, LANE), lambda b, i: (i, 0)),
            pl.BlockSpec((tm, LANE), lambda b, i: (i, 0)),
            _resident((1, LANE)),
            _resident((1, LANE)),
            _resident((LANE, LANE)),
            _layer_resident(proj["wc"], layer),
        ]
        args += [proj["g"].reshape(1, d), proj["w_in"], proj["cos"], proj["sin"], proj["gq"], proj["gk"],
                 proj["msm"], proj["wc"]]
        q_spec = pl.BlockSpec((1, N_HEADS, tm, LANE), lambda b, i: (b, 0, i, 0))
        k_spec = pl.BlockSpec((1, tm, LANE), tok)
        vt_spec = pl.BlockSpec((1, LANE, tm), lambda b, i: (b, 0, i))
        q_shape = jax.ShapeDtypeStruct((groups, N_HEADS, t, LANE), BF16)
        k_shape = jax.ShapeDtypeStruct((groups, t, LANE), BF16)
        vt_shape = jax.ShapeDtypeStruct((groups, LANE, t), BF16)
        out_specs += [q_spec, k_spec, vt_spec, q_spec, k_spec, vt_spec, pl.BlockSpec((1, tm, 2 * C_WIDTH), tok)]
        out_shape += [q_shape, k_shape, vt_shape, q_shape, k_shape, vt_shape,
                      jax.ShapeDtypeStruct((groups, t, 2 * C_WIDTH), F32)]
    out = pl.pallas_call(
        functools.partial(_ffn_kernel, row0=row0, final=final, mixed=mix is not None,
                          rope=None if proj is None else proj["rope"]),
        grid=(groups, t // tm),
        in_specs=in_specs,
        out_specs=out_specs,
        out_shape=out_shape,
        compiler_params=_params("parallel", "parallel"),
        name="ffn",
    )(*args)
    return out[0] if proj is None else out


def _rope_slab(t, cos, sin, lane):
    fwd = pltpu.roll(t, LANE - 16, axis=1)
    bwd = pltpu.roll(t, 16, axis=1)
    partner = jnp.where((lane % 32) < 16, fwd, bwd)
    return t * cos + partner * sin


def _wfold_kernel(cs_hi_ref, cs_lo_ref, w_ref, o_ref):
    w_hi, w_lo = _split3(w_ref[0])
    for half in range(2):
        sl = slice(half * C_WIDTH, (half + 1) * C_WIDTH)
        o_ref[0, :, sl] = _dot3(cs_hi_ref[:, sl], cs_lo_ref[:, sl], w_hi, w_lo).astype(BF16)


def _wfold(cs_hi, cs_lo, wbd):
    depth = wbd.shape[0]
    return pl.pallas_call(
        _wfold_kernel,
        grid=(depth,),
        in_specs=[_resident(cs_hi.shape), _resident(cs_lo.shape),
                  pl.BlockSpec((1, C_WIDTH, C_WIDTH), lambda l: (l, 0, 0))],
        out_specs=pl.BlockSpec((1, C_WIDTH, 2 * C_WIDTH), lambda l: (l, 0, 0)),
        out_shape=jax.ShapeDtypeStruct((depth, C_WIDTH, 2 * C_WIDTH), BF16),
        compiler_params=_params("parallel"),
        name="wfold",
    )(cs_hi, cs_lo, wbd)


def _proj_body(x, mod_ref, g_ref, w_ref, cos_ref, sin_ref, gq_ref, gk_ref, msm_ref, wc_ref,
               qa_ref, ka_ref, vat_ref, qb_ref, kb_ref, vbt_ref, z_ref, *, rope):
    tm = x.shape[0]
    shift = mod_ref[0, 3:4, :]
    scale = mod_ref[0, 4:5, :]
    xm = _norm_mod(x, g_ref[...], shift, scale).astype(BF16)

    u_all = _dot(xm, w_ref[...])

    def u(lo, hi):
        return u_all[:, lo:hi]

    lane = lax.broadcasted_iota(jnp.int32, (tm, LANE), 1)
    low = lane < HEAD_DIM
    if rope:
        cos = cos_ref[...]
        sin = sin_ref[...]

    def qk_norm(t, gain):
        ms = _dot((t * t).astype(BF16), msm_ref[...])
        return t * lax.rsqrt(ms + NORM_EPS) * gain

    def finish(t):
        return _rope_slab(t, cos, sin, lane) if rope else t

    def emit_q(q_ref, slabs):
        for h in range(N_HEADS):
            t = slabs[h // 2] * (LOG2E * HEAD_DIM ** -0.5)
            if (h % 2) != (h // GROUP):
                t = pltpu.roll(t, HEAD_DIM, axis=1)
            keep = low if (h // GROUP) == 0 else jnp.logical_not(low)
            q_ref[0, h] = jnp.where(keep, t, 0.0).astype(BF16)

    qa = [finish(u(LANE * i, LANE * (i + 1))) for i in range(3)]
    emit_q(qa_ref, qa)
    ka_ref[0] = finish(u(384, 512)).astype(BF16)
    vat_ref[0] = u(512, 640).T.astype(BF16)

    qb = [finish(qk_norm(u(640 + LANE * i, 640 + LANE * (i + 1)), gq_ref[...])) for i in range(3)]
    emit_q(qb_ref, qb)
    kb_ref[0] = finish(qk_norm(u(1024, 1152), gk_ref[...])).astype(BF16)
    vbt_ref[0] = u(1152, 1280).T.astype(BF16)

    z_ref[0] = _dot(u(1280, 1536).astype(BF16), wc_ref[...])


def _nt(a, b):
    return lax.dot_general(a, b, (((1,), (1,)), ((), ())), preferred_element_type=F32)


def _store_heads(o_ref, ot):
    for pair in range(N_HEADS // 2):
        slab = jnp.concatenate([ot[2 * pair], ot[2 * pair + 1]], axis=0)
        o_ref[0, :, pair * LANE:(pair + 1) * LANE] = slab.T.astype(o_ref.dtype)


def _attn_latent_kernel(qa_ref, qb_ref, ka_ref, vat_ref, kb_ref, vbt_ref, kac_ref, vatc_ref, kbc_ref, vbtc_ref,
                        sink_ref, oa_ref, ob_ref, s_a, s_b, p_a, p_b, m_sc, bmax_sc, alpha_sc, acc_sc, *, tk):
    tq = qb_ref.shape[2]
    length = kb_ref.shape[1]
    lc = kbc_ref.shape[1]
    nk = (length + lc) // tk
    tail = length - (nk - 1) * tk
    wq = min(tq, WIN_TILE)
    span = wq + 2 * WINDOW
    na = span + lc
    gw = GROUP * tq
    tile = pl.program_id(1)
    starts = [pl.multiple_of(jnp.clip(tile * tq + i * wq - WINDOW, 0, length - span), LANE)
              for i in range(tq // wq)]
    m_sc[...] = jnp.full_like(m_sc, NEG_INF)
    acc_sc[...] = jnp.zeros_like(acc_sc)

    def with_ones(vt):
        return jnp.concatenate([vt, jnp.ones((DEN_ROWS, vt.shape[1]), BF16)], axis=0)

    def rows(g):
        return slice(g * HEAD_DIM, (g + 1) * HEAD_DIM)

    def cols(g):
        return slice(g * gw, (g + 1) * gw)

    def sub_cols(h, i):
        return slice(h * tq + i * wq, h * tq + (i + 1) * wq)

    def scores_win(s_buf, slot):
        for i, start in enumerate(starts):
            kw = ka_ref[0, pl.ds(start, span), :]
            kpos = start + lax.broadcasted_iota(jnp.int32, (span, wq), 0)
            qpos = tile * tq + i * wq + lax.broadcasted_iota(jnp.int32, (span, wq), 1)
            valid = jnp.abs(kpos - qpos) <= WINDOW
            for h in range(N_HEADS):
                cs = sub_cols(h, i)
                q = qa_ref[0, h, i * wq:(i + 1) * wq, :]
                s_w = jnp.where(valid, _nt(kw, q), NEG_INF)
                s_c = _nt(kac_ref[0], q)
                s_buf[0:span, cs] = s_w
                s_buf[span:na, cs] = s_c
                bmax_sc[slot, :, cs] = jnp.maximum(jnp.max(s_w, axis=0, keepdims=True),
                                                   jnp.max(s_c, axis=0, keepdims=True))

    def softmax_win(s_buf, p_buf, slot):
        for h in range(N_HEADS):
            hc = head_cols(h)
            sink = sink_ref[:, hc]
            m = jnp.maximum(sink, bmax_sc[slot, :, hc])
            alpha_sc[slot, :, hc] = jnp.exp2(sink - m)
            p_buf[0:na, hc] = jnp.exp2(s_buf[0:na, hc] - m).astype(BF16)

    def values_win(p_buf, slot):
        parts = [[None] * len(starts) for _ in range(N_HEADS)]
        for g in range(2):
            for i, start in enumerate(starts):
                vt = with_ones(jnp.concatenate(
                    [vat_ref[0, rows(g), pl.ds(start, span)], vatc_ref[0, rows(g), :]], axis=1))
                for h in range(g * GROUP, (g + 1) * GROUP):
                    cs = sub_cols(h, i)
                    acc = _dot(vt, p_buf[0:na, cs])
                    parts[h][i] = acc[0:HEAD_DIM] / (acc[HEAD_DIM:HEAD_DIM + 1] + alpha_sc[slot, :, cs])
        _store_heads(oa_ref, [jnp.concatenate(p, axis=1) for p in parts])

    def key_block(blk):
        if isinstance(blk, int) and blk == nk - 1:
            return jnp.concatenate([kb_ref[0, length - tail:, :], kbc_ref[0]], axis=0)
        return kb_ref[0, pl.ds(pl.multiple_of(blk * tk, tk), tk), :]

    def value_block(blk, g):
        if isinstance(blk, int) and blk == nk - 1:
            return jnp.concatenate([vbt_ref[0, rows(g), length - tail:], vbtc_ref[0, rows(g), :]], axis=1)
        return vbt_ref[0, rows(g), pl.ds(pl.multiple_of(blk * tk, tk), tk)]

    def head_cols(h):
        return slice(h * tq, (h + 1) * tq)

    def scores(blk, s_buf, slot):
        kb = key_block(blk)
        for h in range(N_HEADS):
            s = _nt(kb, qb_ref[0, h])
            s_buf[0:tk, head_cols(h)] = s
            bmax_sc[slot, :, head_cols(h)] = jnp.max(s, axis=0, keepdims=True)

    def softmax(s_buf, p_buf, slot):
        for h in range(N_HEADS):
            hc = head_cols(h)
            m_old = m_sc[:, hc]
            m_new = jnp.maximum(m_old, bmax_sc[slot, :, hc])
            m_sc[:, hc] = m_new
            alpha_sc[slot, :, hc] = jnp.exp2(m_old - m_new)
            p_buf[0:tk, hc] = jnp.exp2(s_buf[0:tk, hc] - m_new).astype(BF16)

    def values(blk, p_buf, slot):
        for g in range(2):
            vt = with_ones(value_block(blk, g))
            acc_sc[g] = acc_sc[g] * alpha_sc[slot][:, cols(g)] + _dot(vt, p_buf[0:tk, cols(g)])

    bufs = ((s_a, p_a), (s_b, p_b))
    n_steps = nk + 1

    def step(t, par, do_scores=True):
        (s_cur, p_cur), (s_oth, p_oth) = bufs[par], bufs[1 - par]
        if do_scores:
            scores(t, s_oth, 1 - par)
        softmax(s_cur, p_cur, par)
        values(t - 2, p_oth, 1 - par)

    scores_win(s_a, 0)
    scores(0, s_b, 1)
    softmax_win(s_a, p_a, 0)
    if nk >= 2:
        scores(1, s_a, 0)
    softmax(s_b, p_b, 1)
    values_win(p_a, 0)
    n_pairs = max(0, (nk - 3) // 2)

    def pair(i, carry):
        t = 2 + 2 * i
        step(t, 0)
        step(t + 1, 1)
        return carry

    lax.fori_loop(0, n_pairs, pair, 0)
    for t in range(2 + 2 * n_pairs, n_steps):
        step(t, t % 2, do_scores=t + 1 < n_steps)
    last = (n_steps - 1) % 2
    values(nk - 1, bufs[last][1], last)

    _store_heads(ob_ref, [acc_sc[h // GROUP, 0:HEAD_DIM, (h % GROUP) * tq:(h % GROUP + 1) * tq]
                          / acc_sc[h // GROUP, HEAD_DIM:HEAD_DIM + 1, (h % GROUP) * tq:(h % GROUP + 1) * tq]
                          for h in range(N_HEADS)])


def _key_tile(n_keys, cap=1024):
    return max(t for t in range(LANE, cap + 1, LANE) if n_keys % t == 0)


def _attn_latent(qa, ka, vat, qb, kb, vbt, ka_c, vat_c, kb_c, vbt_c, sink_row, *, tq, tk):
    b, _, l, _ = qb.shape
    lc = ka_c.shape[1] // b
    assert (l + lc) % tk == 0 and (l + lc) // tk * tk - tk <= l and tk > lc
    buf_rows = max(tk, min(tq, WIN_TILE) + 2 * WINDOW + lc)
    buf_cols = N_HEADS * tq
    q_spec = pl.BlockSpec((1, N_HEADS, tq, LANE), lambda b, i: (b, 0, i, 0))
    k_spec = pl.BlockSpec((1, l, LANE), lambda b, i: (b, 0, 0), pipeline_mode=pl.Buffered(1))
    vt_spec = pl.BlockSpec((1, LANE, l), lambda b, i: (b, 0, 0), pipeline_mode=pl.Buffered(1))
    kc_spec = pl.BlockSpec((1, lc, LANE), lambda b, i: (0, b, 0), pipeline_mode=pl.Buffered(1))
    vtc_spec = pl.BlockSpec((1, LANE, lc), lambda b, i: (0, 0, b), pipeline_mode=pl.Buffered(1))
    o_spec = pl.BlockSpec((1, tq, N_HEADS * HEAD_DIM), lambda b, i: (b, i, 0))
    o_shape = jax.ShapeDtypeStruct((b, l, N_HEADS * HEAD_DIM), BF16)
    return pl.pallas_call(
        functools.partial(_attn_latent_kernel, tk=tk),
        grid=(b, l // tq),
        in_specs=[q_spec, q_spec, k_spec, vt_spec, k_spec, vt_spec, kc_spec, vtc_spec, kc_spec, vtc_spec,
                  pl.BlockSpec((1, N_HEADS * tq), lambda b, i: (0, 0))],
        out_specs=[o_spec, o_spec],
        out_shape=[o_shape, o_shape],
        scratch_shapes=[
            pltpu.VMEM((buf_rows, buf_cols), F32),
            pltpu.VMEM((buf_rows, buf_cols), F32),
            pltpu.VMEM((buf_rows, buf_cols), BF16),
            pltpu.VMEM((buf_rows, buf_cols), BF16),
            pltpu.VMEM((1, N_HEADS * tq), F32),
            pltpu.VMEM((2, 1, N_HEADS * tq), F32),
            pltpu.VMEM((2, 1, N_HEADS * tq), F32),
            pltpu.VMEM((2, HEAD_DIM + DEN_ROWS, GROUP * tq), F32),
        ],
        compiler_params=_params("parallel", "parallel"),
        name="attn_latent",
    )(qa, qb, ka, vat, kb, vbt, ka_c, vat_c, kb_c, vbt_c, sink_row)


def _softmax_pv(q_ref, ks, vts, extra, mask=None):
    tq = q_ref.shape[2]
    aug = [[jnp.concatenate([vt[g * HEAD_DIM:(g + 1) * HEAD_DIM, :],
                             jnp.ones((DEN_ROWS, vt.shape[1]), BF16)], axis=0) for vt in vts]
           for g in range(2)]
    out = []
    for h in range(N_HEADS):
        ss = [_nt(k, q_ref[0, h]) for k in ks]
        if mask is not None:
            ss[0] = jnp.where(mask, ss[0], NEG_INF)
        m = functools.reduce(jnp.maximum, [jnp.max(s, axis=0, keepdims=True) for s in ss])
        if extra is not None:
            e = extra[:, h * tq:(h + 1) * tq]
            m = jnp.maximum(m, e)
        acc = functools.reduce(
            jnp.add, [_dot(vt, jnp.exp2(s - m).astype(BF16)) for s, vt in zip(ss, aug[h // GROUP])])
        den = acc[HEAD_DIM:HEAD_DIM + 1]
        if extra is not None:
            den = den + jnp.exp2(e - m)
        out.append(acc[0:HEAD_DIM] / den)
    return out


def _attn_ctx_kernel(q_ref, k_ref, vt_ref, *rest, has_sink):
    if has_sink:
        sink_ref, o_ref = rest
    else:
        (o_ref,) = rest
    extra = sink_ref[...] if has_sink else None
    _store_heads(o_ref, _softmax_pv(q_ref, [k_ref[0]], [vt_ref[0]], extra))


def _attn_ctx(q, k, vt, sink_row, *, batch):
    lc = q.shape[2] // batch
    in_specs = [
        pl.BlockSpec((1, N_HEADS, lc, LANE), lambda b: (0, 0, b, 0)),
        pl.BlockSpec((1, lc, LANE), lambda b: (0, b, 0)),
        pl.BlockSpec((1, LANE, lc), lambda b: (0, 0, b)),
    ]
    args = [q, k, vt]
    if sink_row is not None:
        in_specs.append(pl.BlockSpec((1, N_HEADS * lc), lambda b: (0, 0)))
        args.append(sink_row)
    return pl.pallas_call(
        functools.partial(_attn_ctx_kernel, has_sink=sink_row is not None),
        grid=(batch,),
        in_specs=in_specs,
        out_specs=pl.BlockSpec((1, lc, N_HEADS * HEAD_DIM), lambda b: (0, b, 0)),
        out_shape=jax.ShapeDtypeStruct((1, batch * lc, N_HEADS * HEAD_DIM), BF16),
        compiler_params=_params("parallel"),
        name="attn_ctx",
    )(*args)


def _stack3_lhs(hi, lo):
    return jnp.concatenate([hi, hi, lo], axis=1)


def _stack3_rhs(x):
    hi, lo = _split3(x)
    return jnp.concatenate([hi, lo, hi], axis=0)


def _fft1_kernel(x_ref, t_ref, twc_ref, tws_ref, o_ref):
    tn2 = x_ref.shape[2]
    n1 = x_ref.shape[1]
    xt = jnp.transpose(x_ref[0], (1, 0, 2))
    y = _dot(t_ref[...], _stack3_rhs(jnp.concatenate([xt[i] for i in range(tn2)], axis=1)))
    cx, sx = y[0:n1], y[n1:]
    w = 2 * C_WIDTH
    outs = []
    for i in range(tn2):
        tc = jnp.concatenate([twc_ref[i]] * (C_WIDTH // LANE), axis=1)
        ts = jnp.concatenate([tws_ref[i]] * (C_WIDTH // LANE), axis=1)
        re, im = slice(i * w, i * w + C_WIDTH), slice(i * w + C_WIDTH, (i + 1) * w)
        ar = cx[:, re] + sx[:, im]
        ai = cx[:, im] - sx[:, re]
        outs.append(jnp.concatenate([ar * tc + ai * ts, ai * tc - ar * ts], axis=1))
    o_ref[0] = jnp.transpose(jnp.stack(outs, axis=0), (1, 0, 2))


def _real_dft(t_ref, x):
    rhs = jnp.concatenate([_stack3_rhs(x[:, 0:C_WIDTH]), _stack3_rhs(x[:, C_WIDTH:])], axis=0)
    return _dot(t_ref[...], rhs)


def _fft2_kernel(x_ref, t_ref, o_ref):
    outs = [_real_dft(t_ref, x_ref[0, i]) for i in range(x_ref.shape[1])]
    o_ref[0] = jnp.transpose(jnp.stack(outs, axis=0), (1, 0, 2))


def _dft_tables(n, norm):
    k = np.arange(n, dtype=np.int64)
    ang = 2.0 * np.pi * ((k[:, None] * k[None, :]) % n).astype(np.float64) / n
    c = jnp.asarray(np.cos(ang) * norm, F32)
    s = jnp.asarray(np.sin(ang) * norm, F32)
    return _stack3_lhs(*_split3(c)), _stack3_lhs(*_split3(s))


def _fourier_latent(z):
    b, l, w = z.shape
    n1, n2 = FFT_N1, l // FFT_N1
    tn2, tk1 = 8, 8
    prod = jnp.arange(n2)[:, None] * jnp.arange(n1)[None, :]
    ang = prod.astype(F32) * (2.0 * np.pi / l)
    twc = jnp.broadcast_to(jnp.cos(ang)[:, :, None], (n2, n1, LANE))
    tws = jnp.broadcast_to(jnp.sin(ang)[:, :, None], (n2, n1, LANE))
    t1 = jnp.concatenate(_dft_tables(n1, 1.0), axis=0)
    y = pl.pallas_call(
        _fft1_kernel,
        grid=(b, n2 // tn2),
        in_specs=[pl.BlockSpec((1, n1, tn2, w), lambda b, j: (b, 0, j, 0))]
        + [_resident(t1.shape)]
        + [pl.BlockSpec((tn2, n1, LANE), lambda b, j: (j, 0, 0))] * 2,
        out_specs=pl.BlockSpec((1, n1, tn2, w), lambda b, j: (b, 0, j, 0)),
        out_shape=jax.ShapeDtypeStruct((b, n1, n2, w), F32),
        compiler_params=_params("parallel", "parallel"),
        name="fft_stage1",
    )(z.reshape(b, n1, n2, w), t1, twc, tws)
    t2m = jnp.concatenate(_dft_tables(n2, 1.0 / np.sqrt(l)), axis=1)
    out = pl.pallas_call(
        _fft2_kernel,
        grid=(b, n1 // tk1),
        in_specs=[pl.BlockSpec((1, tk1, n2, w), lambda b, j: (b, j, 0, 0)), _resident(t2m.shape)],
        out_specs=pl.BlockSpec((1, n2, tk1, C_WIDTH), lambda b, j: (b, 0, j, 0)),
        out_shape=jax.ShapeDtypeStruct((b, n2, n1, C_WIDTH), F32),
        compiler_params=_params("parallel", "parallel"),
        name="fft_stage2",
    )(y, t2m)
    return out.reshape(b, l, C_WIDTH)


def _fourier_ctx_kernel(x_ref, t_ref, o_ref):
    o_ref[0] = _real_dft(t_ref, x_ref[0])


def _fourier_ctx(z, *, batch):
    lc = z.shape[1] // batch
    tabs = jnp.concatenate(_dft_tables(lc, 1.0 / np.sqrt(lc)), axis=1)
    return pl.pallas_call(
        _fourier_ctx_kernel,
        grid=(batch,),
        in_specs=[pl.BlockSpec((1, lc, 2 * C_WIDTH), lambda b: (0, b, 0)), _resident(tabs.shape)],
        out_specs=pl.BlockSpec((1, lc, C_WIDTH), lambda b: (0, b, 0)),
        out_shape=jax.ShapeDtypeStruct((1, batch * lc, C_WIDTH), F32),
        compiler_params=_params("parallel"),
        name="fourier_ctx",
    )(z, tabs)


def _rope_tables(length):
    t = jnp.arange(length)
    row = (t // GRID_W).astype(F32)
    col = (t % GRID_W).astype(F32)
    half = HEAD_DIM // 2
    inv = ROPE_THETA ** (-jnp.arange(0, half, 2, dtype=F32) / half)
    ar, ac = row[:, None] * inv, col[:, None] * inv
    cos = jnp.concatenate([jnp.cos(ar), jnp.cos(ar), jnp.cos(ac), jnp.cos(ac)], axis=-1)
    sin = jnp.concatenate([-jnp.sin(ar), jnp.sin(ar), -jnp.sin(ac), jnp.sin(ac)], axis=-1)
    return jnp.tile(cos, (1, 2)), jnp.tile(sin, (1, 2))


def _channel_dft():
    k = np.arange(C_GROUP_DIM)
    ang = 2.0 * np.pi * ((k[:, None] * k[None, :]) % C_GROUP_DIM) / C_GROUP_DIM
    eye = np.eye(C_GROUPS)
    norm = 1.0 / np.sqrt(C_GROUP_DIM)
    cs = np.concatenate([np.kron(eye, np.cos(ang) * norm), np.kron(eye, -np.sin(ang) * norm)], axis=1)
    return _split3(jnp.asarray(cs, F32))


def _block_diag(w):
    g, c, _ = w.shape
    eye = jnp.eye(g, dtype=w.dtype)
    return (eye[:, None, :, None] * w[:, :, None, :]).reshape(g * c, g * c)


def kernel(x, c, ctx, c_ctx, w_ada, b_ada, g_ffn1, g_mix, g_ffn2, w_in, g_qn, g_kn, sink, w_four,
           w_out, w1_gate, w1_up, w1_down, w2_gate, w2_up, w2_down, g_final):
    batch, length, d = x.shape
    lc = ctx.shape[1]
    depth = w_ada.shape[0]
    assert length % (FFT_N1 * 8) == 0 and length >= BLOCK + 2 * WINDOW and batch + 1 <= 8
    tm = 512
    tm_ctx = batch * lc

    cvec = jnp.zeros((8, d), F32).at[:batch].set(c).at[batch].set(c_ctx)
    mod = _ada(cvec, w_ada, b_ada).reshape(depth, 8, N_MOD, d)

    cos, sin = _rope_tables(length)
    ones = jnp.ones((tm_ctx, LANE), F32)
    msm = _block_diag(jnp.full((2, HEAD_DIM, HEAD_DIM), 1.0 / HEAD_DIM, F32)).astype(BF16)
    wc = _wfold(*_channel_dft(), jax.vmap(_block_diag)(w_four))

    w1 = tuple(w.astype(BF16) for w in (w1_gate, w1_up, w1_down))
    w2 = tuple(w.astype(BF16) for w in (w2_gate, w2_up, w2_down))
    win, wout = w_in.astype(BF16), w_out.astype(BF16)

    h = x
    hc = ctx.reshape(1, batch * lc, d)
    for l in range(depth):
        last = l == depth - 1
        mod_l, mod_c = mod[l, :batch], mod[l, batch:batch + 1]
        gq = jnp.tile(g_qn[l].reshape(1, HEAD_DIM), (1, 2))
        gk = jnp.tile(g_kn[l].reshape(1, HEAD_DIM), (1, 2))
        proj = dict(g=g_mix[l], w_in=win, gq=gq, gk=gk, msm=msm, wc=wc)

        h, qa, ka, vat, qb, kb, vbt, z = _ffn(
            h, mod_l, g_ffn1[l], *w1, layer=l, row0=0, tm=tm, proj=dict(proj, cos=cos, sin=sin, rope=True))
        hc, qa_c, ka_c, vat_c, qb_c, kb_c, vbt_c, z_c = _ffn(
            hc, mod_c, g_ffn1[l], *w1, layer=l, row0=0, tm=tm_ctx, proj=dict(proj, cos=ones, sin=ones, rope=False))

        tq = 512
        sink_l = jnp.repeat(sink[l].astype(F32) * LOG2E, tq).reshape(1, N_HEADS * tq)
        oa, ob = _attn_latent(qa, ka, vat, qb, kb, vbt, ka_c, vat_c, kb_c, vbt_c, sink_l,
                              tq=tq, tk=_key_tile(length + lc))
        oc = _fourier_latent(z)
        h = _ffn(h, mod_l, g_ffn2[l], *w2, layer=l, row0=6, tm=tm, mix=(oa, ob, oc, wout),
                 final_g=g_final if last else None)

        if not last:
            sink_c = jnp.repeat(sink[l].astype(F32) * LOG2E, lc).reshape(1, N_HEADS * lc)
            oa_c = _attn_ctx(qa_c, ka_c, vat_c, sink_c, batch=batch)
            ob_c = _attn_ctx(qb_c, kb_c, vbt_c, None, batch=batch)
            oc_c = _fourier_ctx(z_c, batch=batch)
            hc = _ffn(hc, mod_c, g_ffn2[l], *w2, layer=l, row0=6, tm=tm_ctx, mix=(oa_c, ob_c, oc_c, wout))
    return h
```

```python
import functools

import jax
import jax.numpy as jnp
import numpy as np
from jax import lax
from jax.experimental import pallas as pl
from jax.experimental.pallas import tpu as pltpu

HEAD_DIM = 64
A_HEADS = 6
A_KV_HEADS = 2
B_HEADS = 6
B_KV_HEADS = 2
C_GROUPS = 4
C_GROUP_DIM = 64
GRID_W = 64
WINDOW = 128
BLOCK = 128
ROPE_THETA = 10000.0
NORM_EPS = 1e-6
N_MOD = 9
NEG_INF = -1e30

N_HEADS = 6
GROUP = 3
C_WIDTH = C_GROUPS * C_GROUP_DIM
FFT_N1 = 64
WIN_TILE = 256
DEN_ROWS = 16
LOG2E = 1.4426950408889634

LANE = 128
VMEM_LIMIT = 60 * 1024 * 1024

F32 = jnp.float32
BF16 = jnp.bfloat16


def _params(*sem):
    return pltpu.CompilerParams(dimension_semantics=tuple(sem), vmem_limit_bytes=VMEM_LIMIT)


def _resident(shape):
    nd = len(shape)
    return pl.BlockSpec(shape, lambda *_: (0,) * nd, pipeline_mode=pl.Buffered(1))


def _layer_resident(stacked, layer):
    nd = stacked.ndim
    return pl.BlockSpec((None,) + stacked.shape[1:], lambda *_: (layer,) + (0,) * (nd - 1),
                        pipeline_mode=pl.Buffered(1))


def _split3(x):
    hi = x.astype(BF16)
    lo = (x - hi.astype(F32)).astype(BF16)
    return hi, lo


def _dot(a, b):
    return jnp.dot(a, b, preferred_element_type=F32)


def _dot3(a_hi, a_lo, b_hi, b_lo):
    return _dot(a_hi, b_hi) + _dot(a_hi, b_lo) + _dot(a_lo, b_hi)


def _silu(x):
    return x / (1.0 + jnp.exp(-x))


def _norm_mod(x, g, shift, scale):
    ms = jnp.mean(x * x, axis=-1, keepdims=True)
    return (x * lax.rsqrt(ms + NORM_EPS)) * (g * (1.0 + scale)) + shift


def _ada_kernel(c_ref, w_ref, b_ref, o_ref):
    s_hi, s_lo = _split3(_silu(c_ref[...]))
    w_hi, w_lo = _split3(w_ref[0])
    o_ref[0] = _dot3(s_hi, s_lo, w_hi, w_lo) + b_ref[0]


def _ada(cvec, w_ada, b_ada):
    depth, d, nd = w_ada.shape
    tn = 1024
    return pl.pallas_call(
        _ada_kernel,
        grid=(depth, nd // tn),
        in_specs=[
            pl.BlockSpec((8, d), lambda l, j: (0, 0)),
            pl.BlockSpec((1, d, tn), lambda l, j: (l, 0, j)),
            pl.BlockSpec((1, 1, tn), lambda l, j: (l, 0, j)),
        ],
        out_specs=pl.BlockSpec((1, 8, tn), lambda l, j: (l, 0, j)),
        out_shape=jax.ShapeDtypeStruct((depth, 8, nd), F32),
        compiler_params=_params("parallel", "parallel"),
        name="ada",
    )(cvec, w_ada, b_ada.reshape(depth, 1, nd))


def _mix_out(oa_ref, ob_ref, oc_ref, w_ref):
    na = N_HEADS * HEAD_DIM
    return (_dot(oa_ref[0], w_ref[0:na, :]) + _dot(ob_ref[0], w_ref[na:2 * na, :])
            + _dot(oc_ref[0].astype(BF16), w_ref[2 * na:, :]))


N_PROJ_IN = 8


def _ffn_kernel(*refs, row0, final, mixed, rope):
    refs = list(refs)
    x_ref, mod_ref, g_ref, wg_ref, wu_ref, wd_ref = refs[:6]
    del refs[:6]
    if mixed:
        oa_ref, ob_ref, oc_ref, wo_ref = refs[:4]
        del refs[:4]
    if final:
        gf_ref = refs.pop(0)
    proj_in = refs[:N_PROJ_IN] if rope is not None else []
    o_ref = refs[len(proj_in)]
    proj_out = refs[len(proj_in) + 1:]
    shift = mod_ref[0, row0:row0 + 1, :]
    scale = mod_ref[0, row0 + 1:row0 + 2, :]
    gate = mod_ref[0, row0 + 2:row0 + 3, :]
    x = x_ref[0]
    if mixed:
        x = x + mod_ref[0, 5:6, :] * _mix_out(oa_ref, ob_ref, oc_ref, wo_ref)
    xm = _norm_mod(x, g_ref[...], shift, scale).astype(BF16)
    a = _dot(xm, wg_ref[...])
    u = _dot(xm, wu_ref[...])
    act = (_silu(a) * u).astype(BF16)
    y = x + (0.5 * gate) * _dot(act, wd_ref[...])
    if final:
        ms = jnp.mean(y * y, axis=-1, keepdims=True)
        y = (y * lax.rsqrt(ms + NORM_EPS)) * gf_ref[...]
    o_ref[0] = y
    if rope is not None:
        _proj_body(y, mod_ref, *proj_in, *proj_out, rope=rope)


def _ffn(x, mod, g, wg, wu, wd, *, layer, row0, tm, mix=None, final_g=None, proj=None):
    groups, t, d = x.shape
    final = final_g is not None
    tok = lambda b, i: (b, i, 0)
    in_specs = [
        pl.BlockSpec((1, tm, d), tok),
        pl.BlockSpec((1, N_MOD, d), lambda b, i: (b, 0, 0)),
        _resident((1, d)),
        _layer_resident(wg, layer),
        _layer_resident(wu, layer),
        _layer_resident(wd, layer),
    ]
    args = [x, mod, g.reshape(1, d), wg, wu, wd]
    if mix is not None:
        oa, ob, oc, w_out = mix
        in_specs += [pl.BlockSpec((1, tm, oa.shape[2]), tok), pl.BlockSpec((1, tm, ob.shape[2]), tok),
                     pl.BlockSpec((1, tm, oc.shape[2]), tok), _layer_resident(w_out, layer)]
        args += [oa, ob, oc, w_out]
    if final:
        in_specs.append(_resident((1, d)))
        args.append(final_g.reshape(1, d))
    out_specs = [pl.BlockSpec((1, tm, d), tok)]
    out_shape = [jax.ShapeDtypeStruct((groups, t, d), F32)]
    if proj is not None:
        in_specs += [
            _resident((1, d)),
            _layer_resident(proj["w_in"], layer),
            pl.BlockSpec((tm, LANE), lambda b, i: (i, 0)),
            pl.BlockSpec((tm, LANE), lambda b, i: (i, 0)),
            _resident((1, LANE)),
            _resident((1, LANE)),
            _resident((LANE, LANE)),
            _layer_resident(proj["wc"], layer),
        ]
        args += [proj["g"].reshape(1, d), proj["w_in"], proj["cos"], proj["sin"], proj["gq"], proj["gk"],
                 proj["msm"], proj["wc"]]
        q_spec = pl.BlockSpec((1, N_HEADS, tm, LANE), lambda b, i: (b, 0, i, 0))
        k_spec = pl.BlockSpec((1, tm, LANE), tok)
        vt_spec = pl.BlockSpec((1, LANE, tm), lambda b, i: (b, 0, i))
        q_shape = jax.ShapeDtypeStruct((groups, N_HEADS, t, LANE), BF16)
        k_shape = jax.ShapeDtypeStruct((groups, t, LANE), BF16)
        vt_shape = jax.ShapeDtypeStruct((groups, LANE, t), BF16)
        out_specs += [q_spec, k_spec, vt_spec, q_spec, k_spec, vt_spec, pl.BlockSpec((1, tm, 2 * C_WIDTH), tok)]
        out_shape += [q_shape, k_shape, vt_shape, q_shape, k_shape, vt_shape,
                      jax.ShapeDtypeStruct((groups, t, 2 * C_WIDTH), F32)]
    out = pl.pallas_call(
        functools.partial(_ffn_kernel, row0=row0, final=final, mixed=mix is not None,
                          rope=None if proj is None else proj["rope"]),
        grid=(groups, t // tm),
        in_specs=in_specs,
        out_specs=out_specs,
        out_shape=out_shape,
        compiler_params=_params("parallel", "parallel"),
        name="ffn",
    )(*args)
    return out[0] if proj is None else out


def _rope_slab(t, cos, sin, lane):
    fwd = pltpu.roll(t, LANE - 16, axis=1)
    bwd = pltpu.roll(t, 16, axis=1)
    partner = jnp.where((lane % 32) < 16, fwd, bwd)
    return t * cos + partner * sin


def _wfold_kernel(cs_hi_ref, cs_lo_ref, w_ref, o_ref):
    w_hi, w_lo = _split3(w_ref[0])
    for half in range(2):
        sl = slice(half * C_WIDTH, (half + 1) * C_WIDTH)
        o_ref[0, :, sl] = _dot3(cs_hi_ref[:, sl], cs_lo_ref[:, sl], w_hi, w_lo).astype(BF16)


def _wfold(cs_hi, cs_lo, wbd):
    depth = wbd.shape[0]
    return pl.pallas_call(
        _wfold_kernel,
        grid=(depth,),
        in_specs=[_resident(cs_hi.shape), _resident(cs_lo.shape),
                  pl.BlockSpec((1, C_WIDTH, C_WIDTH), lambda l: (l, 0, 0))],
        out_specs=pl.BlockSpec((1, C_WIDTH, 2 * C_WIDTH), lambda l: (l, 0, 0)),
        out_shape=jax.ShapeDtypeStruct((depth, C_WIDTH, 2 * C_WIDTH), BF16),
        compiler_params=_params("parallel"),
        name="wfold",
    )(cs_hi, cs_lo, wbd)


def _proj_body(x, mod_ref, g_ref, w_ref, cos_ref, sin_ref, gq_ref, gk_ref, msm_ref, wc_ref,
               qa_ref, ka_ref, vat_ref, qb_ref, kb_ref, vbt_ref, z_ref, *, rope):
    tm = x.shape[0]
    shift = mod_ref[0, 3:4, :]
    scale = mod_ref[0, 4:5, :]
    xm = _norm_mod(x, g_ref[...], shift, scale).astype(BF16)

    u_all = _dot(xm, w_ref[...])

    def u(lo, hi):
        return u_all[:, lo:hi]

    lane = lax.broadcasted_iota(jnp.int32, (tm, LANE), 1)
    low = lane < HEAD_DIM
    if rope:
        cos = cos_ref[...]
        sin = sin_ref[...]

    def qk_norm(t, gain):
        ms = _dot((t * t).astype(BF16), msm_ref[...])
        return t * lax.rsqrt(ms + NORM_EPS) * gain

    def finish(t):
        return _rope_slab(t, cos, sin, lane) if rope else t

    def emit_q(q_ref, slabs):
        for h in range(N_HEADS):
            t = slabs[h // 2] * (LOG2E * HEAD_DIM ** -0.5)
            if (h % 2) != (h // GROUP):
                t = pltpu.roll(t, HEAD_DIM, axis=1)
            keep = low if (h // GROUP) == 0 else jnp.logical_not(low)
            q_ref[0, h] = jnp.where(keep, t, 0.0).astype(BF16)

    qa = [finish(u(LANE * i, LANE * (i + 1))) for i in range(3)]
    emit_q(qa_ref, qa)
    ka_ref[0] = finish(u(384, 512)).astype(BF16)
    vat_ref[0] = u(512, 640).T.astype(BF16)

    qb = [finish(qk_norm(u(640 + LANE * i, 640 + LANE * (i + 1)), gq_ref[...])) for i in range(3)]
    emit_q(qb_ref, qb)
    kb_ref[0] = finish(qk_norm(u(1024, 1152), gk_ref[...])).astype(BF16)
    vbt_ref[0] = u(1152, 1280).T.astype(BF16)

    z_ref[0] = _dot(u(1280, 1536).astype(BF16), wc_ref[...])


def _nt(a, b):
    return lax.dot_general(a, b, (((1,), (1,)), ((), ())), preferred_element_type=F32)


def _store_heads(o_ref, ot, r0=0):
    tq = ot[0].shape[1]
    for pair in range(N_HEADS // 2):
        slab = jnp.concatenate([ot[2 * pair], ot[2 * pair + 1]], axis=0)
        o_ref[0, r0:r0 + tq, pair * LANE:(pair + 1) * LANE] = slab.T.astype(o_ref.dtype)


def _attn_latent_kernel(*refs, tq, tk):
    s_a, s_b, p_a, p_b = refs[13:17]
    n_sub = refs[1].shape[2] // tq
    tiles = [_tile_stages(sub, n_sub, *refs, tq=tq, tk=tk) for sub in range(n_sub)]
    nk = tiles[0]["nk"]
    n_steps = nk + 1
    total = n_sub * n_steps
    bufs = ((s_a, p_a), (s_b, p_b))

    def issue_scores(u):
        j, t = divmod(u, n_steps)
        s_buf, slot = bufs[u % 2][0], u % 2
        if t == 0:
            tiles[j]["scores_win"](s_buf, slot)
        else:
            tiles[j]["scores"](t - 1, s_buf, slot)

    def do_softmax(u):
        j, t = divmod(u, n_steps)
        (s_buf, p_buf), slot = bufs[u % 2], u % 2
        tiles[j]["softmax_win" if t == 0 else "softmax"](s_buf, p_buf, slot)

    def do_values(u):
        j, t = divmod(u, n_steps)
        p_buf, slot = bufs[u % 2][1], u % 2
        if t == 0:
            tiles[j]["values_win"](p_buf, slot)
        else:
            tiles[j]["values"](t - 1, p_buf, slot)
        if t == n_steps - 1:
            tiles[j]["finish"]()

    def flat_step(u):
        if u + 1 < total:
            issue_scores(u + 1)
        do_softmax(u)
        if u >= 1:
            do_values(u - 1)

    n_pairs = max(0, (nk - 3) // 2)
    issue_scores(0)
    for j in range(n_sub):
        base = j * n_steps
        flat_step(base)
        flat_step(base + 1)

        def pair(i, carry, tile=tiles[j], base=base):
            t = 2 + 2 * i
            for dt in (0, 1):
                par = (base + dt) % 2
                (s_cur, p_cur), (s_oth, p_oth) = bufs[par], bufs[1 - par]
                tile["scores"](t + dt, s_oth, 1 - par)
                tile["softmax"](s_cur, p_cur, par)
                tile["values"](t + dt - 2, p_oth, 1 - par)
            return carry

        lax.fori_loop(0, n_pairs, pair, 0)
        for t in range(2 + 2 * n_pairs, n_steps):
            flat_step(base + t)
    do_values(total - 1)


def _tile_stages(sub, n_sub, qa_ref, qb_ref, ka_ref, vat_ref, kb_ref, vbt_ref, kac_ref, vatc_ref, kbc_ref,
                 vbtc_ref, sink_ref, oa_ref, ob_ref, s_a, s_b, p_a, p_b, m_sc, bmax_sc, alpha_sc, acc_sc, *, tq, tk):
    length = kb_ref.shape[1]
    lc = kbc_ref.shape[1]
    nk = (length + lc) // tk
    tail = length - (nk - 1) * tk
    wq = min(tq, WIN_TILE)
    span = wq + 2 * WINDOW
    na = span + lc
    tile = pl.program_id(1) * n_sub + sub
    r0 = sub * tq
    m_sc, acc_sc = m_sc.at[sub], acc_sc.at[sub]
    starts = [pl.multiple_of(jnp.clip(tile * tq + i * wq - WINDOW, 0, length - span), LANE)
              for i in range(tq // wq)]
    m_sc[...] = jnp.full_like(m_sc, NEG_INF)
    acc_sc[...] = jnp.zeros_like(acc_sc)

    def with_ones(vt):
        return jnp.concatenate([vt, jnp.ones((DEN_ROWS, vt.shape[1]), BF16)], axis=0)

    def rows(g):
        return slice(g * HEAD_DIM, (g + 1) * HEAD_DIM)

    def sub_cols(h, i):
        return slice(h * tq + i * wq, h * tq + (i + 1) * wq)

    def scores_win(s_buf, slot):
        for i, start in enumerate(starts):
            kw = ka_ref[0, pl.ds(start, span), :]
            kpos = start + lax.broadcasted_iota(jnp.int32, (span, wq), 0)
            qpos = tile * tq + i * wq + lax.broadcasted_iota(jnp.int32, (span, wq), 1)
            valid = jnp.abs(kpos - qpos) <= WINDOW
            for h in range(N_HEADS):
                cs = sub_cols(h, i)
                q = qa_ref[0, h, r0 + i * wq:r0 + (i + 1) * wq, :]
                s_w = jnp.where(valid, _nt(kw, q), NEG_INF)
                s_c = _nt(kac_ref[0], q)
                s_buf[0:span, cs] = s_w
                s_buf[span:na, cs] = s_c
                bmax_sc[slot, :, cs] = jnp.maximum(jnp.max(s_w, axis=0, keepdims=True),
                                                   jnp.max(s_c, axis=0, keepdims=True))

    def softmax_win(s_buf, p_buf, slot):
        for h in range(N_HEADS):
            hc = head_cols(h)
            sink = sink_ref[:, hc]
            m = jnp.maximum(sink, bmax_sc[slot, :, hc])
            alpha_sc[slot, :, hc] = jnp.exp2(sink - m)
            p_buf[0:na, hc] = jnp.exp2(s_buf[0:na, hc] - m).astype(BF16)

    def values_win(p_buf, slot):
        parts = [[None] * len(starts) for _ in range(N_HEADS)]
        for g in range(2):
            for i, start in enumerate(starts):
                vt = with_ones(jnp.concatenate(
                    [vat_ref[0, rows(g), pl.ds(start, span)], vatc_ref[0, rows(g), :]], axis=1))
                for h in range(g * GROUP, (g + 1) * GROUP):
                    cs = sub_cols(h, i)
                    acc = _dot(vt, p_buf[0:na, cs])
                    parts[h][i] = acc[0:HEAD_DIM] / (acc[HEAD_DIM:HEAD_DIM + 1] + alpha_sc[slot, :, cs])
        _store_heads(oa_ref, [jnp.concatenate(p, axis=1) for p in parts], r0)

    def key_block(blk):
        if isinstance(blk, int) and blk == nk - 1:
            return jnp.concatenate([kb_ref[0, length - tail:, :], kbc_ref[0]], axis=0)
        return kb_ref[0, pl.ds(pl.multiple_of(blk * tk, tk), tk), :]

    def value_block(blk, g):
        if isinstance(blk, int) and blk == nk - 1:
            return jnp.concatenate([vbt_ref[0, rows(g), length - tail:], vbtc_ref[0, rows(g), :]], axis=1)
        return vbt_ref[0, rows(g), pl.ds(pl.multiple_of(blk * tk, tk), tk)]

    def head_cols(h):
        return slice(h * tq, (h + 1) * tq)

    def scores(blk, s_buf, slot):
        kb = key_block(blk)
        for h in range(N_HEADS):
            s = _nt(kb, qb_ref[0, h, r0:r0 + tq, :])
            s_buf[0:tk, head_cols(h)] = s
            bmax_sc[slot, :, head_cols(h)] = jnp.max(s, axis=0, keepdims=True)

    def softmax(s_buf, p_buf, slot):
        for h in range(N_HEADS):
            hc = head_cols(h)
            m_old = m_sc[:, hc]
            m_new = jnp.maximum(m_old, bmax_sc[slot, :, hc])
            m_sc[:, hc] = m_new
            alpha_sc[slot, :, hc] = jnp.exp2(m_old - m_new)
            p_buf[0:tk, hc] = jnp.exp2(s_buf[0:tk, hc] - m_new).astype(BF16)

    def values(blk, p_buf, slot):
        for g in range(2):
            vt = with_ones(value_block(blk, g))
            for h in range(g * GROUP, (g + 1) * GROUP):
                hc = head_cols(h)
                acc_sc[h] = acc_sc[h] * alpha_sc[slot, :, hc] + _dot(vt, p_buf[0:tk, hc])

    def finish():
        _store_heads(ob_ref, [acc_sc[h, 0:HEAD_DIM] / acc_sc[h, HEAD_DIM:HEAD_DIM + 1] for h in range(N_HEADS)], r0)

    return dict(nk=nk, scores_win=scores_win, softmax_win=softmax_win, values_win=values_win,
                scores=scores, softmax=softmax, values=values, finish=finish)


def _key_tile(n_keys, cap=1024):
    return max(t for t in range(LANE, cap + 1, LANE) if n_keys % t == 0)


def _attn_latent(qa, ka, vat, qb, kb, vbt, ka_c, vat_c, kb_c, vbt_c, sink_row, *, tq, tk, n_sub):
    b, _, l, _ = qb.shape
    lc = ka_c.shape[1] // b
    assert (l + lc) % tk == 0 and (l + lc) // tk * tk - tk <= l and tk > lc
    buf_rows = max(tk, min(tq, WIN_TILE) + 2 * WINDOW + lc)
    buf_cols = N_HEADS * tq
    q_spec = pl.BlockSpec((1, N_HEADS, n_sub * tq, LANE), lambda b, i: (b, 0, i, 0))
    k_spec = pl.BlockSpec((1, l, LANE), lambda b, i: (b, 0, 0), pipeline_mode=pl.Buffered(1))
    vt_spec = pl.BlockSpec((1, LANE, l), lambda b, i: (b, 0, 0), pipeline_mode=pl.Buffered(1))
    kc_spec = pl.BlockSpec((1, lc, LANE), lambda b, i: (0, b, 0), pipeline_mode=pl.Buffered(1))
    vtc_spec = pl.BlockSpec((1, LANE, lc), lambda b, i: (0, 0, b), pipeline_mode=pl.Buffered(1))
    o_spec = pl.BlockSpec((1, n_sub * tq, N_HEADS * HEAD_DIM), lambda b, i: (b, i, 0))
    o_shape = jax.ShapeDtypeStruct((b, l, N_HEADS * HEAD_DIM), BF16)
    return pl.pallas_call(
        functools.partial(_attn_latent_kernel, tq=tq, tk=tk),
        grid=(b, l // (n_sub * tq)),
        in_specs=[q_spec, q_spec, k_spec, vt_spec, k_spec, vt_spec, kc_spec, vtc_spec, kc_spec, vtc_spec,
                  pl.BlockSpec((1, N_HEADS * tq), lambda b, i: (0, 0))],
        out_specs=[o_spec, o_spec],
        out_shape=[o_shape, o_shape],
        scratch_shapes=[
            pltpu.VMEM((buf_rows, buf_cols), F32),
            pltpu.VMEM((buf_rows, buf_cols), F32),
            pltpu.VMEM((buf_rows, buf_cols), BF16),
            pltpu.VMEM((buf_rows, buf_cols), BF16),
            pltpu.VMEM((n_sub, 1, N_HEADS * tq), F32),
            pltpu.VMEM((2, 1, N_HEADS * tq), F32),
            pltpu.VMEM((2, 1, N_HEADS * tq), F32),
            pltpu.VMEM((n_sub, N_HEADS, HEAD_DIM + DEN_ROWS, tq), F32),
        ],
        compiler_params=_params("parallel", "parallel"),
        name="attn_latent",
    )(qa, qb, ka, vat, kb, vbt, ka_c, vat_c, kb_c, vbt_c, sink_row)


def _softmax_pv(q_ref, ks, vts, extra, mask=None):
    tq = q_ref.shape[2]
    aug = [[jnp.concatenate([vt[g * HEAD_DIM:(g + 1) * HEAD_DIM, :],
                             jnp.ones((DEN_ROWS, vt.shape[1]), BF16)], axis=0) for vt in vts]
           for g in range(2)]
    out = []
    for h in range(N_HEADS):
        ss = [_nt(k, q_ref[0, h]) for k in ks]
        if mask is not None:
            ss[0] = jnp.where(mask, ss[0], NEG_INF)
        m = functools.reduce(jnp.maximum, [jnp.max(s, axis=0, keepdims=True) for s in ss])
        if extra is not None:
            e = extra[:, h * tq:(h + 1) * tq]
            m = jnp.maximum(m, e)
        acc = functools.reduce(
            jnp.add, [_dot(vt, jnp.exp2(s - m).astype(BF16)) for s, vt in zip(ss, aug[h // GROUP])])
        den = acc[HEAD_DIM:HEAD_DIM + 1]
        if extra is not None:
            den = den + jnp.exp2(e - m)
        out.append(acc[0:HEAD_DIM] / den)
    return out


def _attn_ctx_kernel(q_ref, k_ref, vt_ref, *rest, has_sink):
    if has_sink:
        sink_ref, o_ref = rest
    else:
        (o_ref,) = rest
    extra = sink_ref[...] if has_sink else None
    _store_heads(o_ref, _softmax_pv(q_ref, [k_ref[0]], [vt_ref[0]], extra))


def _attn_ctx(q, k, vt, sink_row, *, batch):
    lc = q.shape[2] // batch
    in_specs = [
        pl.BlockSpec((1, N_HEADS, lc, LANE), lambda b: (0, 0, b, 0)),
        pl.BlockSpec((1, lc, LANE), lambda b: (0, b, 0)),
        pl.BlockSpec((1, LANE, lc), lambda b: (0, 0, b)),
    ]
    args = [q, k, vt]
    if sink_row is not None:
        in_specs.append(pl.BlockSpec((1, N_HEADS * lc), lambda b: (0, 0)))
        args.append(sink_row)
    return pl.pallas_call(
        functools.partial(_attn_ctx_kernel, has_sink=sink_row is not None),
        grid=(batch,),
        in_specs=in_specs,
        out_specs=pl.BlockSpec((1, lc, N_HEADS * HEAD_DIM), lambda b: (0, b, 0)),
        out_shape=jax.ShapeDtypeStruct((1, batch * lc, N_HEADS * HEAD_DIM), BF16),
        compiler_params=_params("parallel"),
        name="attn_ctx",
    )(*args)


def _stack3_lhs(hi, lo):
    return jnp.concatenate([hi, hi, lo], axis=1)


def _stack3_rhs(x):
    hi, lo = _split3(x)
    return jnp.concatenate([hi, lo, hi], axis=0)


def _fft1_kernel(x_ref, t_ref, twc_ref, tws_ref, o_ref):
    tn2 = x_ref.shape[2]
    n1 = x_ref.shape[1]
    xt = jnp.transpose(x_ref[0], (1, 0, 2))
    y = _dot(t_ref[...], _stack3_rhs(jnp.concatenate([xt[i] for i in range(tn2)], axis=1)))
    cx, sx = y[0:n1], y[n1:]
    w = 2 * C_WIDTH
    outs = []
    for i in range(tn2):
        tc = jnp.concatenate([twc_ref[i]] * (C_WIDTH // LANE), axis=1)
        ts = jnp.concatenate([tws_ref[i]] * (C_WIDTH // LANE), axis=1)
        re, im = slice(i * w, i * w + C_WIDTH), slice(i * w + C_WIDTH, (i + 1) * w)
        ar = cx[:, re] + sx[:, im]
        ai = cx[:, im] - sx[:, re]
        outs.append(jnp.concatenate([ar * tc + ai * ts, ai * tc - ar * ts], axis=1))
    o_ref[0] = jnp.transpose(jnp.stack(outs, axis=0), (1, 0, 2))


def _real_dft(t_ref, x):
    rhs = jnp.concatenate([_stack3_rhs(x[:, 0:C_WIDTH]), _stack3_rhs(x[:, C_WIDTH:])], axis=0)
    return _dot(t_ref[...], rhs)


def _fft2_kernel(x_ref, t_ref, o_ref):
    outs = [_real_dft(t_ref, x_ref[0, i]) for i in range(x_ref.shape[1])]
    o_ref[0] = jnp.transpose(jnp.stack(outs, axis=0), (1, 0, 2))


def _dft_tables(n, norm):
    k = np.arange(n, dtype=np.int64)
    ang = 2.0 * np.pi * ((k[:, None] * k[None, :]) % n).astype(np.float64) / n
    c = jnp.asarray(np.cos(ang) * norm, F32)
    s = jnp.asarray(np.sin(ang) * norm, F32)
    return _stack3_lhs(*_split3(c)), _stack3_lhs(*_split3(s))


def _fourier_latent(z):
    b, l, w = z.shape
    n1, n2 = FFT_N1, l // FFT_N1
    tn2, tk1 = 16, 16
    prod = jnp.arange(n2)[:, None] * jnp.arange(n1)[None, :]
    ang = prod.astype(F32) * (2.0 * np.pi / l)
    twc = jnp.broadcast_to(jnp.cos(ang)[:, :, None], (n2, n1, LANE))
    tws = jnp.broadcast_to(jnp.sin(ang)[:, :, None], (n2, n1, LANE))
    t1 = jnp.concatenate(_dft_tables(n1, 1.0), axis=0)
    y = pl.pallas_call(
        _fft1_kernel,
        grid=(b, n2 // tn2),
        in_specs=[pl.BlockSpec((1, n1, tn2, w), lambda b, j: (b, 0, j, 0))]
        + [_resident(t1.shape)]
        + [pl.BlockSpec((tn2, n1, LANE), lambda b, j: (j, 0, 0))] * 2,
        out_specs=pl.BlockSpec((1, n1, tn2, w), lambda b, j: (b, 0, j, 0)),
        out_shape=jax.ShapeDtypeStruct((b, n1, n2, w), F32),
        compiler_params=_params("parallel", "parallel"),
        name="fft_stage1",
    )(z.reshape(b, n1, n2, w), t1, twc, tws)
    t2m = jnp.concatenate(_dft_tables(n2, 1.0 / np.sqrt(l)), axis=1)
    out = pl.pallas_call(
        _fft2_kernel,
        grid=(b, n1 // tk1),
        in_specs=[pl.BlockSpec((1, tk1, n2, w), lambda b, j: (b, j, 0, 0)), _resident(t2m.shape)],
        out_specs=pl.BlockSpec((1, n2, tk1, C_WIDTH), lambda b, j: (b, 0, j, 0)),
        out_shape=jax.ShapeDtypeStruct((b, n2, n1, C_WIDTH), F32),
        compiler_params=_params("parallel", "parallel"),
        name="fft_stage2",
    )(y, t2m)
    return out.reshape(b, l, C_WIDTH)


def _fourier_ctx_kernel(x_ref, t_ref, o_ref):
    o_ref[0] = _real_dft(t_ref, x_ref[0])


def _fourier_ctx(z, *, batch):
    lc = z.shape[1] // batch
    tabs = jnp.concatenate(_dft_tables(lc, 1.0 / np.sqrt(lc)), axis=1)
    return pl.pallas_call(
        _fourier_ctx_kernel,
        grid=(batch,),
        in_specs=[pl.BlockSpec((1, lc, 2 * C_WIDTH), lambda b: (0, b, 0)), _resident(tabs.shape)],
        out_specs=pl.BlockSpec((1, lc, C_WIDTH), lambda b: (0, b, 0)),
        out_shape=jax.ShapeDtypeStruct((1, batch * lc, C_WIDTH), F32),
        compiler_params=_params("parallel"),
        name="fourier_ctx",
    )(z, tabs)


def _rope_tables(length):
    t = jnp.arange(length)
    row = (t // GRID_W).astype(F32)
    col = (t % GRID_W).astype(F32)
    half = HEAD_DIM // 2
    inv = ROPE_THETA ** (-jnp.arange(0, half, 2, dtype=F32) / half)
    ar, ac = row[:, None] * inv, col[:, None] * inv
    cos = jnp.concatenate([jnp.cos(ar), jnp.cos(ar), jnp.cos(ac), jnp.cos(ac)], axis=-1)
    sin = jnp.concatenate([-jnp.sin(ar), jnp.sin(ar), -jnp.sin(ac), jnp.sin(ac)], axis=-1)
    return jnp.tile(cos, (1, 2)), jnp.tile(sin, (1, 2))


def _channel_dft():
    k = np.arange(C_GROUP_DIM)
    ang = 2.0 * np.pi * ((k[:, None] * k[None, :]) % C_GROUP_DIM) / C_GROUP_DIM
    eye = np.eye(C_GROUPS)
    norm = 1.0 / np.sqrt(C_GROUP_DIM)
    cs = np.concatenate([np.kron(eye, np.cos(ang) * norm), np.kron(eye, -np.sin(ang) * norm)], axis=1)
    return _split3(jnp.asarray(cs, F32))


def _block_diag(w):
    g, c, _ = w.shape
    eye = jnp.eye(g, dtype=w.dtype)
    return (eye[:, None, :, None] * w[:, :, None, :]).reshape(g * c, g * c)


def kernel(x, c, ctx, c_ctx, w_ada, b_ada, g_ffn1, g_mix, g_ffn2, w_in, g_qn, g_kn, sink, w_four,
           w_out, w1_gate, w1_up, w1_down, w2_gate, w2_up, w2_down, g_final):
    batch, length, d = x.shape
    lc = ctx.shape[1]
    depth = w_ada.shape[0]
    assert length % (FFT_N1 * 8) == 0 and length >= BLOCK + 2 * WINDOW and batch + 1 <= 8
    tm = 512
    tm_ctx = batch * lc

    cvec = jnp.zeros((8, d), F32).at[:batch].set(c).at[batch].set(c_ctx)
    mod = _ada(cvec, w_ada, b_ada).reshape(depth, 8, N_MOD, d)

    cos, sin = _rope_tables(length)
    ones = jnp.ones((tm_ctx, LANE), F32)
    msm = _block_diag(jnp.full((2, HEAD_DIM, HEAD_DIM), 1.0 / HEAD_DIM, F32)).astype(BF16)
    wc = _wfold(*_channel_dft(), jax.vmap(_block_diag)(w_four))

    w1 = tuple(w.astype(BF16) for w in (w1_gate, w1_up, w1_down))
    w2 = tuple(w.astype(BF16) for w in (w2_gate, w2_up, w2_down))
    win, wout = w_in.astype(BF16), w_out.astype(BF16)

    h = x
    hc = ctx.reshape(1, batch * lc, d)
    for l in range(depth):
        last = l == depth - 1
        mod_l, mod_c = mod[l, :batch], mod[l, batch:batch + 1]
        gq = jnp.tile(g_qn[l].reshape(1, HEAD_DIM), (1, 2))
        gk = jnp.tile(g_kn[l].reshape(1, HEAD_DIM), (1, 2))
        proj = dict(g=g_mix[l], w_in=win, gq=gq, gk=gk, msm=msm, wc=wc)

        h, qa, ka, vat, qb, kb, vbt, z = _ffn(
            h, mod_l, g_ffn1[l], *w1, layer=l, row0=0, tm=tm, proj=dict(proj, cos=cos, sin=sin, rope=True))
        hc, qa_c, ka_c, vat_c, qb_c, kb_c, vbt_c, z_c = _ffn(
            hc, mod_c, g_ffn1[l], *w1, layer=l, row0=0, tm=tm_ctx, proj=dict(proj, cos=ones, sin=ones, rope=False))

        tq = 512
        sink_l = jnp.repeat(sink[l].astype(F32) * LOG2E, tq).reshape(1, N_HEADS * tq)
        oa, ob = _attn_latent(qa, ka, vat, qb, kb, vbt, ka_c, vat_c, kb_c, vbt_c, sink_l,
                              tq=tq, tk=_key_tile(length + lc), n_sub=2)
        oc = _fourier_latent(z)
        h = _ffn(h, mod_l, g_ffn2[l], *w2, layer=l, row0=6, tm=tm, mix=(oa, ob, oc, wout),
                 final_g=g_final if last else None)

        if not last:
            sink_c = jnp.repeat(sink[l].astype(F32) * LOG2E, lc).reshape(1, N_HEADS * lc)
            oa_c = _attn_ctx(qa_c, ka_c, vat_c, sink_c, batch=batch)
            ob_c = _attn_ctx(qb_c, kb_c, vbt_c, None, batch=batch)
            oc_c = _fourier_ctx(z_c, batch=batch)
            hc = _ffn(hc, mod_c, g_ffn2[l], *w2, layer=l, row0=6, tm=tm_ctx, mix=(oa_c, ob_c, oc_c, wout))
    return h
```

```python
import functools

import jax
import jax.numpy as jnp
import numpy as np
from jax import lax
from jax.experimental import pallas as pl
from jax.experimental.pallas import tpu as pltpu

HEAD_DIM = 64
A_HEADS = 6
A_KV_HEADS = 2
B_HEADS = 6
B_KV_HEADS = 2
C_GROUPS = 4
C_GROUP_DIM = 64
GRID_W = 64
WINDOW = 128
BLOCK = 128
ROPE_THETA = 10000.0
NORM_EPS = 1e-6
N_MOD = 9
NEG_INF = -1e30

N_HEADS = 6
GROUP = 3
C_WIDTH = C_GROUPS * C_GROUP_DIM
FFT_N1 = 64
WIN_TILE = 256
DEN_ROWS = 16
LOG2E = 1.4426950408889634

LANE = 128
VMEM_LIMIT = 60 * 1024 * 1024

F32 = jnp.float32
BF16 = jnp.bfloat16


def _params(*sem):
    return pltpu.CompilerParams(dimension_semantics=tuple(sem), vmem_limit_bytes=VMEM_LIMIT)


def _resident(shape):
    nd = len(shape)
    return pl.BlockSpec(shape, lambda *_: (0,) * nd, pipeline_mode=pl.Buffered(1))


def _layer_resident(stacked, layer):
    nd = stacked.ndim
    return pl.BlockSpec((None,) + stacked.shape[1:], lambda *_: (layer,) + (0,) * (nd - 1),
                        pipeline_mode=pl.Buffered(1))


def _split3(x):
    hi = x.astype(BF16)
    lo = (x - hi.astype(F32)).astype(BF16)
    return hi, lo


def _dot(a, b):
    return jnp.dot(a, b, preferred_element_type=F32)


def _dot3(a_hi, a_lo, b_hi, b_lo):
    return _dot(a_hi, b_hi) + _dot(a_hi, b_lo) + _dot(a_lo, b_hi)


def _silu(x):
    return x / (1.0 + jnp.exp(-x))


def _norm_mod(x, g, shift, scale):
    ms = jnp.mean(x * x, axis=-1, keepdims=True)
    return (x * lax.rsqrt(ms + NORM_EPS)) * (g * (1.0 + scale)) + shift


def _ada_kernel(c_ref, w_ref, b_ref, o_ref):
    s_hi, s_lo = _split3(_silu(c_ref[...]))
    w_hi, w_lo = _split3(w_ref[0])
    o_ref[0] = _dot3(s_hi, s_lo, w_hi, w_lo) + b_ref[0]


def _ada(cvec, w_ada, b_ada):
    depth, d, nd = w_ada.shape
    tn = 1024
    return pl.pallas_call(
        _ada_kernel,
        grid=(depth, nd // tn),
        in_specs=[
            pl.BlockSpec((8, d), lambda l, j: (0, 0)),
            pl.BlockSpec((1, d, tn), lambda l, j: (l, 0, j)),
            pl.BlockSpec((1, 1, tn), lambda l, j: (l, 0, j)),
        ],
        out_specs=pl.BlockSpec((1, 8, tn), lambda l, j: (l, 0, j)),
        out_shape=jax.ShapeDtypeStruct((depth, 8, nd), F32),
        compiler_params=_params("parallel", "parallel"),
        name="ada",
    )(cvec, w_ada, b_ada.reshape(depth, 1, nd))


def _mix_out(oa_ref, ob_ref, oc_ref, w_ref):
    na = N_HEADS * HEAD_DIM
    return (_dot(oa_ref[0], w_ref[0:na, :]) + _dot(ob_ref[0], w_ref[na:2 * na, :])
            + _dot(oc_ref[0].astype(BF16), w_ref[2 * na:, :]))


N_PROJ_IN = 8


def _ffn_kernel(*refs, row0, final, mixed, rope, n_cast):
    refs = list(refs)
    if n_cast:
        cast_out = refs[-n_cast:]
        del refs[-n_cast:]
    x_ref, mod_ref, g_ref, wg_ref, wu_ref, wd_ref = refs[:6]
    del refs[:6]
    if mixed:
        oa_ref, ob_ref, oc_ref, wo_ref = refs[:4]
        del refs[:4]
    if final:
        gf_ref = refs.pop(0)
    proj_in = refs[:N_PROJ_IN] if rope is not None else []
    del refs[:len(proj_in)]
    cast_in = refs[:n_cast]
    o_ref = refs[n_cast]
    proj_out = refs[n_cast + 1:]
    shift = mod_ref[0, row0:row0 + 1, :]
    scale = mod_ref[0, row0 + 1:row0 + 2, :]
    gate = mod_ref[0, row0 + 2:row0 + 3, :]
    x = x_ref[0]
    if mixed:
        x = x + mod_ref[0, 5:6, :] * _mix_out(oa_ref, ob_ref, oc_ref, wo_ref)
    xm = _norm_mod(x, g_ref[...], shift, scale).astype(BF16)
    a = _dot(xm, wg_ref[...])
    u = _dot(xm, wu_ref[...])
    act = (_silu(a) * u).astype(BF16)
    y = x + (0.5 * gate) * _dot(act, wd_ref[...])
    if final:
        ms = jnp.mean(y * y, axis=-1, keepdims=True)
        y = (y * lax.rsqrt(ms + NORM_EPS)) * gf_ref[...]
    o_ref[0] = y
    if rope is not None:
        _proj_body(y, mod_ref, *proj_in, *proj_out, rope=rope)
    for src, dst in zip(cast_in, cast_out if n_cast else []):
        dst[0] = src[...].astype(BF16)


def _ffn(x, mod, g, wg, wu, wd, *, layer, row0, tm, mix=None, final_g=None, proj=None, cast=()):
    groups, t, d = x.shape
    final = final_g is not None
    tok = lambda b, i: (b, i, 0)
    in_specs = [
        pl.BlockSpec((1, tm, d), tok),
        pl.BlockSpec((1, N_MOD, d), lambda b, i: (b, 0, 0)),
        _resident((1, d)),
        _layer_resident(wg, layer),
        _layer_resident(wu, layer),
        _layer_resident(wd, layer),
    ]
    args = [x, mod, g.reshape(1, d), wg, wu, wd]
    if mix is not None:
        oa, ob, oc, w_out = mix
        in_specs += [pl.BlockSpec((1, tm, oa.shape[2]), tok), pl.BlockSpec((1, tm, ob.shape[2]), tok),
                     pl.BlockSpec((1, tm, oc.shape[2]), tok), _layer_resident(w_out, layer)]
        args += [oa, ob, oc, w_out]
    if final:
        in_specs.append(_resident((1, d)))
        args.append(final_g.reshape(1, d))
    out_specs = [pl.BlockSpec((1, tm, d), tok)]
    out_shape = [jax.ShapeDtypeStruct((groups, t, d), F32)]
    if proj is not None:
        in_specs += [
            _resident((1, d)),
            _layer_resident(proj["w_in"], layer),
            pl.BlockSpec((tm, LANE), lambda b, i: (i, 0)),
            pl.BlockSpec((tm, LANE), lambda b, i: (i, 0)),
            _resident((1, LANE)),
            _resident((1, LANE)),
            _resident((LANE, LANE)),
            _layer_resident(proj["wc"], layer),
        ]
        args += [proj["g"].reshape(1, d), proj["w_in"], proj["cos"], proj["sin"], proj["gq"], proj["gk"],
                 proj["msm"], proj["wc"]]
        q_spec = pl.BlockSpec((1, N_HEADS, tm, LANE), lambda b, i: (b, 0, i, 0))
        k_spec = pl.BlockSpec((1, tm, LANE), tok)
        vt_spec = pl.BlockSpec((1, LANE, tm), lambda b, i: (b, 0, i))
        q_shape = jax.ShapeDtypeStruct((groups, N_HEADS, t, LANE), BF16)
        k_shape = jax.ShapeDtypeStruct((groups, t, LANE), BF16)
        vt_shape = jax.ShapeDtypeStruct((groups, LANE, t), BF16)
        out_specs += [q_spec, k_spec, vt_spec, q_spec, k_spec, vt_spec, pl.BlockSpec((1, tm, 2 * C_WIDTH), tok)]
        out_shape += [q_shape, k_shape, vt_shape, q_shape, k_shape, vt_shape,
                      jax.ShapeDtypeStruct((groups, t, 2 * C_WIDTH), F32)]
    n_tiles = t // tm
    for w in cast:
        _, r, c = w.shape
        if r % (groups * n_tiles * 16) == 0:
            block, index = (r // (groups * n_tiles), c), lambda b, i: (b * n_tiles + i, 0)
        else:
            block, index = (r // n_tiles, c // groups), lambda b, i: (i, b)
        assert block[0] % 16 == 0 and block[1] % LANE == 0
        in_specs.append(pl.BlockSpec((None,) + block, lambda b, i, index=index: (layer,) + index(b, i)))
        args.append(w)
        out_specs.append(pl.BlockSpec((1,) + block, lambda b, i, index=index: (0,) + index(b, i)))
        out_shape.append(jax.ShapeDtypeStruct((1, r, c), BF16))
    out = pl.pallas_call(
        functools.partial(_ffn_kernel, row0=row0, final=final, mixed=mix is not None,
                          rope=None if proj is None else proj["rope"], n_cast=len(cast)),
        grid=(groups, t // tm),
        in_specs=in_specs,
        out_specs=out_specs,
        out_shape=out_shape,
        compiler_params=_params("parallel", "parallel"),
        name="ffn",
    )(*args)
    return out[0] if len(out) == 1 else out


def _rope_slab(t, cos, sin, lane):
    fwd = pltpu.roll(t, LANE - 16, axis=1)
    bwd = pltpu.roll(t, 16, axis=1)
    partner = jnp.where((lane % 32) < 16, fwd, bwd)
    return t * cos + partner * sin


def _wfold_kernel(cs_hi_ref, cs_lo_ref, w_ref, o_ref):
    w_hi, w_lo = _split3(w_ref[0])
    for half in range(2):
        sl = slice(half * C_WIDTH, (half + 1) * C_WIDTH)
        o_ref[0, :, sl] = _dot3(cs_hi_ref[:, sl], cs_lo_ref[:, sl], w_hi, w_lo).astype(BF16)


def _wfold(cs_hi, cs_lo, wbd):
    depth = wbd.shape[0]
    return pl.pallas_call(
        _wfold_kernel,
        grid=(depth,),
        in_specs=[_resident(cs_hi.shape), _resident(cs_lo.shape),
                  pl.BlockSpec((1, C_WIDTH, C_WIDTH), lambda l: (l, 0, 0))],
        out_specs=pl.BlockSpec((1, C_WIDTH, 2 * C_WIDTH), lambda l: (l, 0, 0)),
        out_shape=jax.ShapeDtypeStruct((depth, C_WIDTH, 2 * C_WIDTH), BF16),
        compiler_params=_params("parallel"),
        name="wfold",
    )(cs_hi, cs_lo, wbd)


def _proj_body(x, mod_ref, g_ref, w_ref, cos_ref, sin_ref, gq_ref, gk_ref, msm_ref, wc_ref,
               qa_ref, ka_ref, vat_ref, qb_ref, kb_ref, vbt_ref, z_ref, *, rope):
    tm = x.shape[0]
    shift = mod_ref[0, 3:4, :]
    scale = mod_ref[0, 4:5, :]
    xm = _norm_mod(x, g_ref[...], shift, scale).astype(BF16)

    u_all = _dot(xm, w_ref[...])

    def u(lo, hi):
        return u_all[:, lo:hi]

    lane = lax.broadcasted_iota(jnp.int32, (tm, LANE), 1)
    low = lane < HEAD_DIM
    if rope:
        cos = cos_ref[...]
        sin = sin_ref[...]

    def qk_norm(t, gain):
        ms = _dot((t * t).astype(BF16), msm_ref[...])
        return t * lax.rsqrt(ms + NORM_EPS) * gain

    def finish(t):
        return _rope_slab(t, cos, sin, lane) if rope else t

    def emit_q(q_ref, slabs):
        for h in range(N_HEADS):
            t = slabs[h // 2] * (LOG2E * HEAD_DIM ** -0.5)
            if (h % 2) != (h // GROUP):
                t = pltpu.roll(t, HEAD_DIM, axis=1)
            keep = low if (h // GROUP) == 0 else jnp.logical_not(low)
            q_ref[0, h] = jnp.where(keep, t, 0.0).astype(BF16)

    qa = [finish(u(LANE * i, LANE * (i + 1))) for i in range(3)]
    emit_q(qa_ref, qa)
    ka_ref[0] = finish(u(384, 512)).astype(BF16)
    vat_ref[0] = u(512, 640).T.astype(BF16)

    qb = [finish(qk_norm(u(640 + LANE * i, 640 + LANE * (i + 1)), gq_ref[...])) for i in range(3)]
    emit_q(qb_ref, qb)
    kb_ref[0] = finish(qk_norm(u(1024, 1152), gk_ref[...])).astype(BF16)
    vbt_ref[0] = u(1152, 1280).T.astype(BF16)

    z_ref[0] = _dot(u(1280, 1536).astype(BF16), wc_ref[...])


def _nt(a, b):
    return lax.dot_general(a, b, (((1,), (1,)), ((), ())), preferred_element_type=F32)


def _store_heads(o_ref, ot, r0=0):
    tq = ot[0].shape[1]
    for pair in range(N_HEADS // 2):
        slab = jnp.concatenate([ot[2 * pair], ot[2 * pair + 1]], axis=0)
        o_ref[0, r0:r0 + tq, pair * LANE:(pair + 1) * LANE] = slab.T.astype(o_ref.dtype)


def _attn_latent_kernel(*refs, tq, tk):
    s_a, s_b, p_a, p_b = refs[13:17]
    n_sub = refs[1].shape[2] // tq
    tiles = [_tile_stages(sub, n_sub, *refs, tq=tq, tk=tk) for sub in range(n_sub)]
    nk = tiles[0]["nk"]
    n_steps = nk + 1
    total = n_sub * n_steps
    bufs = ((s_a, p_a), (s_b, p_b))

    def issue_scores(u):
        j, t = divmod(u, n_steps)
        s_buf, slot = bufs[u % 2][0], u % 2
        if t == 0:
            tiles[j]["scores_win"](s_buf, slot)
        else:
            tiles[j]["scores"](t - 1, s_buf, slot)

    def do_softmax(u):
        j, t = divmod(u, n_steps)
        (s_buf, p_buf), slot = bufs[u % 2], u % 2
        tiles[j]["softmax_win" if t == 0 else "softmax"](s_buf, p_buf, slot)

    def do_values(u):
        j, t = divmod(u, n_steps)
        p_buf, slot = bufs[u % 2][1], u % 2
        if t == 0:
            tiles[j]["values_win"](p_buf, slot)
        else:
            tiles[j]["values"](t - 1, p_buf, slot)
        if t == n_steps - 1:
            tiles[j]["finish"]()

    def flat_step(u):
        if u + 1 < total:
            issue_scores(u + 1)
        do_softmax(u)
        if u >= 1:
            do_values(u - 1)

    n_pairs = max(0, (nk - 3) // 2)
    issue_scores(0)
    for j in range(n_sub):
        base = j * n_steps
        flat_step(base)
        flat_step(base + 1)

        def pair(i, carry, tile=tiles[j], base=base):
            t = 2 + 2 * i
            for dt in (0, 1):
                par = (base + dt) % 2
                (s_cur, p_cur), (s_oth, p_oth) = bufs[par], bufs[1 - par]
                tile["scores"](t + dt, s_oth, 1 - par)
                tile["softmax"](s_cur, p_cur, par)
                tile["values"](t + dt - 2, p_oth, 1 - par)
            return carry

        lax.fori_loop(0, n_pairs, pair, 0)
        for t in range(2 + 2 * n_pairs, n_steps):
            flat_step(base + t)
    do_values(total - 1)


def _tile_stages(sub, n_sub, qa_ref, qb_ref, ka_ref, vat_ref, kb_ref, vbt_ref, kac_ref, vatc_ref, kbc_ref,
                 vbtc_ref, sink_ref, oa_ref, ob_ref, s_a, s_b, p_a, p_b, m_sc, bmax_sc, alpha_sc, acc_sc, *, tq, tk):
    length = kb_ref.shape[1]
    lc = kbc_ref.shape[1]
    nk = (length + lc) // tk
    tail = length - (nk - 1) * tk
    wq = min(tq, WIN_TILE)
    span = wq + 2 * WINDOW
    na = span + lc
    tile = pl.program_id(1) * n_sub + sub
    r0 = sub * tq
    m_sc, acc_sc = m_sc.at[sub], acc_sc.at[sub]
    starts = [pl.multiple_of(jnp.clip(tile * tq + i * wq - WINDOW, 0, length - span), LANE)
              for i in range(tq // wq)]
    m_sc[...] = jnp.full_like(m_sc, NEG_INF)
    acc_sc[...] = jnp.zeros_like(acc_sc)

    def with_ones(vt):
        return jnp.concatenate([vt, jnp.ones((DEN_ROWS, vt.shape[1]), BF16)], axis=0)

    def rows(g):
        return slice(g * HEAD_DIM, (g + 1) * HEAD_DIM)

    def sub_cols(h, i):
        return slice(h * tq + i * wq, h * tq + (i + 1) * wq)

    def scores_win(s_buf, slot):
        for i, start in enumerate(starts):
            kw = ka_ref[0, pl.ds(start, span), :]
            kpos = start + lax.broadcasted_iota(jnp.int32, (span, wq), 0)
            qpos = tile * tq + i * wq + lax.broadcasted_iota(jnp.int32, (span, wq), 1)
            valid = jnp.abs(kpos - qpos) <= WINDOW
            for h in range(N_HEADS):
                cs = sub_cols(h, i)
                q = qa_ref[0, h, r0 + i * wq:r0 + (i + 1) * wq, :]
                s_w = jnp.where(valid, _nt(kw, q), NEG_INF)
                s_c = _nt(kac_ref[0], q)
                s_buf[0:span, cs] = s_w
                s_buf[span:na, cs] = s_c
                bmax_sc[slot, :, cs] = jnp.maximum(jnp.max(s_w, axis=0, keepdims=True),
                                                   jnp.max(s_c, axis=0, keepdims=True))

    def softmax_win(s_buf, p_buf, slot):
        for h in range(N_HEADS):
            hc = head_cols(h)
            sink = sink_ref[:, hc]
            m = jnp.maximum(sink, bmax_sc[slot, :, hc])
            alpha_sc[slot, :, hc] = jnp.exp2(sink - m)
            p_buf[0:na, hc] = jnp.exp2(s_buf[0:na, hc] - m).astype(BF16)

    def values_win(p_buf, slot):
        parts = [[None] * len(starts) for _ in range(N_HEADS)]
        for g in range(2):
            for i, start in enumerate(starts):
                vt = with_ones(jnp.concatenate(
                    [vat_ref[0, rows(g), pl.ds(start, span)], vatc_ref[0, rows(g), :]], axis=1))
                for h in range(g * GROUP, (g + 1) * GROUP):
                    cs = sub_cols(h, i)
                    acc = _dot(vt, p_buf[0:na, cs])
                    parts[h][i] = acc[0:HEAD_DIM] / (acc[HEAD_DIM:HEAD_DIM + 1] + alpha_sc[slot, :, cs])
        _store_heads(oa_ref, [jnp.concatenate(p, axis=1) for p in parts], r0)

    def key_block(blk):
        if isinstance(blk, int) and blk == nk - 1:
            return jnp.concatenate([kb_ref[0, length - tail:, :], kbc_ref[0]], axis=0)
        return kb_ref[0, pl.ds(pl.multiple_of(blk * tk, tk), tk), :]

    def value_block(blk, g):
        if isinstance(blk, int) and blk == nk - 1:
            return jnp.concatenate([vbt_ref[0, rows(g), length - tail:], vbtc_ref[0, rows(g), :]], axis=1)
        return vbt_ref[0, rows(g), pl.ds(pl.multiple_of(blk * tk, tk), tk)]

    def head_cols(h):
        return slice(h * tq, (h + 1) * tq)

    def scores(blk, s_buf, slot):
        kb = key_block(blk)
        for h in range(N_HEADS):
            s = _nt(kb, qb_ref[0, h, r0:r0 + tq, :])
            s_buf[0:tk, head_cols(h)] = s
            bmax_sc[slot, :, head_cols(h)] = jnp.max(s, axis=0, keepdims=True)

    def softmax(s_buf, p_buf, slot):
        for h in range(N_HEADS):
            hc = head_cols(h)
            m_old = m_sc[:, hc]
            m_new = jnp.maximum(m_old, bmax_sc[slot, :, hc])
            m_sc[:, hc] = m_new
            alpha_sc[slot, :, hc] = jnp.exp2(m_old - m_new)
            p_buf[0:tk, hc] = jnp.exp2(s_buf[0:tk, hc] - m_new).astype(BF16)

    def values(blk, p_buf, slot):
        for g in range(2):
            vt = with_ones(value_block(blk, g))
            for h in range(g * GROUP, (g + 1) * GROUP):
                hc = head_cols(h)
                acc_sc[h] = acc_sc[h] * alpha_sc[slot, :, hc] + _dot(vt, p_buf[0:tk, hc])

    def finish():
        _store_heads(ob_ref, [acc_sc[h, 0:HEAD_DIM] / acc_sc[h, HEAD_DIM:HEAD_DIM + 1] for h in range(N_HEADS)], r0)

    return dict(nk=nk, scores_win=scores_win, softmax_win=softmax_win, values_win=values_win,
                scores=scores, softmax=softmax, values=values, finish=finish)


def _key_tile(n_keys, cap=1024):
    return max(t for t in range(LANE, cap + 1, LANE) if n_keys % t == 0)


def _attn_latent(qa, ka, vat, qb, kb, vbt, ka_c, vat_c, kb_c, vbt_c, sink_row, *, tq, tk, n_sub):
    b, _, l, _ = qb.shape
    lc = ka_c.shape[1] // b
    assert (l + lc) % tk == 0 and (l + lc) // tk * tk - tk <= l and tk > lc
    buf_rows = max(tk, min(tq, WIN_TILE) + 2 * WINDOW + lc)
    buf_cols = N_HEADS * tq
    q_spec = pl.BlockSpec((1, N_HEADS, n_sub * tq, LANE), lambda b, i: (b, 0, i, 0))
    k_spec = pl.BlockSpec((1, l, LANE), lambda b, i: (b, 0, 0), pipeline_mode=pl.Buffered(1))
    vt_spec = pl.BlockSpec((1, LANE, l), lambda b, i: (b, 0, 0), pipeline_mode=pl.Buffered(1))
    kc_spec = pl.BlockSpec((1, lc, LANE), lambda b, i: (0, b, 0), pipeline_mode=pl.Buffered(1))
    vtc_spec = pl.BlockSpec((1, LANE, lc), lambda b, i: (0, 0, b), pipeline_mode=pl.Buffered(1))
    o_spec = pl.BlockSpec((1, n_sub * tq, N_HEADS * HEAD_DIM), lambda b, i: (b, i, 0))
    o_shape = jax.ShapeDtypeStruct((b, l, N_HEADS * HEAD_DIM), BF16)
    return pl.pallas_call(
        functools.partial(_attn_latent_kernel, tq=tq, tk=tk),
        grid=(b, l // (n_sub * tq)),
        in_specs=[q_spec, q_spec, k_spec, vt_spec, k_spec, vt_spec, kc_spec, vtc_spec, kc_spec, vtc_spec,
                  pl.BlockSpec((1, N_HEADS * tq), lambda b, i: (0, 0))],
        out_specs=[o_spec, o_spec],
        out_shape=[o_shape, o_shape],
        scratch_shapes=[
            pltpu.VMEM((buf_rows, buf_cols), F32),
            pltpu.VMEM((buf_rows, buf_cols), F32),
            pltpu.VMEM((buf_rows, buf_cols), BF16),
            pltpu.VMEM((buf_rows, buf_cols), BF16),
            pltpu.VMEM((n_sub, 1, N_HEADS * tq), F32),
            pltpu.VMEM((2, 1, N_HEADS * tq), F32),
            pltpu.VMEM((2, 1, N_HEADS * tq), F32),
            pltpu.VMEM((n_sub, N_HEADS, HEAD_DIM + DEN_ROWS, tq), F32),
        ],
        compiler_params=_params("parallel", "parallel"),
        name="attn_latent",
    )(qa, qb, ka, vat, kb, vbt, ka_c, vat_c, kb_c, vbt_c, sink_row)


def _softmax_pv(q_ref, ks, vts, extra, mask=None):
    tq = q_ref.shape[2]
    aug = [[jnp.concatenate([vt[g * HEAD_DIM:(g + 1) * HEAD_DIM, :],
                             jnp.ones((DEN_ROWS, vt.shape[1]), BF16)], axis=0) for vt in vts]
           for g in range(2)]
    out = []
    for h in range(N_HEADS):
        ss = [_nt(k, q_ref[0, h]) for k in ks]
        if mask is not None:
            ss[0] = jnp.where(mask, ss[0], NEG_INF)
        m = functools.reduce(jnp.maximum, [jnp.max(s, axis=0, keepdims=True) for s in ss])
        if extra is not None:
            e = extra[:, h * tq:(h + 1) * tq]
            m = jnp.maximum(m, e)
        acc = functools.reduce(
            jnp.add, [_dot(vt, jnp.exp2(s - m).astype(BF16)) for s, vt in zip(ss, aug[h // GROUP])])
        den = acc[HEAD_DIM:HEAD_DIM + 1]
        if extra is not None:
            den = den + jnp.exp2(e - m)
        out.append(acc[0:HEAD_DIM] / den)
    return out


def _attn_ctx_kernel(q_ref, k_ref, vt_ref, *rest, has_sink):
    if has_sink:
        sink_ref, o_ref = rest
    else:
        (o_ref,) = rest
    extra = sink_ref[...] if has_sink else None
    _store_heads(o_ref, _softmax_pv(q_ref, [k_ref[0]], [vt_ref[0]], extra))


def _attn_ctx(q, k, vt, sink_row, *, batch):
    lc = q.shape[2] // batch
    in_specs = [
        pl.BlockSpec((1, N_HEADS, lc, LANE), lambda b: (0, 0, b, 0)),
        pl.BlockSpec((1, lc, LANE), lambda b: (0, b, 0)),
        pl.BlockSpec((1, LANE, lc), lambda b: (0, 0, b)),
    ]
    args = [q, k, vt]
    if sink_row is not None:
        in_specs.append(pl.BlockSpec((1, N_HEADS * lc), lambda b: (0, 0)))
        args.append(sink_row)
    return pl.pallas_call(
        functools.partial(_attn_ctx_kernel, has_sink=sink_row is not None),
        grid=(batch,),
        in_specs=in_specs,
        out_specs=pl.BlockSpec((1, lc, N_HEADS * HEAD_DIM), lambda b: (0, b, 0)),
        out_shape=jax.ShapeDtypeStruct((1, batch * lc, N_HEADS * HEAD_DIM), BF16),
        compiler_params=_params("parallel"),
        name="attn_ctx",
    )(*args)


def _stack3_lhs(hi, lo):
    return jnp.concatenate([hi, hi, lo], axis=1)


def _stack3_rhs(x):
    hi, lo = _split3(x)
    return jnp.concatenate([hi, lo, hi], axis=0)


def _fft1_kernel(x_ref, t_ref, twc_ref, tws_ref, o_ref):
    tn2 = x_ref.shape[2]
    n1 = x_ref.shape[1]
    xt = jnp.transpose(x_ref[0], (1, 0, 2))
    y = _dot(t_ref[...], _stack3_rhs(jnp.concatenate([xt[i] for i in range(tn2)], axis=1)))
    cx, sx = y[0:n1], y[n1:]
    w = 2 * C_WIDTH
    outs = []
    for i in range(tn2):
        tc = jnp.concatenate([twc_ref[i]] * (C_WIDTH // LANE), axis=1)
        ts = jnp.concatenate([tws_ref[i]] * (C_WIDTH // LANE), axis=1)
        re, im = slice(i * w, i * w + C_WIDTH), slice(i * w + C_WIDTH, (i + 1) * w)
        ar = cx[:, re] + sx[:, im]
        ai = cx[:, im] - sx[:, re]
        outs.append(jnp.concatenate([ar * tc + ai * ts, ai * tc - ar * ts], axis=1))
    o_ref[0] = jnp.transpose(jnp.stack(outs, axis=0), (1, 0, 2))


def _real_dft(t_ref, x):
    rhs = jnp.concatenate([_stack3_rhs(x[:, 0:C_WIDTH]), _stack3_rhs(x[:, C_WIDTH:])], axis=0)
    return _dot(t_ref[...], rhs)


def _fft2_kernel(x_ref, t_ref, o_ref):
    outs = [_real_dft(t_ref, x_ref[0, i]) for i in range(x_ref.shape[1])]
    o_ref[0] = jnp.transpose(jnp.stack(outs, axis=0), (1, 0, 2))


def _dft_tables(n, norm):
    k = np.arange(n, dtype=np.int64)
    ang = 2.0 * np.pi * ((k[:, None] * k[None, :]) % n).astype(np.float64) / n
    c = jnp.asarray(np.cos(ang) * norm, F32)
    s = jnp.asarray(np.sin(ang) * norm, F32)
    return _stack3_lhs(*_split3(c)), _stack3_lhs(*_split3(s))


def _fourier_latent(z):
    b, l, w = z.shape
    n1, n2 = FFT_N1, l // FFT_N1
    tn2, tk1 = 16, 16
    prod = jnp.arange(n2)[:, None] * jnp.arange(n1)[None, :]
    ang = prod.astype(F32) * (2.0 * np.pi / l)
    twc = jnp.broadcast_to(jnp.cos(ang)[:, :, None], (n2, n1, LANE))
    tws = jnp.broadcast_to(jnp.sin(ang)[:, :, None], (n2, n1, LANE))
    t1 = jnp.concatenate(_dft_tables(n1, 1.0), axis=0)
    y = pl.pallas_call(
        _fft1_kernel,
        grid=(b, n2 // tn2),
        in_specs=[pl.BlockSpec((1, n1, tn2, w), lambda b, j: (b, 0, j, 0))]
        + [_resident(t1.shape)]
        + [pl.BlockSpec((tn2, n1, LANE), lambda b, j: (j, 0, 0))] * 2,
        out_specs=pl.BlockSpec((1, n1, tn2, w), lambda b, j: (b, 0, j, 0)),
        out_shape=jax.ShapeDtypeStruct((b, n1, n2, w), F32),
        compiler_params=_params("parallel", "parallel"),
        name="fft_stage1",
    )(z.reshape(b, n1, n2, w), t1, twc, tws)
    t2m = jnp.concatenate(_dft_tables(n2, 1.0 / np.sqrt(l)), axis=1)
    out = pl.pallas_call(
        _fft2_kernel,
        grid=(b, n1 // tk1),
        in_specs=[pl.BlockSpec((1, tk1, n2, w), lambda b, j: (b, j, 0, 0)), _resident(t2m.shape)],
        out_specs=pl.BlockSpec((1, n2, tk1, C_WIDTH), lambda b, j: (b, 0, j, 0)),
        out_shape=jax.ShapeDtypeStruct((b, n2, n1, C_WIDTH), F32),
        compiler_params=_params("parallel", "parallel"),
        name="fft_stage2",
    )(y, t2m)
    return out.reshape(b, l, C_WIDTH)


def _fourier_ctx_kernel(x_ref, t_ref, o_ref):
    o_ref[0] = _real_dft(t_ref, x_ref[0])


def _fourier_ctx(z, *, batch):
    lc = z.shape[1] // batch
    tabs = jnp.concatenate(_dft_tables(lc, 1.0 / np.sqrt(lc)), axis=1)
    return pl.pallas_call(
        _fourier_ctx_kernel,
        grid=(batch,),
        in_specs=[pl.BlockSpec((1, lc, 2 * C_WIDTH), lambda b: (0, b, 0)), _resident(tabs.shape)],
        out_specs=pl.BlockSpec((1, lc, C_WIDTH), lambda b: (0, b, 0)),
        out_shape=jax.ShapeDtypeStruct((1, batch * lc, C_WIDTH), F32),
        compiler_params=_params("parallel"),
        name="fourier_ctx",
    )(z, tabs)


def _rope_tables(length):
    t = jnp.arange(length)
    row = (t // GRID_W).astype(F32)
    col = (t % GRID_W).astype(F32)
    half = HEAD_DIM // 2
    inv = ROPE_THETA ** (-jnp.arange(0, half, 2, dtype=F32) / half)
    ar, ac = row[:, None] * inv, col[:, None] * inv
    cos = jnp.concatenate([jnp.cos(ar), jnp.cos(ar), jnp.cos(ac), jnp.cos(ac)], axis=-1)
    sin = jnp.concatenate([-jnp.sin(ar), jnp.sin(ar), -jnp.sin(ac), jnp.sin(ac)], axis=-1)
    return jnp.tile(cos, (1, 2)), jnp.tile(sin, (1, 2))


def _channel_dft():
    k = np.arange(C_GROUP_DIM)
    ang = 2.0 * np.pi * ((k[:, None] * k[None, :]) % C_GROUP_DIM) / C_GROUP_DIM
    eye = np.eye(C_GROUPS)
    norm = 1.0 / np.sqrt(C_GROUP_DIM)
    cs = np.concatenate([np.kron(eye, np.cos(ang) * norm), np.kron(eye, -np.sin(ang) * norm)], axis=1)
    return _split3(jnp.asarray(cs, F32))


def _block_diag(w):
    g, c, _ = w.shape
    eye = jnp.eye(g, dtype=w.dtype)
    return (eye[:, None, :, None] * w[:, :, None, :]).reshape(g * c, g * c)


def kernel(x, c, ctx, c_ctx, w_ada, b_ada, g_ffn1, g_mix, g_ffn2, w_in, g_qn, g_kn, sink, w_four,
           w_out, w1_gate, w1_up, w1_down, w2_gate, w2_up, w2_down, g_final):
    batch, length, d = x.shape
    lc = ctx.shape[1]
    depth = w_ada.shape[0]
    assert length % (FFT_N1 * 8) == 0 and length >= BLOCK + 2 * WINDOW and batch + 1 <= 8
    tm = 512
    tm_ctx = batch * lc

    cvec = jnp.zeros((8, d), F32).at[:batch].set(c).at[batch].set(c_ctx)
    mod = _ada(cvec, w_ada, b_ada).reshape(depth, 8, N_MOD, d)

    cos, sin = _rope_tables(length)
    ones = jnp.ones((tm_ctx, LANE), F32)
    msm = _block_diag(jnp.full((2, HEAD_DIM, HEAD_DIM), 1.0 / HEAD_DIM, F32)).astype(BF16)
    wc = _wfold(*_channel_dft(), jax.vmap(_block_diag)(w_four))

    w1 = tuple(w.astype(BF16) for w in (w1_gate, w1_up, w1_down))
    win = w_in.astype(BF16)

    h = x
    hc = ctx.reshape(1, batch * lc, d)
    for l in range(depth):
        last = l == depth - 1
        mod_l, mod_c = mod[l, :batch], mod[l, batch:batch + 1]
        gq = jnp.tile(g_qn[l].reshape(1, HEAD_DIM), (1, 2))
        gk = jnp.tile(g_kn[l].reshape(1, HEAD_DIM), (1, 2))
        proj = dict(g=g_mix[l], w_in=win, gq=gq, gk=gk, msm=msm, wc=wc)

        h, qa, ka, vat, qb, kb, vbt, z, *w2, wout = _ffn(
            h, mod_l, g_ffn1[l], *w1, layer=l, row0=0, tm=tm, proj=dict(proj, cos=cos, sin=sin, rope=True),
            cast=(w2_gate, w2_up, w2_down, w_out))
        hc, qa_c, ka_c, vat_c, qb_c, kb_c, vbt_c, z_c = _ffn(
            hc, mod_c, g_ffn1[l], *w1, layer=l, row0=0, tm=tm_ctx, proj=dict(proj, cos=ones, sin=ones, rope=False))

        tq = 512
        sink_l = jnp.repeat(sink[l].astype(F32) * LOG2E, tq).reshape(1, N_HEADS * tq)
        oa, ob = _attn_latent(qa, ka, vat, qb, kb, vbt, ka_c, vat_c, kb_c, vbt_c, sink_l,
                              tq=tq, tk=_key_tile(length + lc), n_sub=1)
        oc = _fourier_latent(z)
        h = _ffn(h, mod_l, g_ffn2[l], *w2, layer=0, row0=6, tm=tm, mix=(oa, ob, oc, wout),
                 final_g=g_final if last else None)

        if not last:
            sink_c = jnp.repeat(sink[l].astype(F32) * LOG2E, lc).reshape(1, N_HEADS * lc)
            oa_c = _attn_ctx(qa_c, ka_c, vat_c, sink_c, batch=batch)
            ob_c = _attn_ctx(qb_c, kb_c, vbt_c, None, batch=batch)
            oc_c = _fourier_ctx(z_c, batch=batch)
            hc = _ffn(hc, mod_c, g_ffn2[l], *w2, layer=0, row0=6, tm=tm_ctx, mix=(oa_c, ob_c, oc_c, wout))
    return h
```

```python
import functools

import jax
import jax.numpy as jnp
import numpy as np
from jax import lax
from jax.experimental import pallas as pl
from jax.experimental.pallas import tpu as pltpu

HEAD_DIM = 64
A_HEADS = 6
A_KV_HEADS = 2
B_HEADS = 6
B_KV_HEADS = 2
C_GROUPS = 4
C_GROUP_DIM = 64
GRID_W = 64
WINDOW = 128
BLOCK = 128
ROPE_THETA = 10000.0
NORM_EPS = 1e-6
N_MOD = 9
NEG_INF = -1e30

N_HEADS = 6
GROUP = 3
C_WIDTH = C_GROUPS * C_GROUP_DIM
FFT_N1 = 64
WIN_TILE = 256
DEN_ROWS = 16
LOG2E = 1.4426950408889634

LANE = 128
VMEM_LIMIT = 60 * 1024 * 1024

F32 = jnp.float32
BF16 = jnp.bfloat16


def _params(*sem):
    return pltpu.CompilerParams(dimension_semantics=tuple(sem), vmem_limit_bytes=VMEM_LIMIT)


def _resident(shape):
    nd = len(shape)
    return pl.BlockSpec(shape, lambda *_: (0,) * nd, pipeline_mode=pl.Buffered(1))


def _layer_resident(stacked, layer):
    nd = stacked.ndim
    return pl.BlockSpec((None,) + stacked.shape[1:], lambda *_: (layer,) + (0,) * (nd - 1),
                        pipeline_mode=pl.Buffered(1))


def _split3(x):
    hi = x.astype(BF16)
    lo = (x - hi.astype(F32)).astype(BF16)
    return hi, lo


def _dot(a, b):
    return jnp.dot(a, b, preferred_element_type=F32)


def _dot3(a_hi, a_lo, b_hi, b_lo):
    return _dot(a_hi, b_hi) + _dot(a_hi, b_lo) + _dot(a_lo, b_hi)


def _silu(x):
    return x / (1.0 + jnp.exp(-x))


def _norm_mod(x, g, shift, scale):
    ms = jnp.mean(x * x, axis=-1, keepdims=True)
    return (x * lax.rsqrt(ms + NORM_EPS)) * (g * (1.0 + scale)) + shift


def _ada_kernel(c_ref, w_ref, b_ref, o_ref):
    s_hi, s_lo = _split3(_silu(c_ref[...]))
    w_hi, w_lo = _split3(w_ref[0])
    o_ref[0] = _dot3(s_hi, s_lo, w_hi, w_lo) + b_ref[0]


def _ada(cvec, w_ada, b_ada):
    depth, d, nd = w_ada.shape
    tn = 1024
    return pl.pallas_call(
        _ada_kernel,
        grid=(depth, nd // tn),
        in_specs=[
            pl.BlockSpec((8, d), lambda l, j: (0, 0)),
            pl.BlockSpec((1, d, tn), lambda l, j: (l, 0, j)),
            pl.BlockSpec((1, 1, tn), lambda l, j: (l, 0, j)),
        ],
        out_specs=pl.BlockSpec((1, 8, tn), lambda l, j: (l, 0, j)),
        out_shape=jax.ShapeDtypeStruct((depth, 8, nd), F32),
        compiler_params=_params("parallel", "parallel"),
        name="ada",
    )(cvec, w_ada, b_ada.reshape(depth, 1, nd))


def _mix_out(oa_ref, ob_ref, oc_ref, w_ref):
    na = N_HEADS * HEAD_DIM
    return (_dot(oa_ref[0], w_ref[0:na, :]) + _dot(ob_ref[0], w_ref[na:2 * na, :])
            + _dot(oc_ref[0].astype(BF16), w_ref[2 * na:, :]))


N_PROJ_IN = 8


def _ffn_kernel(*refs, row0, final, mixed, rope, n_cast):
    refs = list(refs)
    if n_cast:
        cast_out = refs[-n_cast:]
        del refs[-n_cast:]
    x_ref, mod_ref, g_ref, wg_ref, wu_ref, wd_ref = refs[:6]
    del refs[:6]
    if mixed:
        oa_ref, ob_ref, oc_ref, wo_ref = refs[:4]
        del refs[:4]
    if final:
        gf_ref = refs.pop(0)
    proj_in = refs[:N_PROJ_IN] if rope is not None else []
    del refs[:len(proj_in)]
    cast_in = refs[:n_cast]
    o_ref = refs[n_cast]
    proj_out = refs[n_cast + 1:]
    shift = mod_ref[0, row0:row0 + 1, :]
    scale = mod_ref[0, row0 + 1:row0 + 2, :]
    gate = mod_ref[0, row0 + 2:row0 + 3, :]
    x = x_ref[0]
    if mixed:
        x = x + mod_ref[0, 5:6, :] * _mix_out(oa_ref, ob_ref, oc_ref, wo_ref)
    xm = _norm_mod(x, g_ref[...], shift, scale).astype(BF16)
    a = _dot(xm, wg_ref[...])
    u = _dot(xm, wu_ref[...])
    act = (_silu(a) * u).astype(BF16)
    y = x + (0.5 * gate) * _dot(act, wd_ref[...])
    if final:
        ms = jnp.mean(y * y, axis=-1, keepdims=True)
        y = (y * lax.rsqrt(ms + NORM_EPS)) * gf_ref[...]
    o_ref[0] = y
    if rope is not None:
        _proj_body(y, mod_ref, *proj_in, *proj_out, rope=rope)
    for src, dst in zip(cast_in, cast_out if n_cast else []):
        dst[0] = src[...].astype(BF16)


def _ffn(x, mod, g, wg, wu, wd, *, row0, tm, mix=None, final_g=None, proj=None, cast=None):
    cast_layer, cast = cast if cast is not None else (0, ())
    groups, t, d = x.shape
    final = final_g is not None
    tok = lambda b, i: (b, i, 0)
    in_specs = [
        pl.BlockSpec((1, tm, d), tok),
        pl.BlockSpec((1, N_MOD, d), lambda b, i: (b, 0, 0)),
        _resident((1, d)),
        _layer_resident(wg, 0),
        _layer_resident(wu, 0),
        _layer_resident(wd, 0),
    ]
    args = [x, mod, g.reshape(1, d), wg, wu, wd]
    if mix is not None:
        oa, ob, oc, w_out = mix
        in_specs += [pl.BlockSpec((1, tm, oa.shape[2]), tok), pl.BlockSpec((1, tm, ob.shape[2]), tok),
                     pl.BlockSpec((1, tm, oc.shape[2]), tok), _layer_resident(w_out, 0)]
        args += [oa, ob, oc, w_out]
    if final:
        in_specs.append(_resident((1, d)))
        args.append(final_g.reshape(1, d))
    out_specs = [pl.BlockSpec((1, tm, d), tok)]
    out_shape = [jax.ShapeDtypeStruct((groups, t, d), F32)]
    if proj is not None:
        in_specs += [
            _resident((1, d)),
            _layer_resident(proj["w_in"], 0),
            pl.BlockSpec((tm, LANE), lambda b, i: (i, 0)),
            pl.BlockSpec((tm, LANE), lambda b, i: (i, 0)),
            _resident((1, LANE)),
            _resident((1, LANE)),
            _resident((LANE, LANE)),
            _layer_resident(proj["wc"], 0),
        ]
        args += [proj["g"].reshape(1, d), proj["w_in"], proj["cos"], proj["sin"], proj["gq"], proj["gk"],
                 proj["msm"], proj["wc"]]
        q_spec = pl.BlockSpec((1, N_HEADS, tm, LANE), lambda b, i: (b, 0, i, 0))
        k_spec = pl.BlockSpec((1, tm, LANE), tok)
        vt_spec = pl.BlockSpec((1, LANE, tm), lambda b, i: (b, 0, i))
        q_shape = jax.ShapeDtypeStruct((groups, N_HEADS, t, LANE), BF16)
        k_shape = jax.ShapeDtypeStruct((groups, t, LANE), BF16)
        vt_shape = jax.ShapeDtypeStruct((groups, LANE, t), BF16)
        out_specs += [q_spec, k_spec, vt_spec, q_spec, k_spec, vt_spec, pl.BlockSpec((1, tm, 2 * C_WIDTH), tok)]
        out_shape += [q_shape, k_shape, vt_shape, q_shape, k_shape, vt_shape,
                      jax.ShapeDtypeStruct((groups, t, 2 * C_WIDTH), F32)]
    n_tiles = t // tm
    for w in cast:
        _, r, c = w.shape
        if r % (groups * n_tiles * 16) == 0:
            block, index = (r // (groups * n_tiles), c), lambda b, i: (b * n_tiles + i, 0)
        else:
            block, index = (r // n_tiles, c // groups), lambda b, i: (i, b)
        assert block[0] % 16 == 0 and block[1] % LANE == 0
        in_specs.append(pl.BlockSpec((None,) + block, lambda b, i, index=index: (cast_layer,) + index(b, i)))
        args.append(w)
        out_specs.append(pl.BlockSpec((1,) + block, lambda b, i, index=index: (0,) + index(b, i)))
        out_shape.append(jax.ShapeDtypeStruct((1, r, c), BF16))
    out = pl.pallas_call(
        functools.partial(_ffn_kernel, row0=row0, final=final, mixed=mix is not None,
                          rope=None if proj is None else proj["rope"], n_cast=len(cast)),
        grid=(groups, t // tm),
        in_specs=in_specs,
        out_specs=out_specs,
        out_shape=out_shape,
        compiler_params=_params("parallel", "parallel"),
        name="ffn",
    )(*args)
    return out[0] if len(out) == 1 else out


def _rope_slab(t, cos, sin, lane):
    fwd = pltpu.roll(t, LANE - 16, axis=1)
    bwd = pltpu.roll(t, 16, axis=1)
    partner = jnp.where((lane % 32) < 16, fwd, bwd)
    return t * cos + partner * sin


def _wfold_kernel(cs_hi_ref, cs_lo_ref, w_ref, o_ref):
    w_hi, w_lo = _split3(w_ref[0])
    for half in range(2):
        sl = slice(half * C_WIDTH, (half + 1) * C_WIDTH)
        o_ref[0, :, sl] = _dot3(cs_hi_ref[:, sl], cs_lo_ref[:, sl], w_hi, w_lo).astype(BF16)


def _wfold(cs_hi, cs_lo, wbd):
    depth = wbd.shape[0]
    return pl.pallas_call(
        _wfold_kernel,
        grid=(depth,),
        in_specs=[_resident(cs_hi.shape), _resident(cs_lo.shape),
                  pl.BlockSpec((1, C_WIDTH, C_WIDTH), lambda l: (l, 0, 0))],
        out_specs=pl.BlockSpec((1, C_WIDTH, 2 * C_WIDTH), lambda l: (l, 0, 0)),
        out_shape=jax.ShapeDtypeStruct((depth, C_WIDTH, 2 * C_WIDTH), BF16),
        compiler_params=_params("parallel"),
        name="wfold",
    )(cs_hi, cs_lo, wbd)


def _proj_body(x, mod_ref, g_ref, w_ref, cos_ref, sin_ref, gq_ref, gk_ref, msm_ref, wc_ref,
               qa_ref, ka_ref, vat_ref, qb_ref, kb_ref, vbt_ref, z_ref, *, rope):
    tm = x.shape[0]
    shift = mod_ref[0, 3:4, :]
    scale = mod_ref[0, 4:5, :]
    xm = _norm_mod(x, g_ref[...], shift, scale).astype(BF16)

    u_all = _dot(xm, w_ref[...])

    def u(lo, hi):
        return u_all[:, lo:hi]

    lane = lax.broadcasted_iota(jnp.int32, (tm, LANE), 1)
    low = lane < HEAD_DIM
    if rope:
        cos = cos_ref[...]
        sin = sin_ref[...]

    def qk_norm(t, gain):
        ms = _dot((t * t).astype(BF16), msm_ref[...])
        return t * lax.rsqrt(ms + NORM_EPS) * gain

    def finish(t):
        return _rope_slab(t, cos, sin, lane) if rope else t

    def emit_q(q_ref, slabs):
        for h in range(N_HEADS):
            t = slabs[h // 2] * (LOG2E * HEAD_DIM ** -0.5)
            if (h % 2) != (h // GROUP):
                t = pltpu.roll(t, HEAD_DIM, axis=1)
            keep = low if (h // GROUP) == 0 else jnp.logical_not(low)
            q_ref[0, h] = jnp.where(keep, t, 0.0).astype(BF16)

    qa = [finish(u(LANE * i, LANE * (i + 1))) for i in range(3)]
    emit_q(qa_ref, qa)
    ka_ref[0] = finish(u(384, 512)).astype(BF16)
    vat_ref[0] = u(512, 640).T.astype(BF16)

    qb = [finish(qk_norm(u(640 + LANE * i, 640 + LANE * (i + 1)), gq_ref[...])) for i in range(3)]
    emit_q(qb_ref, qb)
    kb_ref[0] = finish(qk_norm(u(1024, 1152), gk_ref[...])).astype(BF16)
    vbt_ref[0] = u(1152, 1280).T.astype(BF16)

    z_ref[0] = _dot(u(1280, 1536).astype(BF16), wc_ref[...])


def _nt(a, b):
    return lax.dot_general(a, b, (((1,), (1,)), ((), ())), preferred_element_type=F32)


def _store_heads(o_ref, ot, r0=0):
    tq = ot[0].shape[1]
    for pair in range(N_HEADS // 2):
        slab = jnp.concatenate([ot[2 * pair], ot[2 * pair + 1]], axis=0)
        o_ref[0, r0:r0 + tq, pair * LANE:(pair + 1) * LANE] = slab.T.astype(o_ref.dtype)


def _attn_latent_kernel(*refs, tq, tk):
    s_a, s_b, p_a, p_b = refs[13:17]
    n_sub = refs[1].shape[2] // tq
    tiles = [_tile_stages(sub, n_sub, *refs, tq=tq, tk=tk) for sub in range(n_sub)]
    nk = tiles[0]["nk"]
    n_steps = nk + 1
    total = n_sub * n_steps
    bufs = ((s_a, p_a), (s_b, p_b))

    def issue_scores(u):
        j, t = divmod(u, n_steps)
        s_buf, slot = bufs[u % 2][0], u % 2
        if t == 0:
            tiles[j]["scores_win"](s_buf, slot)
        else:
            tiles[j]["scores"](t - 1, s_buf, slot)

    def do_softmax(u):
        j, t = divmod(u, n_steps)
        (s_buf, p_buf), slot = bufs[u % 2], u % 2
        tiles[j]["softmax_win" if t == 0 else "softmax"](s_buf, p_buf, slot)

    def do_values(u):
        j, t = divmod(u, n_steps)
        p_buf, slot = bufs[u % 2][1], u % 2
        if t == 0:
            tiles[j]["values_win"](p_buf, slot)
        else:
            tiles[j]["values"](t - 1, p_buf, slot)
        if t == n_steps - 1:
            tiles[j]["finish"]()

    def flat_step(u):
        if u + 1 < total:
            issue_scores(u + 1)
        do_softmax(u)
        if u >= 1:
            do_values(u - 1)

    n_pairs = max(0, (nk - 3) // 2)
    issue_scores(0)
    for j in range(n_sub):
        base = j * n_steps
        flat_step(base)
        flat_step(base + 1)

        def pair(i, carry, tile=tiles[j], base=base):
            t = 2 + 2 * i
            for dt in (0, 1):
                par = (base + dt) % 2
                (s_cur, p_cur), (s_oth, p_oth) = bufs[par], bufs[1 - par]
                tile["scores"](t + dt, s_oth, 1 - par)
                tile["softmax"](s_cur, p_cur, par)
                tile["values"](t + dt - 2, p_oth, 1 - par)
            return carry

        lax.fori_loop(0, n_pairs, pair, 0)
        for t in range(2 + 2 * n_pairs, n_steps):
            flat_step(base + t)
    do_values(total - 1)


def _tile_stages(sub, n_sub, qa_ref, qb_ref, ka_ref, vat_ref, kb_ref, vbt_ref, kac_ref, vatc_ref, kbc_ref,
                 vbtc_ref, sink_ref, oa_ref, ob_ref, s_a, s_b, p_a, p_b, m_sc, bmax_sc, alpha_sc, acc_sc, *, tq, tk):
    length = kb_ref.shape[1]
    lc = kbc_ref.shape[1]
    nk = (length + lc) // tk
    tail = length - (nk - 1) * tk
    wq = min(tq, WIN_TILE)
    span = wq + 2 * WINDOW
    na = span + lc
    tile = pl.program_id(1) * n_sub + sub
    r0 = sub * tq
    m_sc, acc_sc = m_sc.at[sub], acc_sc.at[sub]
    starts = [pl.multiple_of(jnp.clip(tile * tq + i * wq - WINDOW, 0, length - span), LANE)
              for i in range(tq // wq)]
    m_sc[...] = jnp.full_like(m_sc, NEG_INF)
    acc_sc[...] = jnp.zeros_like(acc_sc)

    def with_ones(vt):
        return jnp.concatenate([vt, jnp.ones((DEN_ROWS, vt.shape[1]), BF16)], axis=0)

    def rows(g):
        return slice(g * HEAD_DIM, (g + 1) * HEAD_DIM)

    def sub_cols(h, i):
        return slice(h * tq + i * wq, h * tq + (i + 1) * wq)

    def scores_win(s_buf, slot):
        for i, start in enumerate(starts):
            kw = ka_ref[0, pl.ds(start, span), :]
            kpos = start + lax.broadcasted_iota(jnp.int32, (span, wq), 0)
            qpos = tile * tq + i * wq + lax.broadcasted_iota(jnp.int32, (span, wq), 1)
            valid = jnp.abs(kpos - qpos) <= WINDOW
            for h in range(N_HEADS):
                cs = sub_cols(h, i)
                q = qa_ref[0, h, r0 + i * wq:r0 + (i + 1) * wq, :]
                s_w = jnp.where(valid, _nt(kw, q), NEG_INF)
                s_c = _nt(kac_ref[0], q)
                s_buf[0:span, cs] = s_w
                s_buf[span:na, cs] = s_c
                bmax_sc[slot, :, cs] = jnp.maximum(jnp.max(s_w, axis=0, keepdims=True),
                                                   jnp.max(s_c, axis=0, keepdims=True))

    def softmax_win(s_buf, p_buf, slot):
        for h in range(N_HEADS):
            hc = head_cols(h)
            sink = sink_ref[:, hc]
            m = jnp.maximum(sink, bmax_sc[slot, :, hc])
            alpha_sc[slot, :, hc] = jnp.exp2(sink - m)
            p_buf[0:na, hc] = jnp.exp2(s_buf[0:na, hc] - m).astype(BF16)

    def values_win(p_buf, slot):
        parts = [[None] * len(starts) for _ in range(N_HEADS)]
        for g in range(2):
            for i, start in enumerate(starts):
                vt = with_ones(jnp.concatenate(
                    [vat_ref[0, rows(g), pl.ds(start, span)], vatc_ref[0, rows(g), :]], axis=1))
                for h in range(g * GROUP, (g + 1) * GROUP):
                    cs = sub_cols(h, i)
                    acc = _dot(vt, p_buf[0:na, cs])
                    parts[h][i] = acc[0:HEAD_DIM] / (acc[HEAD_DIM:HEAD_DIM + 1] + alpha_sc[slot, :, cs])
        _store_heads(oa_ref, [jnp.concatenate(p, axis=1) for p in parts], r0)

    def key_block(blk):
        if isinstance(blk, int) and blk == nk - 1:
            return jnp.concatenate([kb_ref[0, length - tail:, :], kbc_ref[0]], axis=0)
        return kb_ref[0, pl.ds(pl.multiple_of(blk * tk, tk), tk), :]

    def value_block(blk, g):
        if isinstance(blk, int) and blk == nk - 1:
            return jnp.concatenate([vbt_ref[0, rows(g), length - tail:], vbtc_ref[0, rows(g), :]], axis=1)
        return vbt_ref[0, rows(g), pl.ds(pl.multiple_of(blk * tk, tk), tk)]

    def head_cols(h):
        return slice(h * tq, (h + 1) * tq)

    def scores(blk, s_buf, slot):
        kb = key_block(blk)
        for h in range(N_HEADS):
            s = _nt(kb, qb_ref[0, h, r0:r0 + tq, :])
            s_buf[0:tk, head_cols(h)] = s
            bmax_sc[slot, :, head_cols(h)] = jnp.max(s, axis=0, keepdims=True)

    def softmax(s_buf, p_buf, slot):
        for h in range(N_HEADS):
            hc = head_cols(h)
            m_old = m_sc[:, hc]
            m_new = jnp.maximum(m_old, bmax_sc[slot, :, hc])
            m_sc[:, hc] = m_new
            alpha_sc[slot, :, hc] = jnp.exp2(m_old - m_new)
            p_buf[0:tk, hc] = jnp.exp2(s_buf[0:tk, hc] - m_new).astype(BF16)

    def values(blk, p_buf, slot):
        for g in range(2):
            vt = with_ones(value_block(blk, g))
            for h in range(g * GROUP, (g + 1) * GROUP):
                hc = head_cols(h)
                acc_sc[h] = acc_sc[h] * alpha_sc[slot, :, hc] + _dot(vt, p_buf[0:tk, hc])

    def finish():
        _store_heads(ob_ref, [acc_sc[h, 0:HEAD_DIM] / acc_sc[h, HEAD_DIM:HEAD_DIM + 1] for h in range(N_HEADS)], r0)

    return dict(nk=nk, scores_win=scores_win, softmax_win=softmax_win, values_win=values_win,
                scores=scores, softmax=softmax, values=values, finish=finish)


def _key_tile(n_keys, cap=1024):
    return max(t for t in range(LANE, cap + 1, LANE) if n_keys % t == 0)


def _attn_latent(qa, ka, vat, qb, kb, vbt, ka_c, vat_c, kb_c, vbt_c, sink_row, *, tq, tk, n_sub):
    b, _, l, _ = qb.shape
    lc = ka_c.shape[1] // b
    assert (l + lc) % tk == 0 and (l + lc) // tk * tk - tk <= l and tk > lc
    buf_rows = max(tk, min(tq, WIN_TILE) + 2 * WINDOW + lc)
    buf_cols = N_HEADS * tq
    q_spec = pl.BlockSpec((1, N_HEADS, n_sub * tq, LANE), lambda b, i: (b, 0, i, 0))
    k_spec = pl.BlockSpec((1, l, LANE), lambda b, i: (b, 0, 0), pipeline_mode=pl.Buffered(1))
    vt_spec = pl.BlockSpec((1, LANE, l), lambda b, i: (b, 0, 0), pipeline_mode=pl.Buffered(1))
    kc_spec = pl.BlockSpec((1, lc, LANE), lambda b, i: (0, b, 0), pipeline_mode=pl.Buffered(1))
    vtc_spec = pl.BlockSpec((1, LANE, lc), lambda b, i: (0, 0, b), pipeline_mode=pl.Buffered(1))
    o_spec = pl.BlockSpec((1, n_sub * tq, N_HEADS * HEAD_DIM), lambda b, i: (b, i, 0))
    o_shape = jax.ShapeDtypeStruct((b, l, N_HEADS * HEAD_DIM), BF16)
    return pl.pallas_call(
        functools.partial(_attn_latent_kernel, tq=tq, tk=tk),
        grid=(b, l // (n_sub * tq)),
        in_specs=[q_spec, q_spec, k_spec, vt_spec, k_spec, vt_spec, kc_spec, vtc_spec, kc_spec, vtc_spec,
                  pl.BlockSpec((1, N_HEADS * tq), lambda b, i: (0, 0))],
        out_specs=[o_spec, o_spec],
        out_shape=[o_shape, o_shape],
        scratch_shapes=[
            pltpu.VMEM((buf_rows, buf_cols), F32),
            pltpu.VMEM((buf_rows, buf_cols), F32),
            pltpu.VMEM((buf_rows, buf_cols), BF16),
            pltpu.VMEM((buf_rows, buf_cols), BF16),
            pltpu.VMEM((n_sub, 1, N_HEADS * tq), F32),
            pltpu.VMEM((2, 1, N_HEADS * tq), F32),
            pltpu.VMEM((2, 1, N_HEADS * tq), F32),
            pltpu.VMEM((n_sub, N_HEADS, HEAD_DIM + DEN_ROWS, tq), F32),
        ],
        compiler_params=_params("parallel", "parallel"),
        name="attn_latent",
    )(qa, qb, ka, vat, kb, vbt, ka_c, vat_c, kb_c, vbt_c, sink_row)


def _softmax_pv(q_ref, ks, vts, extra, mask=None):
    tq = q_ref.shape[2]
    aug = [[jnp.concatenate([vt[g * HEAD_DIM:(g + 1) * HEAD_DIM, :],
                             jnp.ones((DEN_ROWS, vt.shape[1]), BF16)], axis=0) for vt in vts]
           for g in range(2)]
    out = []
    for h in range(N_HEADS):
        ss = [_nt(k, q_ref[0, h]) for k in ks]
        if mask is not None:
            ss[0] = jnp.where(mask, ss[0], NEG_INF)
        m = functools.reduce(jnp.maximum, [jnp.max(s, axis=0, keepdims=True) for s in ss])
        if extra is not None:
            e = extra[:, h * tq:(h + 1) * tq]
            m = jnp.maximum(m, e)
        acc = functools.reduce(
            jnp.add, [_dot(vt, jnp.exp2(s - m).astype(BF16)) for s, vt in zip(ss, aug[h // GROUP])])
        den = acc[HEAD_DIM:HEAD_DIM + 1]
        if extra is not None:
            den = den + jnp.exp2(e - m)
        out.append(acc[0:HEAD_DIM] / den)
    return out


def _attn_ctx_kernel(q_ref, k_ref, vt_ref, *rest, has_sink):
    if has_sink:
        sink_ref, o_ref = rest
    else:
        (o_ref,) = rest
    extra = sink_ref[...] if has_sink else None
    _store_heads(o_ref, _softmax_pv(q_ref, [k_ref[0]], [vt_ref[0]], extra))


def _attn_ctx(q, k, vt, sink_row, *, batch):
    lc = q.shape[2] // batch
    in_specs = [
        pl.BlockSpec((1, N_HEADS, lc, LANE), lambda b: (0, 0, b, 0)),
        pl.BlockSpec((1, lc, LANE), lambda b: (0, b, 0)),
        pl.BlockSpec((1, LANE, lc), lambda b: (0, 0, b)),
    ]
    args = [q, k, vt]
    if sink_row is not None:
        in_specs.append(pl.BlockSpec((1, N_HEADS * lc), lambda b: (0, 0)))
        args.append(sink_row)
    return pl.pallas_call(
        functools.partial(_attn_ctx_kernel, has_sink=sink_row is not None),
        grid=(batch,),
        in_specs=in_specs,
        out_specs=pl.BlockSpec((1, lc, N_HEADS * HEAD_DIM), lambda b: (0, b, 0)),
        out_shape=jax.ShapeDtypeStruct((1, batch * lc, N_HEADS * HEAD_DIM), BF16),
        compiler_params=_params("parallel"),
        name="attn_ctx",
    )(*args)


def _stack3_lhs(hi, lo):
    return jnp.concatenate([hi, hi, lo], axis=1)


def _stack3_rhs(x):
    hi, lo = _split3(x)
    return jnp.concatenate([hi, lo, hi], axis=0)


def _fft1_kernel(x_ref, t_ref, twc_ref, tws_ref, o_ref):
    tn2 = x_ref.shape[2]
    n1 = x_ref.shape[1]
    xt = jnp.transpose(x_ref[0], (1, 0, 2))
    y = _dot(t_ref[...], _stack3_rhs(jnp.concatenate([xt[i] for i in range(tn2)], axis=1)))
    cx, sx = y[0:n1], y[n1:]
    w = 2 * C_WIDTH
    outs = []
    for i in range(tn2):
        tc = jnp.concatenate([twc_ref[i]] * (C_WIDTH // LANE), axis=1)
        ts = jnp.concatenate([tws_ref[i]] * (C_WIDTH // LANE), axis=1)
        re, im = slice(i * w, i * w + C_WIDTH), slice(i * w + C_WIDTH, (i + 1) * w)
        ar = cx[:, re] + sx[:, im]
        ai = cx[:, im] - sx[:, re]
        outs.append(jnp.concatenate([ar * tc + ai * ts, ai * tc - ar * ts], axis=1))
    o_ref[0] = jnp.transpose(jnp.stack(outs, axis=0), (1, 0, 2))


def _real_dft(t_ref, x):
    rhs = jnp.concatenate([_stack3_rhs(x[:, 0:C_WIDTH]), _stack3_rhs(x[:, C_WIDTH:])], axis=0)
    return _dot(t_ref[...], rhs)


def _fft2_kernel(x_ref, t_ref, o_ref):
    outs = [_real_dft(t_ref, x_ref[0, i]) for i in range(x_ref.shape[1])]
    o_ref[0] = jnp.transpose(jnp.stack(outs, axis=0), (1, 0, 2))


def _dft_tables(n, norm):
    k = np.arange(n, dtype=np.int64)
    ang = 2.0 * np.pi * ((k[:, None] * k[None, :]) % n).astype(np.float64) / n
    c = jnp.asarray(np.cos(ang) * norm, F32)
    s = jnp.asarray(np.sin(ang) * norm, F32)
    return _stack3_lhs(*_split3(c)), _stack3_lhs(*_split3(s))


def _fourier_latent(z):
    b, l, w = z.shape
    n1, n2 = FFT_N1, l // FFT_N1
    tn2, tk1 = 16, 16
    prod = jnp.arange(n2)[:, None] * jnp.arange(n1)[None, :]
    ang = prod.astype(F32) * (2.0 * np.pi / l)
    twc = jnp.broadcast_to(jnp.cos(ang)[:, :, None], (n2, n1, LANE))
    tws = jnp.broadcast_to(jnp.sin(ang)[:, :, None], (n2, n1, LANE))
    t1 = jnp.concatenate(_dft_tables(n1, 1.0), axis=0)
    y = pl.pallas_call(
        _fft1_kernel,
        grid=(b, n2 // tn2),
        in_specs=[pl.BlockSpec((1, n1, tn2, w), lambda b, j: (b, 0, j, 0))]
        + [_resident(t1.shape)]
        + [pl.BlockSpec((tn2, n1, LANE), lambda b, j: (j, 0, 0))] * 2,
        out_specs=pl.BlockSpec((1, n1, tn2, w), lambda b, j: (b, 0, j, 0)),
        out_shape=jax.ShapeDtypeStruct((b, n1, n2, w), F32),
        compiler_params=_params("parallel", "parallel"),
        name="fft_stage1",
    )(z.reshape(b, n1, n2, w), t1, twc, tws)
    t2m = jnp.concatenate(_dft_tables(n2, 1.0 / np.sqrt(l)), axis=1)
    out = pl.pallas_call(
        _fft2_kernel,
        grid=(b, n1 // tk1),
        in_specs=[pl.BlockSpec((1, tk1, n2, w), lambda b, j: (b, j, 0, 0)), _resident(t2m.shape)],
        out_specs=pl.BlockSpec((1, n2, tk1, C_WIDTH), lambda b, j: (b, 0, j, 0)),
        out_shape=jax.ShapeDtypeStruct((b, n2, n1, C_WIDTH), F32),
        compiler_params=_params("parallel", "parallel"),
        name="fft_stage2",
    )(y, t2m)
    return out.reshape(b, l, C_WIDTH)


def _fourier_ctx_kernel(x_ref, t_ref, o_ref):
    o_ref[0] = _real_dft(t_ref, x_ref[0])


def _fourier_ctx(z, *, batch):
    lc = z.shape[1] // batch
    tabs = jnp.concatenate(_dft_tables(lc, 1.0 / np.sqrt(lc)), axis=1)
    return pl.pallas_call(
        _fourier_ctx_kernel,
        grid=(batch,),
        in_specs=[pl.BlockSpec((1, lc, 2 * C_WIDTH), lambda b: (0, b, 0)), _resident(tabs.shape)],
        out_specs=pl.BlockSpec((1, lc, C_WIDTH), lambda b: (0, b, 0)),
        out_shape=jax.ShapeDtypeStruct((1, batch * lc, C_WIDTH), F32),
        compiler_params=_params("parallel"),
        name="fourier_ctx",
    )(z, tabs)


def _rope_tables(length):
    t = jnp.arange(length)
    row = (t // GRID_W).astype(F32)
    col = (t % GRID_W).astype(F32)
    half = HEAD_DIM // 2
    inv = ROPE_THETA ** (-jnp.arange(0, half, 2, dtype=F32) / half)
    ar, ac = row[:, None] * inv, col[:, None] * inv
    cos = jnp.concatenate([jnp.cos(ar), jnp.cos(ar), jnp.cos(ac), jnp.cos(ac)], axis=-1)
    sin = jnp.concatenate([-jnp.sin(ar), jnp.sin(ar), -jnp.sin(ac), jnp.sin(ac)], axis=-1)
    return jnp.tile(cos, (1, 2)), jnp.tile(sin, (1, 2))


def _channel_dft():
    k = np.arange(C_GROUP_DIM)
    ang = 2.0 * np.pi * ((k[:, None] * k[None, :]) % C_GROUP_DIM) / C_GROUP_DIM
    eye = np.eye(C_GROUPS)
    norm = 1.0 / np.sqrt(C_GROUP_DIM)
    cs = np.concatenate([np.kron(eye, np.cos(ang) * norm), np.kron(eye, -np.sin(ang) * norm)], axis=1)
    return _split3(jnp.asarray(cs, F32))


def _block_diag(w):
    g, c, _ = w.shape
    eye = jnp.eye(g, dtype=w.dtype)
    return (eye[:, None, :, None] * w[:, :, None, :]).reshape(g * c, g * c)


def kernel(x, c, ctx, c_ctx, w_ada, b_ada, g_ffn1, g_mix, g_ffn2, w_in, g_qn, g_kn, sink, w_four,
           w_out, w1_gate, w1_up, w1_down, w2_gate, w2_up, w2_down, g_final):
    batch, length, d = x.shape
    lc = ctx.shape[1]
    depth = w_ada.shape[0]
    assert length % (FFT_N1 * 8) == 0 and length >= BLOCK + 2 * WINDOW and batch + 1 <= 8
    tm = 512
    tm_ctx = batch * lc

    cvec = jnp.zeros((8, d), F32).at[:batch].set(c).at[batch].set(c_ctx)
    mod = _ada(cvec, w_ada, b_ada).reshape(depth, 8, N_MOD, d)

    cos, sin = _rope_tables(length)
    ones = jnp.ones((tm_ctx, LANE), F32)
    msm = _block_diag(jnp.full((2, HEAD_DIM, HEAD_DIM), 1.0 / HEAD_DIM, F32)).astype(BF16)
    wc = _wfold(*_channel_dft(), jax.vmap(_block_diag)(w_four))

    cast_ffn1 = (w1_gate, w1_up, w1_down, w_in)
    cast_ffn2 = (w2_gate, w2_up, w2_down, w_out)
    *w1, win = (w[0:1].astype(BF16) for w in cast_ffn1)

    h = x
    hc = ctx.reshape(1, batch * lc, d)
    for l in range(depth):
        last = l == depth - 1
        mod_l, mod_c = mod[l, :batch], mod[l, batch:batch + 1]
        gq = jnp.tile(g_qn[l].reshape(1, HEAD_DIM), (1, 2))
        gk = jnp.tile(g_kn[l].reshape(1, HEAD_DIM), (1, 2))
        proj = dict(g=g_mix[l], w_in=win, gq=gq, gk=gk, msm=msm, wc=wc[l:l + 1])

        h, qa, ka, vat, qb, kb, vbt, z, *w2, wout = _ffn(
            h, mod_l, g_ffn1[l], *w1, row0=0, tm=tm, proj=dict(proj, cos=cos, sin=sin, rope=True),
            cast=(l, cast_ffn2))
        hc, qa_c, ka_c, vat_c, qb_c, kb_c, vbt_c, z_c = _ffn(
            hc, mod_c, g_ffn1[l], *w1, row0=0, tm=tm_ctx, proj=dict(proj, cos=ones, sin=ones, rope=False))

        tq = 512
        sink_l = jnp.repeat(sink[l].astype(F32) * LOG2E, tq).reshape(1, N_HEADS * tq)
        oa, ob = _attn_latent(qa, ka, vat, qb, kb, vbt, ka_c, vat_c, kb_c, vbt_c, sink_l,
                              tq=tq, tk=_key_tile(length + lc), n_sub=1)
        oc = _fourier_latent(z)
        if last:
            h = _ffn(h, mod_l, g_ffn2[l], *w2, row0=6, tm=tm, mix=(oa, ob, oc, wout), final_g=g_final)
        else:
            h, *w1, win = _ffn(h, mod_l, g_ffn2[l], *w2, row0=6, tm=tm, mix=(oa, ob, oc, wout),
                               cast=(l + 1, cast_ffn1))

        if not last:
            sink_c = jnp.repeat(sink[l].astype(F32) * LOG2E, lc).reshape(1, N_HEADS * lc)
            oa_c = _attn_ctx(qa_c, ka_c, vat_c, sink_c, batch=batch)
            ob_c = _attn_ctx(qb_c, kb_c, vbt_c, None, batch=batch)
            oc_c = _fourier_ctx(z_c, batch=batch)
            hc = _ffn(hc, mod_c, g_ffn2[l], *w2, row0=6, tm=tm_ctx, mix=(oa_c, ob_c, oc_c, wout))
    return h
```

```python
import functools

import jax
import jax.numpy as jnp
import numpy as np
from jax import lax
from jax.experimental import pallas as pl
from jax.experimental.pallas import tpu as pltpu

HEAD_DIM = 64
A_HEADS = 6
A_KV_HEADS = 2
B_HEADS = 6
B_KV_HEADS = 2
C_GROUPS = 4
C_GROUP_DIM = 64
GRID_W = 64
WINDOW = 128
BLOCK = 128
ROPE_THETA = 10000.0
NORM_EPS = 1e-6
N_MOD = 9
NEG_INF = -1e30

N_HEADS = 6
GROUP = 3
C_WIDTH = C_GROUPS * C_GROUP_DIM
FFT_N1 = 64
WIN_TILE = 256
DEN_ROWS = 16
LOG2E = 1.4426950408889634

LANE = 128
VMEM_LIMIT = 60 * 1024 * 1024

F32 = jnp.float32
BF16 = jnp.bfloat16


def _params(*sem):
    return pltpu.CompilerParams(dimension_semantics=tuple(sem), vmem_limit_bytes=VMEM_LIMIT)


def _resident(shape):
    nd = len(shape)
    return pl.BlockSpec(shape, lambda *_: (0,) * nd, pipeline_mode=pl.Buffered(1))


def _layer_resident(stacked, layer):
    nd = stacked.ndim
    return pl.BlockSpec((None,) + stacked.shape[1:], lambda *_: (layer,) + (0,) * (nd - 1),
                        pipeline_mode=pl.Buffered(1))


def _split3(x):
    hi = x.astype(BF16)
    lo = (x - hi.astype(F32)).astype(BF16)
    return hi, lo


def _dot(a, b):
    return jnp.dot(a, b, preferred_element_type=F32)


def _dot3(a_hi, a_lo, b_hi, b_lo):
    return _dot(a_hi, b_hi) + _dot(a_hi, b_lo) + _dot(a_lo, b_hi)


def _silu(x):
    return x / (1.0 + jnp.exp(-x))


def _norm_mod(x, g, shift, scale):
    ms = jnp.mean(x * x, axis=-1, keepdims=True)
    return (x * lax.rsqrt(ms + NORM_EPS)) * (g * (1.0 + scale)) + shift


def _ada_kernel(c_ref, w_ref, b_ref, o_ref):
    s_hi, s_lo = _split3(_silu(c_ref[...]))
    w_hi, w_lo = _split3(w_ref[0])
    o_ref[0] = _dot3(s_hi, s_lo, w_hi, w_lo) + b_ref[0]


def _ada(cvec, w_ada, b_ada):
    depth, d, nd = w_ada.shape
    tn = 1024
    return pl.pallas_call(
        _ada_kernel,
        grid=(depth, nd // tn),
        in_specs=[
            pl.BlockSpec((8, d), lambda l, j: (0, 0)),
            pl.BlockSpec((1, d, tn), lambda l, j: (l, 0, j)),
            pl.BlockSpec((1, 1, tn), lambda l, j: (l, 0, j)),
        ],
        out_specs=pl.BlockSpec((1, 8, tn), lambda l, j: (l, 0, j)),
        out_shape=jax.ShapeDtypeStruct((depth, 8, nd), F32),
        compiler_params=_params("parallel", "parallel"),
        name="ada",
    )(cvec, w_ada, b_ada.reshape(depth, 1, nd))


def _mix_out(oa_ref, ob_ref, oc_ref, w_ref):
    na = N_HEADS * HEAD_DIM
    return (_dot(oa_ref[0], w_ref[0:na, :]) + _dot(ob_ref[0], w_ref[na:2 * na, :])
            + _dot(oc_ref[0].astype(BF16), w_ref[2 * na:, :]))


N_PROJ_IN = 8


def _ffn_kernel(*refs, row0, final, mixed, rope, n_cast):
    refs = list(refs)
    if n_cast:
        cast_out = refs[-n_cast:]
        del refs[-n_cast:]
    x_ref, mod_ref, g_ref, wg_ref, wu_ref, wd_ref = refs[:6]
    del refs[:6]
    if mixed:
        oa_ref, ob_ref, oc_ref, wo_ref = refs[:4]
        del refs[:4]
    if final:
        gf_ref = refs.pop(0)
    proj_in = refs[:N_PROJ_IN] if rope is not None else []
    del refs[:len(proj_in)]
    cast_in = refs[:n_cast]
    o_ref = refs[n_cast]
    proj_out = refs[n_cast + 1:]
    shift = mod_ref[0, row0:row0 + 1, :]
    scale = mod_ref[0, row0 + 1:row0 + 2, :]
    gate = mod_ref[0, row0 + 2:row0 + 3, :]
    x = x_ref[0]
    if mixed:
        x = x + mod_ref[0, 5:6, :] * _mix_out(oa_ref, ob_ref, oc_ref, wo_ref)
    xm = _norm_mod(x, g_ref[...], shift, scale).astype(BF16)
    a = _dot(xm, wg_ref[...])
    u = _dot(xm, wu_ref[...])
    act = (_silu(a) * u).astype(BF16)
    y = x + (0.5 * gate) * _dot(act, wd_ref[...])
    if final:
        ms = jnp.mean(y * y, axis=-1, keepdims=True)
        y = (y * lax.rsqrt(ms + NORM_EPS)) * gf_ref[...]
    o_ref[0] = y
    if rope is not None:
        _proj_body(y, mod_ref, *proj_in, *proj_out, rope=rope)
    for src, dst in zip(cast_in, cast_out if n_cast else []):
        dst[0] = src[...].astype(BF16)


def _ffn(x, mod, g, wg, wu, wd, *, layer, row0, tm, mix=None, final_g=None, proj=None, cast=()):
    groups, t, d = x.shape
    final = final_g is not None
    tok = lambda b, i: (b, i, 0)
    in_specs = [
        pl.BlockSpec((1, tm, d), tok),
        pl.BlockSpec((1, N_MOD, d), lambda b, i: (b, 0, 0)),
        _resident((1, d)),
        _layer_resident(wg, layer),
        _layer_resident(wu, layer),
        _layer_resident(wd, layer),
    ]
    args = [x, mod, g.reshape(1, d), wg, wu, wd]
    if mix is not None:
        oa, ob, oc, w_out = mix
        in_specs += [pl.BlockSpec((1, tm, oa.shape[2]), tok), pl.BlockSpec((1, tm, ob.shape[2]), tok),
                     pl.BlockSpec((1, tm, oc.shape[2]), tok), _layer_resident(w_out, layer)]
        args += [oa, ob, oc, w_out]
    if final:
        in_specs.append(_resident((1, d)))
        args.append(final_g.reshape(1, d))
    out_specs = [pl.BlockSpec((1, tm, d), tok)]
    out_shape = [jax.ShapeDtypeStruct((groups, t, d), F32)]
    if proj is not None:
        in_specs += [
            _resident((1, d)),
            _layer_resident(proj["w_in"], layer),
            pl.BlockSpec((tm, LANE), lambda b, i: (i, 0)),
            pl.BlockSpec((tm, LANE), lambda b, i: (i, 0)),
            _resident((1, LANE)),
            _resident((1, LANE)),
            _resident((LANE, LANE)),
            _layer_resident(proj["wc"], layer),
        ]
        args += [proj["g"].reshape(1, d), proj["w_in"], proj["cos"], proj["sin"], proj["gq"], proj["gk"],
                 proj["msm"], proj["wc"]]
        q_spec = pl.BlockSpec((1, N_HEADS, tm, LANE), lambda b, i: (b, 0, i, 0))
        k_spec = pl.BlockSpec((1, tm, LANE), tok)
        vt_spec = pl.BlockSpec((1, LANE, tm), lambda b, i: (b, 0, i))
        q_shape = jax.ShapeDtypeStruct((groups, N_HEADS, t, LANE), BF16)
        k_shape = jax.ShapeDtypeStruct((groups, t, LANE), BF16)
        vt_shape = jax.ShapeDtypeStruct((groups, LANE, t), BF16)
        out_specs += [q_spec, k_spec, vt_spec, q_spec, k_spec, vt_spec, pl.BlockSpec((1, tm, 2 * C_WIDTH), tok)]
        out_shape += [q_shape, k_shape, vt_shape, q_shape, k_shape, vt_shape,
                      jax.ShapeDtypeStruct((groups, t, 2 * C_WIDTH), F32)]
    n_tiles = t // tm
    for w in cast:
        _, r, c = w.shape
        if r % (groups * n_tiles * 16) == 0:
            block, index = (r // (groups * n_tiles), c), lambda b, i: (b * n_tiles + i, 0)
        else:
            block, index = (r // n_tiles, c // groups), lambda b, i: (i, b)
        assert block[0] % 16 == 0 and block[1] % LANE == 0
        in_specs.append(pl.BlockSpec((None,) + block, lambda b, i, index=index: (layer,) + index(b, i)))
        args.append(w)
        out_specs.append(pl.BlockSpec((1,) + block, lambda b, i, index=index: (0,) + index(b, i)))
        out_shape.append(jax.ShapeDtypeStruct((1, r, c), BF16))
    out = pl.pallas_call(
        functools.partial(_ffn_kernel, row0=row0, final=final, mixed=mix is not None,
                          rope=None if proj is None else proj["rope"], n_cast=len(cast)),
        grid=(groups, t // tm),
        in_specs=in_specs,
        out_specs=out_specs,
        out_shape=out_shape,
        compiler_params=_params("parallel", "parallel"),
        name="ffn",
    )(*args)
    return out[0] if len(out) == 1 else out


def _rope_slab(t, cos, sin, lane):
    fwd = pltpu.roll(t, LANE - 16, axis=1)
    bwd = pltpu.roll(t, 16, axis=1)
    partner = jnp.where((lane % 32) < 16, fwd, bwd)
    return t * cos + partner * sin


def _wfold_kernel(cs_hi_ref, cs_lo_ref, w_ref, o_ref):
    w_hi, w_lo = _split3(w_ref[0])
    for half in range(2):
        sl = slice(half * C_WIDTH, (half + 1) * C_WIDTH)
        o_ref[0, :, sl] = _dot3(cs_hi_ref[:, sl], cs_lo_ref[:, sl], w_hi, w_lo).astype(BF16)


def _wfold(cs_hi, cs_lo, wbd):
    depth = wbd.shape[0]
    return pl.pallas_call(
        _wfold_kernel,
        grid=(depth,),
        in_specs=[_resident(cs_hi.shape), _resident(cs_lo.shape),
                  pl.BlockSpec((1, C_WIDTH, C_WIDTH), lambda l: (l, 0, 0))],
        out_specs=pl.BlockSpec((1, C_WIDTH, 2 * C_WIDTH), lambda l: (l, 0, 0)),
        out_shape=jax.ShapeDtypeStruct((depth, C_WIDTH, 2 * C_WIDTH), BF16),
        compiler_params=_params("parallel"),
        name="wfold",
    )(cs_hi, cs_lo, wbd)


def _proj_body(x, mod_ref, g_ref, w_ref, cos_ref, sin_ref, gq_ref, gk_ref, msm_ref, wc_ref,
               qa_ref, ka_ref, vat_ref, qb_ref, kb_ref, vbt_ref, z_ref, *, rope):
    tm = x.shape[0]
    shift = mod_ref[0, 3:4, :]
    scale = mod_ref[0, 4:5, :]
    xm = _norm_mod(x, g_ref[...], shift, scale).astype(BF16)

    u_all = _dot(xm, w_ref[...])

    def u(lo, hi):
        return u_all[:, lo:hi]

    lane = lax.broadcasted_iota(jnp.int32, (tm, LANE), 1)
    low = lane < HEAD_DIM
    if rope:
        cos = cos_ref[...]
        sin = sin_ref[...]

    def qk_norm(t, gain):
        ms = _dot((t * t).astype(BF16), msm_ref[...])
        return t * lax.rsqrt(ms + NORM_EPS) * gain

    def finish(t):
        return _rope_slab(t, cos, sin, lane) if rope else t

    def emit_q(q_ref, slabs):
        for h in range(N_HEADS):
            t = slabs[h // 2] * (LOG2E * HEAD_DIM ** -0.5)
            if (h % 2) != (h // GROUP):
                t = pltpu.roll(t, HEAD_DIM, axis=1)
            keep = low if (h // GROUP) == 0 else jnp.logical_not(low)
            q_ref[0, h] = jnp.where(keep, t, 0.0).astype(BF16)

    qa = [finish(u(LANE * i, LANE * (i + 1))) for i in range(3)]
    emit_q(qa_ref, qa)
    ka_ref[0] = finish(u(384, 512)).astype(BF16)
    vat_ref[0] = u(512, 640).T.astype(BF16)

    qb = [finish(qk_norm(u(640 + LANE * i, 640 + LANE * (i + 1)), gq_ref[...])) for i in range(3)]
    emit_q(qb_ref, qb)
    kb_ref[0] = finish(qk_norm(u(1024, 1152), gk_ref[...])).astype(BF16)
    vbt_ref[0] = u(1152, 1280).T.astype(BF16)

    z_ref[0] = _dot(u(1280, 1536).astype(BF16), wc_ref[...])


def _nt(a, b):
    return lax.dot_general(a, b, (((1,), (1,)), ((), ())), preferred_element_type=F32)


def _store_heads(o_ref, ot, r0=0):
    tq = ot[0].shape[1]
    for pair in range(N_HEADS // 2):
        slab = jnp.concatenate([ot[2 * pair], ot[2 * pair + 1]], axis=0)
        o_ref[0, r0:r0 + tq, pair * LANE:(pair + 1) * LANE] = slab.T.astype(o_ref.dtype)


def _attn_latent_kernel(*refs, tq, tk):
    s_a, s_b, p_a, p_b = refs[13:17]
    n_sub = refs[1].shape[2] // tq
    tiles = [_tile_stages(sub, n_sub, *refs, tq=tq, tk=tk) for sub in range(n_sub)]
    nk = tiles[0]["nk"]
    n_steps = nk + 1
    total = n_sub * n_steps
    bufs = ((s_a, p_a), (s_b, p_b))

    def issue_scores(u):
        j, t = divmod(u, n_steps)
        s_buf, slot = bufs[u % 2][0], u % 2
        if t == 0:
            tiles[j]["scores_win"](s_buf, slot)
        else:
            tiles[j]["scores"](t - 1, s_buf, slot)

    def do_softmax(u):
        j, t = divmod(u, n_steps)
        (s_buf, p_buf), slot = bufs[u % 2], u % 2
        tiles[j]["softmax_win" if t == 0 else "softmax"](s_buf, p_buf, slot)

    def do_values(u):
        j, t = divmod(u, n_steps)
        p_buf, slot = bufs[u % 2][1], u % 2
        if t == 0:
            tiles[j]["values_win"](p_buf, slot)
        else:
            tiles[j]["values"](t - 1, p_buf, slot)
        if t == n_steps - 1:
            tiles[j]["finish"]()

    def flat_step(u):
        if u + 1 < total:
            issue_scores(u + 1)
        do_softmax(u)
        if u >= 1:
            do_values(u - 1)

    n_pairs = max(0, (nk - 3) // 2)
    issue_scores(0)
    for j in range(n_sub):
        base = j * n_steps
        flat_step(base)
        flat_step(base + 1)

        def pair(i, carry, tile=tiles[j], base=base):
            t = 2 + 2 * i
            for dt in (0, 1):
                par = (base + dt) % 2
                (s_cur, p_cur), (s_oth, p_oth) = bufs[par], bufs[1 - par]
                tile["scores"](t + dt, s_oth, 1 - par)
                tile["softmax"](s_cur, p_cur, par)
                tile["values"](t + dt - 2, p_oth, 1 - par)
            return carry

        lax.fori_loop(0, n_pairs, pair, 0)
        for t in range(2 + 2 * n_pairs, n_steps):
            flat_step(base + t)
    do_values(total - 1)


def _tile_stages(sub, n_sub, qa_ref, qb_ref, ka_ref, vat_ref, kb_ref, vbt_ref, kac_ref, vatc_ref, kbc_ref,
                 vbtc_ref, sink_ref, oa_ref, ob_ref, s_a, s_b, p_a, p_b, m_sc, bmax_sc, alpha_sc, acc_sc, *, tq, tk):
    length = kb_ref.shape[1]
    lc = kbc_ref.shape[1]
    nk = (length + lc) // tk
    tail = length - (nk - 1) * tk
    wq = min(tq, WIN_TILE)
    span = wq + 2 * WINDOW
    na = span + lc
    tile = pl.program_id(1) * n_sub + sub
    r0 = sub * tq
    m_sc, acc_sc = m_sc.at[sub], acc_sc.at[sub]
    starts = [pl.multiple_of(jnp.clip(tile * tq + i * wq - WINDOW, 0, length - span), LANE)
              for i in range(tq // wq)]
    m_sc[...] = jnp.full_like(m_sc, NEG_INF)
    acc_sc[...] = jnp.zeros_like(acc_sc)

    def with_ones(vt):
        return jnp.concatenate([vt, jnp.ones((DEN_ROWS, vt.shape[1]), BF16)], axis=0)

    def rows(g):
        return slice(g * HEAD_DIM, (g + 1) * HEAD_DIM)

    def sub_cols(h, i):
        return slice(h * tq + i * wq, h * tq + (i + 1) * wq)

    def scores_win(s_buf, slot):
        for i, start in enumerate(starts):
            kw = ka_ref[0, pl.ds(start, span), :]
            kpos = start + lax.broadcasted_iota(jnp.int32, (span, wq), 0)
            qpos = tile * tq + i * wq + lax.broadcasted_iota(jnp.int32, (span, wq), 1)
            valid = jnp.abs(kpos - qpos) <= WINDOW
            for h in range(N_HEADS):
                cs = sub_cols(h, i)
                q = qa_ref[0, h, r0 + i * wq:r0 + (i + 1) * wq, :]
                s_w = jnp.where(valid, _nt(kw, q), NEG_INF)
                s_c = _nt(kac_ref[0], q)
                s_buf[0:span, cs] = s_w
                s_buf[span:na, cs] = s_c
                bmax_sc[slot, :, cs] = jnp.maximum(jnp.max(s_w, axis=0, keepdims=True),
                                                   jnp.max(s_c, axis=0, keepdims=True))

    def softmax_win(s_buf, p_buf, slot):
        for h in range(N_HEADS):
            hc = head_cols(h)
            sink = sink_ref[:, hc]
            m = jnp.maximum(sink, bmax_sc[slot, :, hc])
            alpha_sc[slot, :, hc] = jnp.exp2(sink - m)
            p_buf[0:na, hc] = jnp.exp2(s_buf[0:na, hc] - m).astype(BF16)

    def values_win(p_buf, slot):
        parts = [[None] * len(starts) for _ in range(N_HEADS)]
        for g in range(2):
            for i, start in enumerate(starts):
                vt = with_ones(jnp.concatenate(
                    [vat_ref[0, rows(g), pl.ds(start, span)], vatc_ref[0, rows(g), :]], axis=1))
                for h in range(g * GROUP, (g + 1) * GROUP):
                    cs = sub_cols(h, i)
                    acc = _dot(vt, p_buf[0:na, cs])
                    parts[h][i] = acc[0:HEAD_DIM] / (acc[HEAD_DIM:HEAD_DIM + 1] + alpha_sc[slot, :, cs])
        _store_heads(oa_ref, [jnp.concatenate(p, axis=1) for p in parts], r0)

    def key_block(blk):
        if isinstance(blk, int) and blk == nk - 1:
            return jnp.concatenate([kb_ref[0, length - tail:, :], kbc_ref[0]], axis=0)
        return kb_ref[0, pl.ds(pl.multiple_of(blk * tk, tk), tk), :]

    def value_block(blk, g):
        if isinstance(blk, int) and blk == nk - 1:
            return jnp.concatenate([vbt_ref[0, rows(g), length - tail:], vbtc_ref[0, rows(g), :]], axis=1)
        return vbt_ref[0, rows(g), pl.ds(pl.multiple_of(blk * tk, tk), tk)]

    def head_cols(h):
        return slice(h * tq, (h + 1) * tq)

    def scores(blk, s_buf, slot):
        kb = key_block(blk)
        for h in range(N_HEADS):
            s = _nt(kb, qb_ref[0, h, r0:r0 + tq, :])
            s_buf[0:tk, head_cols(h)] = s
            bmax_sc[slot, :, head_cols(h)] = jnp.max(s, axis=0, keepdims=True)

    def softmax(s_buf, p_buf, slot):
        for h in range(N_HEADS):
            hc = head_cols(h)
            m_old = m_sc[:, hc]
            m_new = jnp.maximum(m_old, bmax_sc[slot, :, hc])
            m_sc[:, hc] = m_new
            alpha_sc[slot, :, hc] = jnp.exp2(m_old - m_new)
            p_buf[0:tk, hc] = jnp.exp2(s_buf[0:tk, hc] - m_new).astype(BF16)

    def values(blk, p_buf, slot):
        for g in range(2):
            vt = with_ones(value_block(blk, g))
            for h in range(g * GROUP, (g + 1) * GROUP):
                hc = head_cols(h)
                acc_sc[h] = acc_sc[h] * alpha_sc[slot, :, hc] + _dot(vt, p_buf[0:tk, hc])

    def finish():
        _store_heads(ob_ref, [acc_sc[h, 0:HEAD_DIM] / acc_sc[h, HEAD_DIM:HEAD_DIM + 1] for h in range(N_HEADS)], r0)

    return dict(nk=nk, scores_win=scores_win, softmax_win=softmax_win, values_win=values_win,
                scores=scores, softmax=softmax, values=values, finish=finish)


def _key_tile(n_keys, cap=1024):
    return max(t for t in range(LANE, cap + 1, LANE) if n_keys % t == 0)


def _attn_latent(qa, ka, vat, qb, kb, vbt, ka_c, vat_c, kb_c, vbt_c, sink_row, *, tq, tk, n_sub):
    b, _, l, _ = qb.shape
    lc = ka_c.shape[1] // b
    assert (l + lc) % tk == 0 and (l + lc) // tk * tk - tk <= l and tk > lc
    buf_rows = max(tk, min(tq, WIN_TILE) + 2 * WINDOW + lc)
    buf_cols = N_HEADS * tq
    q_spec = pl.BlockSpec((1, N_HEADS, n_sub * tq, LANE), lambda b, i: (b, 0, i, 0))
    k_spec = pl.BlockSpec((1, l, LANE), lambda b, i: (b, 0, 0), pipeline_mode=pl.Buffered(1))
    vt_spec = pl.BlockSpec((1, LANE, l), lambda b, i: (b, 0, 0), pipeline_mode=pl.Buffered(1))
    kc_spec = pl.BlockSpec((1, lc, LANE), lambda b, i: (0, b, 0), pipeline_mode=pl.Buffered(1))
    vtc_spec = pl.BlockSpec((1, LANE, lc), lambda b, i: (0, 0, b), pipeline_mode=pl.Buffered(1))
    o_spec = pl.BlockSpec((1, n_sub * tq, N_HEADS * HEAD_DIM), lambda b, i: (b, i, 0))
    o_shape = jax.ShapeDtypeStruct((b, l, N_HEADS * HEAD_DIM), BF16)
    return pl.pallas_call(
        functools.partial(_attn_latent_kernel, tq=tq, tk=tk),
        grid=(b, l // (n_sub * tq)),
        in_specs=[q_spec, q_spec, k_spec, vt_spec, k_spec, vt_spec, kc_spec, vtc_spec, kc_spec, vtc_spec,
                  pl.BlockSpec((1, N_HEADS * tq), lambda b, i: (0, 0))],
        out_specs=[o_spec, o_spec],
        out_shape=[o_shape, o_shape],
        scratch_shapes=[
            pltpu.VMEM((buf_rows, buf_cols), F32),
            pltpu.VMEM((buf_rows, buf_cols), F32),
            pltpu.VMEM((buf_rows, buf_cols), BF16),
            pltpu.VMEM((buf_rows, buf_cols), BF16),
            pltpu.VMEM((n_sub, 1, N_HEADS * tq), F32),
            pltpu.VMEM((2, 1, N_HEADS * tq), F32),
            pltpu.VMEM((2, 1, N_HEADS * tq), F32),
            pltpu.VMEM((n_sub, N_HEADS, HEAD_DIM + DEN_ROWS, tq), F32),
        ],
        compiler_params=_params("parallel", "parallel"),
        name="attn_latent",
    )(qa, qb, ka, vat, kb, vbt, ka_c, vat_c, kb_c, vbt_c, sink_row)


def _softmax_pv(q_ref, ks, vts, extra, mask=None):
    tq = q_ref.shape[2]
    aug = [[jnp.concatenate([vt[g * HEAD_DIM:(g + 1) * HEAD_DIM, :],
                             jnp.ones((DEN_ROWS, vt.shape[1]), BF16)], axis=0) for vt in vts]
           for g in range(2)]
    out = []
    for h in range(N_HEADS):
        ss = [_nt(k, q_ref[0, h]) for k in ks]
        if mask is not None:
            ss[0] = jnp.where(mask, ss[0], NEG_INF)
        m = functools.reduce(jnp.maximum, [jnp.max(s, axis=0, keepdims=True) for s in ss])
        if extra is not None:
            e = extra[:, h * tq:(h + 1) * tq]
            m = jnp.maximum(m, e)
        acc = functools.reduce(
            jnp.add, [_dot(vt, jnp.exp2(s - m).astype(BF16)) for s, vt in zip(ss, aug[h // GROUP])])
        den = acc[HEAD_DIM:HEAD_DIM + 1]
        if extra is not None:
            den = den + jnp.exp2(e - m)
        out.append(acc[0:HEAD_DIM] / den)
    return out


def _attn_ctx_kernel(q_ref, k_ref, vt_ref, *rest, has_sink):
    if has_sink:
        sink_ref, o_ref = rest
    else:
        (o_ref,) = rest
    extra = sink_ref[...] if has_sink else None
    _store_heads(o_ref, _softmax_pv(q_ref, [k_ref[0]], [vt_ref[0]], extra))


def _attn_ctx(q, k, vt, sink_row, *, batch):
    lc = q.shape[2] // batch
    in_specs = [
        pl.BlockSpec((1, N_HEADS, lc, LANE), lambda b: (0, 0, b, 0)),
        pl.BlockSpec((1, lc, LANE), lambda b: (0, b, 0)),
        pl.BlockSpec((1, LANE, lc), lambda b: (0, 0, b)),
    ]
    args = [q, k, vt]
    if sink_row is not None:
        in_specs.append(pl.BlockSpec((1, N_HEADS * lc), lambda b: (0, 0)))
        args.append(sink_row)
    return pl.pallas_call(
        functools.partial(_attn_ctx_kernel, has_sink=sink_row is not None),
        grid=(batch,),
        in_specs=in_specs,
        out_specs=pl.BlockSpec((1, lc, N_HEADS * HEAD_DIM), lambda b: (0, b, 0)),
        out_shape=jax.ShapeDtypeStruct((1, batch * lc, N_HEADS * HEAD_DIM), BF16),
        compiler_params=_params("parallel"),
        name="attn_ctx",
    )(*args)


def _stack3_lhs(hi, lo):
    return jnp.concatenate([hi, hi, lo], axis=1)


def _stack3_rhs(x):
    hi, lo = _split3(x)
    return jnp.concatenate([hi, lo, hi], axis=0)


def _fft1_tile(x, t_ref, twc_ref, tws_ref):
    n1, tn2, _ = x.shape
    xt = jnp.transpose(x, (1, 0, 2))
    y = _dot(t_ref[...], _stack3_rhs(jnp.concatenate([xt[i] for i in range(tn2)], axis=1)))
    cx, sx = y[0:n1], y[n1:]
    w = 2 * C_WIDTH
    outs = []
    for i in range(tn2):
        tc = jnp.concatenate([twc_ref[i]] * (C_WIDTH // LANE), axis=1)
        ts = jnp.concatenate([tws_ref[i]] * (C_WIDTH // LANE), axis=1)
        re, im = slice(i * w, i * w + C_WIDTH), slice(i * w + C_WIDTH, (i + 1) * w)
        ar = cx[:, re] + sx[:, im]
        ai = cx[:, im] - sx[:, re]
        outs.append(jnp.concatenate([ar * tc + ai * ts, ai * tc - ar * ts], axis=1))
    return jnp.transpose(jnp.stack(outs, axis=0), (1, 0, 2))


def _real_dft(t_ref, x):
    rhs = jnp.concatenate([_stack3_rhs(x[:, 0:C_WIDTH]), _stack3_rhs(x[:, C_WIDTH:])], axis=0)
    return _dot(t_ref[...], rhs)


def _fft_kernel(x_ref, t1_ref, twc_ref, tws_ref, t2_ref, o_ref, mid, *, n_a):
    j = pl.program_id(1)
    tn2 = x_ref.shape[2]
    tk1 = o_ref.shape[2]

    @pl.when(j < n_a)
    def _():
        mid[:, pl.ds(pl.multiple_of(j * tn2, tn2), tn2), :] = _fft1_tile(x_ref[0], t1_ref, twc_ref, tws_ref)

    @pl.when(j >= n_a)
    def _():
        off = (j - n_a) * tk1
        outs = [_real_dft(t2_ref, mid[off + i]) for i in range(tk1)]
        o_ref[0] = jnp.transpose(jnp.stack(outs, axis=0), (1, 0, 2))


def _dft_tables(n, norm):
    k = np.arange(n, dtype=np.int64)
    ang = 2.0 * np.pi * ((k[:, None] * k[None, :]) % n).astype(np.float64) / n
    c = jnp.asarray(np.cos(ang) * norm, F32)
    s = jnp.asarray(np.sin(ang) * norm, F32)
    return _stack3_lhs(*_split3(c)), _stack3_lhs(*_split3(s))


def _fourier_latent(z):
    b, l, w = z.shape
    n1, n2 = FFT_N1, l // FFT_N1
    tn2, tk1 = 16, 16
    prod = jnp.arange(n2)[:, None] * jnp.arange(n1)[None, :]
    ang = prod.astype(F32) * (2.0 * np.pi / l)
    twc = jnp.broadcast_to(jnp.cos(ang)[:, :, None], (n2, n1, LANE))
    tws = jnp.broadcast_to(jnp.sin(ang)[:, :, None], (n2, n1, LANE))
    t1 = jnp.concatenate(_dft_tables(n1, 1.0), axis=0)
    t2m = jnp.concatenate(_dft_tables(n2, 1.0 / np.sqrt(l)), axis=1)
    n_a, n_b = n2 // tn2, n1 // tk1
    stage1 = lambda j: jnp.minimum(j, n_a - 1)
    stage2 = lambda j: jnp.maximum(j - n_a, 0)
    out = pl.pallas_call(
        functools.partial(_fft_kernel, n_a=n_a),
        grid=(b, n_a + n_b),
        in_specs=[pl.BlockSpec((1, n1, tn2, w), lambda b, j: (b, 0, stage1(j), 0)),
                  _resident(t1.shape),
                  pl.BlockSpec((tn2, n1, LANE), lambda b, j: (stage1(j), 0, 0)),
                  pl.BlockSpec((tn2, n1, LANE), lambda b, j: (stage1(j), 0, 0)),
                  _resident(t2m.shape)],
        out_specs=pl.BlockSpec((1, n2, tk1, C_WIDTH), lambda b, j: (b, 0, stage2(j), 0)),
        out_shape=jax.ShapeDtypeStruct((b, n2, n1, C_WIDTH), F32),
        scratch_shapes=[pltpu.VMEM((n1, n2, w), F32)],
        compiler_params=_params("parallel", "arbitrary"),
        name="fft",
    )(z.reshape(b, n1, n2, w), t1, twc, tws, t2m)
    return out.reshape(b, l, C_WIDTH)


def _fourier_ctx_kernel(x_ref, t_ref, o_ref):
    o_ref[0] = _real_dft(t_ref, x_ref[0])


def _fourier_ctx(z, *, batch):
    lc = z.shape[1] // batch
    tabs = jnp.concatenate(_dft_tables(lc, 1.0 / np.sqrt(lc)), axis=1)
    return pl.pallas_call(
        _fourier_ctx_kernel,
        grid=(batch,),
        in_specs=[pl.BlockSpec((1, lc, 2 * C_WIDTH), lambda b: (0, b, 0)), _resident(tabs.shape)],
        out_specs=pl.BlockSpec((1, lc, C_WIDTH), lambda b: (0, b, 0)),
        out_shape=jax.ShapeDtypeStruct((1, batch * lc, C_WIDTH), F32),
        compiler_params=_params("parallel"),
        name="fourier_ctx",
    )(z, tabs)


def _rope_tables(length):
    t = jnp.arange(length)
    row = (t // GRID_W).astype(F32)
    col = (t % GRID_W).astype(F32)
    half = HEAD_DIM // 2
    inv = ROPE_THETA ** (-jnp.arange(0, half, 2, dtype=F32) / half)
    ar, ac = row[:, None] * inv, col[:, None] * inv
    cos = jnp.concatenate([jnp.cos(ar), jnp.cos(ar), jnp.cos(ac), jnp.cos(ac)], axis=-1)
    sin = jnp.concatenate([-jnp.sin(ar), jnp.sin(ar), -jnp.sin(ac), jnp.sin(ac)], axis=-1)
    return jnp.tile(cos, (1, 2)), jnp.tile(sin, (1, 2))


def _channel_dft():
    k = np.arange(C_GROUP_DIM)
    ang = 2.0 * np.pi * ((k[:, None] * k[None, :]) % C_GROUP_DIM) / C_GROUP_DIM
    eye = np.eye(C_GROUPS)
    norm = 1.0 / np.sqrt(C_GROUP_DIM)
    cs = np.concatenate([np.kron(eye, np.cos(ang) * norm), np.kron(eye, -np.sin(ang) * norm)], axis=1)
    return _split3(jnp.asarray(cs, F32))


def _block_diag(w):
    g, c, _ = w.shape
    eye = jnp.eye(g, dtype=w.dtype)
    return (eye[:, None, :, None] * w[:, :, None, :]).reshape(g * c, g * c)


def kernel(x, c, ctx, c_ctx, w_ada, b_ada, g_ffn1, g_mix, g_ffn2, w_in, g_qn, g_kn, sink, w_four,
           w_out, w1_gate, w1_up, w1_down, w2_gate, w2_up, w2_down, g_final):
    batch, length, d = x.shape
    lc = ctx.shape[1]
    depth = w_ada.shape[0]
    assert length % (FFT_N1 * 8) == 0 and length >= BLOCK + 2 * WINDOW and batch + 1 <= 8
    tm = 512
    tm_ctx = batch * lc

    cvec = jnp.zeros((8, d), F32).at[:batch].set(c).at[batch].set(c_ctx)
    mod = _ada(cvec, w_ada, b_ada).reshape(depth, 8, N_MOD, d)

    cos, sin = _rope_tables(length)
    ones = jnp.ones((tm_ctx, LANE), F32)
    msm = _block_diag(jnp.full((2, HEAD_DIM, HEAD_DIM), 1.0 / HEAD_DIM, F32)).astype(BF16)
    wc = _wfold(*_channel_dft(), jax.vmap(_block_diag)(w_four))

    w1 = tuple(w.astype(BF16) for w in (w1_gate, w1_up, w1_down))
    win = w_in.astype(BF16)

    h = x
    hc = ctx.reshape(1, batch * lc, d)
    for l in range(depth):
        last = l == depth - 1
        mod_l, mod_c = mod[l, :batch], mod[l, batch:batch + 1]
        gq = jnp.tile(g_qn[l].reshape(1, HEAD_DIM), (1, 2))
        gk = jnp.tile(g_kn[l].reshape(1, HEAD_DIM), (1, 2))
        proj = dict(g=g_mix[l], w_in=win, gq=gq, gk=gk, msm=msm, wc=wc)

        h, qa, ka, vat, qb, kb, vbt, z, *w2, wout = _ffn(
            h, mod_l, g_ffn1[l], *w1, layer=l, row0=0, tm=tm, proj=dict(proj, cos=cos, sin=sin, rope=True),
            cast=(w2_gate, w2_up, w2_down, w_out))
        hc, qa_c, ka_c, vat_c, qb_c, kb_c, vbt_c, z_c = _ffn(
            hc, mod_c, g_ffn1[l], *w1, layer=l, row0=0, tm=tm_ctx, proj=dict(proj, cos=ones, sin=ones, rope=False))

        tq = 512
        sink_l = jnp.repeat(sink[l].astype(F32) * LOG2E, tq).reshape(1, N_HEADS * tq)
        oa, ob = _attn_latent(qa, ka, vat, qb, kb, vbt, ka_c, vat_c, kb_c, vbt_c, sink_l,
                              tq=tq, tk=_key_tile(length + lc), n_sub=1)
        oc = _fourier_latent(z)
        h = _ffn(h, mod_l, g_ffn2[l], *w2, layer=0, row0=6, tm=tm, mix=(oa, ob, oc, wout),
                 final_g=g_final if last else None)

        if not last:
            sink_c = jnp.repeat(sink[l].astype(F32) * LOG2E, lc).reshape(1, N_HEADS * lc)
            oa_c = _attn_ctx(qa_c, ka_c, vat_c, sink_c, batch=batch)
            ob_c = _attn_ctx(qb_c, kb_c, vbt_c, None, batch=batch)
            oc_c = _fourier_ctx(z_c, batch=batch)
            hc = _ffn(hc, mod_c, g_ffn2[l], *w2, layer=0, row0=6, tm=tm_ctx, mix=(oa_c, ob_c, oc_c, wout))
    return h
```

```python
import functools

import jax
import jax.numpy as jnp
import numpy as np
from jax import lax
from jax.experimental import pallas as pl
from jax.experimental.pallas import tpu as pltpu

HEAD_DIM = 64
A_HEADS = 6
A_KV_HEADS = 2
B_HEADS = 6
B_KV_HEADS = 2
C_GROUPS = 4
C_GROUP_DIM = 64
GRID_W = 64
WINDOW = 128
BLOCK = 128
ROPE_THETA = 10000.0
NORM_EPS = 1e-6
N_MOD = 9
NEG_INF = -1e30

N_HEADS = 6
GROUP = 3
C_WIDTH = C_GROUPS * C_GROUP_DIM
FFT_N1 = 64
WIN_TILE = 256
DEN_ROWS = 16
LOG2E = 1.4426950408889634

LANE = 128
VMEM_LIMIT = 60 * 1024 * 1024

F32 = jnp.float32
BF16 = jnp.bfloat16


def _params(*sem):
    return pltpu.CompilerParams(dimension_semantics=tuple(sem), vmem_limit_bytes=VMEM_LIMIT)


def _resident(shape):
    nd = len(shape)
    return pl.BlockSpec(shape, lambda *_: (0,) * nd, pipeline_mode=pl.Buffered(1))


def _layer_resident(stacked, layer):
    nd = stacked.ndim
    return pl.BlockSpec((None,) + stacked.shape[1:], lambda *_: (layer,) + (0,) * (nd - 1),
                        pipeline_mode=pl.Buffered(1))


def _split3(x):
    hi = x.astype(BF16)
    lo = (x - hi.astype(F32)).astype(BF16)
    return hi, lo


def _dot(a, b):
    return jnp.dot(a, b, preferred_element_type=F32)


def _dot3(a_hi, a_lo, b_hi, b_lo):
    return _dot(a_hi, b_hi) + _dot(a_hi, b_lo) + _dot(a_lo, b_hi)


def _silu(x):
    return x / (1.0 + jnp.exp(-x))


def _norm_mod(x, g, shift, scale):
    ms = jnp.mean(x * x, axis=-1, keepdims=True)
    return (x * lax.rsqrt(ms + NORM_EPS)) * (g * (1.0 + scale)) + shift


def _ada_kernel(c_ref, w_ref, b_ref, o_ref):
    s_hi, s_lo = _split3(_silu(c_ref[...]))
    w_hi, w_lo = _split3(w_ref[0])
    both = _dot(jnp.concatenate([s_hi, s_lo], axis=0), w_hi)
    rows = s_hi.shape[0]
    o_ref[0] = both[0:rows] + both[rows:] + _dot(s_hi, w_lo) + b_ref[0]


def _ada(cvec, w_ada, b_ada):
    depth, d, nd = w_ada.shape
    tn = 1024
    return pl.pallas_call(
        _ada_kernel,
        grid=(depth, nd // tn),
        in_specs=[
            pl.BlockSpec((8, d), lambda l, j: (0, 0)),
            pl.BlockSpec((1, d, tn), lambda l, j: (l, 0, j)),
            pl.BlockSpec((1, 1, tn), lambda l, j: (l, 0, j)),
        ],
        out_specs=pl.BlockSpec((1, 8, tn), lambda l, j: (l, 0, j)),
        out_shape=jax.ShapeDtypeStruct((depth, 8, nd), F32),
        compiler_params=_params("parallel", "parallel"),
        name="ada",
    )(cvec, w_ada, b_ada.reshape(depth, 1, nd))


def _mix_out(oa_ref, ob_ref, oc_ref, w_ref):
    na = N_HEADS * HEAD_DIM
    return (_dot(oa_ref[0], w_ref[0:na, :]) + _dot(ob_ref[0], w_ref[na:2 * na, :])
            + _dot(oc_ref[0].astype(BF16), w_ref[2 * na:, :]))


N_PROJ_IN = 8


def _ffn_kernel(*refs, row0, final, mixed, rope, n_cast):
    refs = list(refs)
    if n_cast:
        cast_out = refs[-n_cast:]
        del refs[-n_cast:]
    x_ref, mod_ref, g_ref, wg_ref, wu_ref, wd_ref = refs[:6]
    del refs[:6]
    if mixed:
        oa_ref, ob_ref, oc_ref, wo_ref = refs[:4]
        del refs[:4]
    if final:
        gf_ref = refs.pop(0)
    proj_in = refs[:N_PROJ_IN] if rope is not None else []
    del refs[:len(proj_in)]
    cast_in = refs[:n_cast]
    o_ref = refs[n_cast]
    proj_out = refs[n_cast + 1:]
    shift = mod_ref[0, row0:row0 + 1, :]
    scale = mod_ref[0, row0 + 1:row0 + 2, :]
    gate = mod_ref[0, row0 + 2:row0 + 3, :]
    x = x_ref[0]
    if mixed:
        x = x + mod_ref[0, 5:6, :] * _mix_out(oa_ref, ob_ref, oc_ref, wo_ref)
    xm = _norm_mod(x, g_ref[...], shift, scale).astype(BF16)
    a = _dot(xm, wg_ref[...])
    u = _dot(xm, wu_ref[...])
    act = (_silu(a) * u).astype(BF16)
    y = x + (0.5 * gate) * _dot(act, wd_ref[...])
    if final:
        ms = jnp.mean(y * y, axis=-1, keepdims=True)
        y = (y * lax.rsqrt(ms + NORM_EPS)) * gf_ref[...]
    o_ref[0] = y
    if rope is not None:
        _proj_body(y, mod_ref, *proj_in, *proj_out, rope=rope)
    for src, dst in zip(cast_in, cast_out if n_cast else []):
        dst[0] = src[...].astype(BF16)


def _ffn(x, mod, g, wg, wu, wd, *, layer, row0, tm, mix=None, final_g=None, proj=None, cast=()):
    groups, t, d = x.shape
    final = final_g is not None
    tok = lambda b, i: (b, i, 0)
    in_specs = [
        pl.BlockSpec((1, tm, d), tok),
        pl.BlockSpec((1, N_MOD, d), lambda b, i: (b, 0, 0)),
        _resident((1, d)),
        _layer_resident(wg, layer),
        _layer_resident(wu, layer),
        _layer_resident(wd, layer),
    ]
    args = [x, mod, g.reshape(1, d), wg, wu, wd]
    if mix is not None:
        oa, ob, oc, w_out = mix
        in_specs += [pl.BlockSpec((1, tm, oa.shape[2]), tok), pl.BlockSpec((1, tm, ob.shape[2]), tok),
                     pl.BlockSpec((1, tm, oc.shape[2]), tok), _layer_resident(w_out, layer)]
        args += [oa, ob, oc, w_out]
    if final:
        in_specs.append(_resident((1, d)))
        args.append(final_g.reshape(1, d))
    out_specs = [pl.BlockSpec((1, tm, d), tok)]
    out_shape = [jax.ShapeDtypeStruct((groups, t, d), F32)]
    if proj is not None:
        in_specs += [
            _resident((1, d)),
            _layer_resident(proj["w_in"], layer),
            pl.BlockSpec((tm, LANE), lambda b, i: (i, 0)),
            pl.BlockSpec((tm, LANE), lambda b, i: (i, 0)),
            _resident((1, LANE)),
            _resident((1, LANE)),
            _resident((LANE, LANE)),
            _layer_resident(proj["wc"], layer),
        ]
        args += [proj["g"].reshape(1, d), proj["w_in"], proj["cos"], proj["sin"], proj["gq"], proj["gk"],
                 proj["msm"], proj["wc"]]
        q_spec = pl.BlockSpec((1, N_HEADS, tm, LANE), lambda b, i: (b, 0, i, 0))
        k_spec = pl.BlockSpec((1, tm, LANE), tok)
        vt_spec = pl.BlockSpec((1, LANE, tm), lambda b, i: (b, 0, i))
        q_shape = jax.ShapeDtypeStruct((groups, N_HEADS, t, LANE), BF16)
        k_shape = jax.ShapeDtypeStruct((groups, t, LANE), BF16)
        vt_shape = jax.ShapeDtypeStruct((groups, LANE, t), BF16)
        out_specs += [q_spec, k_spec, vt_spec, q_spec, k_spec, vt_spec, pl.BlockSpec((1, tm, 2 * C_WIDTH), tok)]
        out_shape += [q_shape, k_shape, vt_shape, q_shape, k_shape, vt_shape,
                      jax.ShapeDtypeStruct((groups, t, 2 * C_WIDTH), F32)]
    n_tiles = t // tm
    for w in cast:
        _, r, c = w.shape
        if r % (groups * n_tiles * 16) == 0:
            block, index = (r // (groups * n_tiles), c), lambda b, i: (b * n_tiles + i, 0)
        else:
            block, index = (r // n_tiles, c // groups), lambda b, i: (i, b)
        assert block[0] % 16 == 0 and block[1] % LANE == 0
        in_specs.append(pl.BlockSpec((None,) + block, lambda b, i, index=index: (layer,) + index(b, i)))
        args.append(w)
        out_specs.append(pl.BlockSpec((1,) + block, lambda b, i, index=index: (0,) + index(b, i)))
        out_shape.append(jax.ShapeDtypeStruct((1, r, c), BF16))
    out = pl.pallas_call(
        functools.partial(_ffn_kernel, row0=row0, final=final, mixed=mix is not None,
                          rope=None if proj is None else proj["rope"], n_cast=len(cast)),
        grid=(groups, t // tm),
        in_specs=in_specs,
        out_specs=out_specs,
        out_shape=out_shape,
        compiler_params=_params("parallel", "parallel"),
        name="ffn",
    )(*args)
    return out[0] if len(out) == 1 else out


def _rope_slab(t, cos, sin, lane):
    fwd = pltpu.roll(t, LANE - 16, axis=1)
    bwd = pltpu.roll(t, 16, axis=1)
    partner = jnp.where((lane % 32) < 16, fwd, bwd)
    return t * cos + partner * sin


def _wfold_kernel(cs_hi_ref, cs_lo_ref, w_ref, o_ref):
    w_hi, w_lo = _split3(w_ref[0])
    for half in range(2):
        sl = slice(half * C_WIDTH, (half + 1) * C_WIDTH)
        o_ref[0, :, sl] = _dot3(cs_hi_ref[:, sl], cs_lo_ref[:, sl], w_hi, w_lo).astype(BF16)


def _wfold(cs_hi, cs_lo, wbd):
    depth = wbd.shape[0]
    return pl.pallas_call(
        _wfold_kernel,
        grid=(depth,),
        in_specs=[_resident(cs_hi.shape), _resident(cs_lo.shape),
                  pl.BlockSpec((1, C_WIDTH, C_WIDTH), lambda l: (l, 0, 0))],
        out_specs=pl.BlockSpec((1, C_WIDTH, 2 * C_WIDTH), lambda l: (l, 0, 0)),
        out_shape=jax.ShapeDtypeStruct((depth, C_WIDTH, 2 * C_WIDTH), BF16),
        compiler_params=_params("parallel"),
        name="wfold",
    )(cs_hi, cs_lo, wbd)


def _proj_body(x, mod_ref, g_ref, w_ref, cos_ref, sin_ref, gq_ref, gk_ref, msm_ref, wc_ref,
               qa_ref, ka_ref, vat_ref, qb_ref, kb_ref, vbt_ref, z_ref, *, rope):
    tm = x.shape[0]
    shift = mod_ref[0, 3:4, :]
    scale = mod_ref[0, 4:5, :]
    xm = _norm_mod(x, g_ref[...], shift, scale).astype(BF16)

    u_all = _dot(xm, w_ref[...])

    def u(lo, hi):
        return u_all[:, lo:hi]

    lane = lax.broadcasted_iota(jnp.int32, (tm, LANE), 1)
    low = lane < HEAD_DIM
    if rope:
        cos = cos_ref[...]
        sin = sin_ref[...]

    def qk_norm(t, gain):
        ms = _dot((t * t).astype(BF16), msm_ref[...])
        return t * lax.rsqrt(ms + NORM_EPS) * gain

    def finish(t):
        return _rope_slab(t, cos, sin, lane) if rope else t

    def emit_q(q_ref, slabs):
        for h in range(N_HEADS):
            t = slabs[h // 2] * (LOG2E * HEAD_DIM ** -0.5)
            if (h % 2) != (h // GROUP):
                t = pltpu.roll(t, HEAD_DIM, axis=1)
            keep = low if (h // GROUP) == 0 else jnp.logical_not(low)
            q_ref[0, h] = jnp.where(keep, t, 0.0).astype(BF16)

    qa = [finish(u(LANE * i, LANE * (i + 1))) for i in range(3)]
    emit_q(qa_ref, qa)
    ka_ref[0] = finish(u(384, 512)).astype(BF16)
    vat_ref[0] = u(512, 640).T.astype(BF16)

    qb = [finish(qk_norm(u(640 + LANE * i, 640 + LANE * (i + 1)), gq_ref[...])) for i in range(3)]
    emit_q(qb_ref, qb)
    kb_ref[0] = finish(qk_norm(u(1024, 1152), gk_ref[...])).astype(BF16)
    vbt_ref[0] = u(1152, 1280).T.astype(BF16)

    z_ref[0] = _dot(u(1280, 1536).astype(BF16), wc_ref[...])


def _nt(a, b):
    return lax.dot_general(a, b, (((1,), (1,)), ((), ())), preferred_element_type=F32)


def _store_heads(o_ref, ot, r0=0):
    tq = ot[0].shape[1]
    for pair in range(N_HEADS // 2):
        slab = jnp.concatenate([ot[2 * pair], ot[2 * pair + 1]], axis=0)
        o_ref[0, r0:r0 + tq, pair * LANE:(pair + 1) * LANE] = slab.T.astype(o_ref.dtype)


def _attn_latent_kernel(*refs, tq, tk):
    s_a, s_b, p_a, p_b = refs[13:17]
    n_sub = refs[1].shape[2] // tq
    tiles = [_tile_stages(sub, n_sub, *refs, tq=tq, tk=tk) for sub in range(n_sub)]
    nk = tiles[0]["nk"]
    n_steps = nk + 1
    total = n_sub * n_steps
    bufs = ((s_a, p_a), (s_b, p_b))

    def issue_scores(u):
        j, t = divmod(u, n_steps)
        s_buf, slot = bufs[u % 2][0], u % 2
        if t == 0:
            tiles[j]["scores_win"](s_buf, slot)
        else:
            tiles[j]["scores"](t - 1, s_buf, slot)

    def do_softmax(u):
        j, t = divmod(u, n_steps)
        (s_buf, p_buf), slot = bufs[u % 2], u % 2
        tiles[j]["softmax_win" if t == 0 else "softmax"](s_buf, p_buf, slot)

    def do_values(u):
        j, t = divmod(u, n_steps)
        p_buf, slot = bufs[u % 2][1], u % 2
        if t == 0:
            tiles[j]["values_win"](p_buf, slot)
        else:
            tiles[j]["values"](t - 1, p_buf, slot)
        if t == n_steps - 1:
            tiles[j]["finish"]()

    def flat_step(u):
        if u + 1 < total:
            issue_scores(u + 1)
        do_softmax(u)
        if u >= 1:
            do_values(u - 1)

    n_pairs = max(0, (nk - 3) // 2)
    issue_scores(0)
    for j in range(n_sub):
        base = j * n_steps
        flat_step(base)
        flat_step(base + 1)

        def pair(i, carry, tile=tiles[j], base=base):
            t = 2 + 2 * i
            for dt in (0, 1):
                par = (base + dt) % 2
                (s_cur, p_cur), (s_oth, p_oth) = bufs[par], bufs[1 - par]
                tile["scores"](t + dt, s_oth, 1 - par)
                tile["softmax"](s_cur, p_cur, par)
                tile["values"](t + dt - 2, p_oth, 1 - par)
            return carry

        lax.fori_loop(0, n_pairs, pair, 0)
        for t in range(2 + 2 * n_pairs, n_steps):
            flat_step(base + t)
    do_values(total - 1)


def _tile_stages(sub, n_sub, qa_ref, qb_ref, ka_ref, vat_ref, kb_ref, vbt_ref, kac_ref, vatc_ref, kbc_ref,
                 vbtc_ref, sink_ref, oa_ref, ob_ref, s_a, s_b, p_a, p_b, m_sc, bmax_sc, alpha_sc, acc_sc, *, tq, tk):
    length = kb_ref.shape[1]
    lc = kbc_ref.shape[1]
    nk = (length + lc) // tk
    tail = length - (nk - 1) * tk
    wq = min(tq, WIN_TILE)
    span = wq + 2 * WINDOW
    na = span + lc
    tile = pl.program_id(1) * n_sub + sub
    r0 = sub * tq
    m_sc, acc_sc = m_sc.at[sub], acc_sc.at[sub]
    starts = [pl.multiple_of(jnp.clip(tile * tq + i * wq - WINDOW, 0, length - span), LANE)
              for i in range(tq // wq)]
    m_sc[...] = jnp.full_like(m_sc, NEG_INF)
    acc_sc[...] = jnp.zeros_like(acc_sc)

    def with_ones(vt):
        return jnp.concatenate([vt, jnp.ones((DEN_ROWS, vt.shape[1]), BF16)], axis=0)

    def rows(g):
        return slice(g * HEAD_DIM, (g + 1) * HEAD_DIM)

    def sub_cols(h, i):
        return slice(h * tq + i * wq, h * tq + (i + 1) * wq)

    def scores_win(s_buf, slot):
        for i, start in enumerate(starts):
            kw = ka_ref[0, pl.ds(start, span), :]
            kpos = start + lax.broadcasted_iota(jnp.int32, (span, wq), 0)
            qpos = tile * tq + i * wq + lax.broadcasted_iota(jnp.int32, (span, wq), 1)
            valid = jnp.abs(kpos - qpos) <= WINDOW
            for h in range(N_HEADS):
                cs = sub_cols(h, i)
                q = qa_ref[0, h, r0 + i * wq:r0 + (i + 1) * wq, :]
                s_w = jnp.where(valid, _nt(kw, q), NEG_INF)
                s_c = _nt(kac_ref[0], q)
                s_buf[0:span, cs] = s_w
                s_buf[span:na, cs] = s_c
                bmax_sc[slot, :, cs] = jnp.maximum(jnp.max(s_w, axis=0, keepdims=True),
                                                   jnp.max(s_c, axis=0, keepdims=True))

    def softmax_win(s_buf, p_buf, slot):
        for h in range(N_HEADS):
            hc = head_cols(h)
            sink = sink_ref[:, hc]
            m = jnp.maximum(sink, bmax_sc[slot, :, hc])
            alpha_sc[slot, :, hc] = jnp.exp2(sink - m)
            p_buf[0:na, hc] = jnp.exp2(s_buf[0:na, hc] - m).astype(BF16)

    def values_win(p_buf, slot):
        parts = [[None] * len(starts) for _ in range(N_HEADS)]
        for g in range(2):
            for i, start in enumerate(starts):
                vt = with_ones(jnp.concatenate(
                    [vat_ref[0, rows(g), pl.ds(start, span)], vatc_ref[0, rows(g), :]], axis=1))
                for h in range(g * GROUP, (g + 1) * GROUP):
                    cs = sub_cols(h, i)
                    acc = _dot(vt, p_buf[0:na, cs])
                    parts[h][i] = acc[0:HEAD_DIM] / (acc[HEAD_DIM:HEAD_DIM + 1] + alpha_sc[slot, :, cs])
        _store_heads(oa_ref, [jnp.concatenate(p, axis=1) for p in parts], r0)

    def key_block(blk):
        if isinstance(blk, int) and blk == nk - 1:
            return jnp.concatenate([kb_ref[0, length - tail:, :], kbc_ref[0]], axis=0)
        return kb_ref[0, pl.ds(pl.multiple_of(blk * tk, tk), tk), :]

    def value_block(blk, g):
        if isinstance(blk, int) and blk == nk - 1:
            return jnp.concatenate([vbt_ref[0, rows(g), length - tail:], vbtc_ref[0, rows(g), :]], axis=1)
        return vbt_ref[0, rows(g), pl.ds(pl.multiple_of(blk * tk, tk), tk)]

    def head_cols(h):
        return slice(h * tq, (h + 1) * tq)

    def scores(blk, s_buf, slot):
        kb = key_block(blk)
        for h in range(N_HEADS):
            s = _nt(kb, qb_ref[0, h, r0:r0 + tq, :])
            s_buf[0:tk, head_cols(h)] = s
            bmax_sc[slot, :, head_cols(h)] = jnp.max(s, axis=0, keepdims=True)

    def softmax(s_buf, p_buf, slot):
        for h in range(N_HEADS):
            hc = head_cols(h)
            m_old = m_sc[:, hc]
            m_new = jnp.maximum(m_old, bmax_sc[slot, :, hc])
            m_sc[:, hc] = m_new
            alpha_sc[slot, :, hc] = jnp.exp2(m_old - m_new)
            p_buf[0:tk, hc] = jnp.exp2(s_buf[0:tk, hc] - m_new).astype(BF16)

    def values(blk, p_buf, slot):
        for g in range(2):
            vt = with_ones(value_block(blk, g))
            for h in range(g * GROUP, (g + 1) * GROUP):
                hc = head_cols(h)
                acc_sc[h] = acc_sc[h] * alpha_sc[slot, :, hc] + _dot(vt, p_buf[0:tk, hc])

    def finish():
        _store_heads(ob_ref, [acc_sc[h, 0:HEAD_DIM] / acc_sc[h, HEAD_DIM:HEAD_DIM + 1] for h in range(N_HEADS)], r0)

    return dict(nk=nk, scores_win=scores_win, softmax_win=softmax_win, values_win=values_win,
                scores=scores, softmax=softmax, values=values, finish=finish)


def _key_tile(n_keys, cap=1024):
    return max(t for t in range(LANE, cap + 1, LANE) if n_keys % t == 0)


def _attn_latent(qa, ka, vat, qb, kb, vbt, ka_c, vat_c, kb_c, vbt_c, sink_row, *, tq, tk, n_sub):
    b, _, l, _ = qb.shape
    lc = ka_c.shape[1] // b
    assert (l + lc) % tk == 0 and (l + lc) // tk * tk - tk <= l and tk > lc
    buf_rows = max(tk, min(tq, WIN_TILE) + 2 * WINDOW + lc)
    buf_cols = N_HEADS * tq
    q_spec = pl.BlockSpec((1, N_HEADS, n_sub * tq, LANE), lambda b, i: (b, 0, i, 0))
    k_spec = pl.BlockSpec((1, l, LANE), lambda b, i: (b, 0, 0), pipeline_mode=pl.Buffered(1))
    vt_spec = pl.BlockSpec((1, LANE, l), lambda b, i: (b, 0, 0), pipeline_mode=pl.Buffered(1))
    kc_spec = pl.BlockSpec((1, lc, LANE), lambda b, i: (0, b, 0), pipeline_mode=pl.Buffered(1))
    vtc_spec = pl.BlockSpec((1, LANE, lc), lambda b, i: (0, 0, b), pipeline_mode=pl.Buffered(1))
    o_spec = pl.BlockSpec((1, n_sub * tq, N_HEADS * HEAD_DIM), lambda b, i: (b, i, 0))
    o_shape = jax.ShapeDtypeStruct((b, l, N_HEADS * HEAD_DIM), BF16)
    return pl.pallas_call(
        functools.partial(_attn_latent_kernel, tq=tq, tk=tk),
        grid=(b, l // (n_sub * tq)),
        in_specs=[q_spec, q_spec, k_spec, vt_spec, k_spec, vt_spec, kc_spec, vtc_spec, kc_spec, vtc_spec,
                  pl.BlockSpec((1, N_HEADS * tq), lambda b, i: (0, 0))],
        out_specs=[o_spec, o_spec],
        out_shape=[o_shape, o_shape],
        scratch_shapes=[
            pltpu.VMEM((buf_rows, buf_cols), F32),
            pltpu.VMEM((buf_rows, buf_cols), F32),
            pltpu.VMEM((buf_rows, buf_cols), BF16),
            pltpu.VMEM((buf_rows, buf_cols), BF16),
            pltpu.VMEM((n_sub, 1, N_HEADS * tq), F32),
            pltpu.VMEM((2, 1, N_HEADS * tq), F32),
            pltpu.VMEM((2, 1, N_HEADS * tq), F32),
            pltpu.VMEM((n_sub, N_HEADS, HEAD_DIM + DEN_ROWS, tq), F32),
        ],
        compiler_params=_params("parallel", "parallel"),
        name="attn_latent",
    )(qa, qb, ka, vat, kb, vbt, ka_c, vat_c, kb_c, vbt_c, sink_row)


def _softmax_pv(q_ref, ks, vts, extra, mask=None):
    tq = q_ref.shape[2]
    aug = [[jnp.concatenate([vt[g * HEAD_DIM:(g + 1) * HEAD_DIM, :],
                             jnp.ones((DEN_ROWS, vt.shape[1]), BF16)], axis=0) for vt in vts]
           for g in range(2)]
    out = []
    for h in range(N_HEADS):
        ss = [_nt(k, q_ref[0, h]) for k in ks]
        if mask is not None:
            ss[0] = jnp.where(mask, ss[0], NEG_INF)
        m = functools.reduce(jnp.maximum, [jnp.max(s, axis=0, keepdims=True) for s in ss])
        if extra is not None:
            e = extra[:, h * tq:(h + 1) * tq]
            m = jnp.maximum(m, e)
        acc = functools.reduce(
            jnp.add, [_dot(vt, jnp.exp2(s - m).astype(BF16)) for s, vt in zip(ss, aug[h // GROUP])])
        den = acc[HEAD_DIM:HEAD_DIM + 1]
        if extra is not None:
            den = den + jnp.exp2(e - m)
        out.append(acc[0:HEAD_DIM] / den)
    return out


def _attn_ctx_kernel(q_ref, k_ref, vt_ref, *rest, has_sink):
    if has_sink:
        sink_ref, o_ref = rest
    else:
        (o_ref,) = rest
    extra = sink_ref[...] if has_sink else None
    _store_heads(o_ref, _softmax_pv(q_ref, [k_ref[0]], [vt_ref[0]], extra))


def _attn_ctx(q, k, vt, sink_row, *, batch):
    lc = q.shape[2] // batch
    in_specs = [
        pl.BlockSpec((1, N_HEADS, lc, LANE), lambda b: (0, 0, b, 0)),
        pl.BlockSpec((1, lc, LANE), lambda b: (0, b, 0)),
        pl.BlockSpec((1, LANE, lc), lambda b: (0, 0, b)),
    ]
    args = [q, k, vt]
    if sink_row is not None:
        in_specs.append(pl.BlockSpec((1, N_HEADS * lc), lambda b: (0, 0)))
        args.append(sink_row)
    return pl.pallas_call(
        functools.partial(_attn_ctx_kernel, has_sink=sink_row is not None),
        grid=(batch,),
        in_specs=in_specs,
        out_specs=pl.BlockSpec((1, lc, N_HEADS * HEAD_DIM), lambda b: (0, b, 0)),
        out_shape=jax.ShapeDtypeStruct((1, batch * lc, N_HEADS * HEAD_DIM), BF16),
        compiler_params=_params("parallel"),
        name="attn_ctx",
    )(*args)


def _stack3_lhs(hi, lo):
    return jnp.concatenate([hi, hi, lo], axis=1)


def _stack3_rhs(x):
    hi, lo = _split3(x)
    return jnp.concatenate([hi, lo, hi], axis=0)


def _fft1_tile(x, t_ref, twc_ref, tws_ref):
    n1, tn2, _ = x.shape
    xt = jnp.transpose(x, (1, 0, 2))
    y = _dot(t_ref[...], _stack3_rhs(jnp.concatenate([xt[i] for i in range(tn2)], axis=1)))
    cx, sx = y[0:n1], y[n1:]
    w = 2 * C_WIDTH
    outs = []
    for i in range(tn2):
        tc = jnp.concatenate([twc_ref[i]] * (C_WIDTH // LANE), axis=1)
        ts = jnp.concatenate([tws_ref[i]] * (C_WIDTH // LANE), axis=1)
        re, im = slice(i * w, i * w + C_WIDTH), slice(i * w + C_WIDTH, (i + 1) * w)
        ar = cx[:, re] + sx[:, im]
        ai = cx[:, im] - sx[:, re]
        outs.append(jnp.concatenate([ar * tc + ai * ts, ai * tc - ar * ts], axis=1))
    return jnp.transpose(jnp.stack(outs, axis=0), (1, 0, 2))


def _real_dft(t_ref, x):
    rhs = jnp.concatenate([_stack3_rhs(x[:, 0:C_WIDTH]), _stack3_rhs(x[:, C_WIDTH:])], axis=0)
    return _dot(t_ref[...], rhs)


def _fft_kernel(x_ref, t1_ref, twc_ref, tws_ref, t2_ref, o_ref, mid, *, n_a):
    j = pl.program_id(1)
    tn2 = x_ref.shape[2]
    tk1 = o_ref.shape[2]

    @pl.when(j < n_a)
    def _():
        mid[:, pl.ds(pl.multiple_of(j * tn2, tn2), tn2), :] = _fft1_tile(x_ref[0], t1_ref, twc_ref, tws_ref)

    @pl.when(j >= n_a)
    def _():
        off = (j - n_a) * tk1
        outs = [_real_dft(t2_ref, mid[off + i]) for i in range(tk1)]
        o_ref[0] = jnp.transpose(jnp.stack(outs, axis=0), (1, 0, 2)).astype(o_ref.dtype)


def _dft_tables(n, norm):
    k = np.arange(n, dtype=np.int64)
    ang = 2.0 * np.pi * ((k[:, None] * k[None, :]) % n).astype(np.float64) / n
    c = jnp.asarray(np.cos(ang) * norm, F32)
    s = jnp.asarray(np.sin(ang) * norm, F32)
    return _stack3_lhs(*_split3(c)), _stack3_lhs(*_split3(s))


def _fourier_latent(z):
    b, l, w = z.shape
    n1, n2 = FFT_N1, l // FFT_N1
    tn2, tk1 = 16, 16
    prod = jnp.arange(n2)[:, None] * jnp.arange(n1)[None, :]
    ang = prod.astype(F32) * (2.0 * np.pi / l)
    twc = jnp.broadcast_to(jnp.cos(ang)[:, :, None], (n2, n1, LANE))
    tws = jnp.broadcast_to(jnp.sin(ang)[:, :, None], (n2, n1, LANE))
    t1 = jnp.concatenate(_dft_tables(n1, 1.0), axis=0)
    t2m = jnp.concatenate(_dft_tables(n2, 1.0 / np.sqrt(l)), axis=1)
    n_a, n_b = n2 // tn2, n1 // tk1
    stage1 = lambda j: jnp.minimum(j, n_a - 1)
    stage2 = lambda j: jnp.maximum(j - n_a, 0)
    out = pl.pallas_call(
        functools.partial(_fft_kernel, n_a=n_a),
        grid=(b, n_a + n_b),
        in_specs=[pl.BlockSpec((1, n1, tn2, w), lambda b, j: (b, 0, stage1(j), 0)),
                  _resident(t1.shape),
                  pl.BlockSpec((tn2, n1, LANE), lambda b, j: (stage1(j), 0, 0)),
                  pl.BlockSpec((tn2, n1, LANE), lambda b, j: (stage1(j), 0, 0)),
                  _resident(t2m.shape)],
        out_specs=pl.BlockSpec((1, n2, tk1, C_WIDTH), lambda b, j: (b, 0, stage2(j), 0)),
        out_shape=jax.ShapeDtypeStruct((b, n2, n1, C_WIDTH), BF16),
        scratch_shapes=[pltpu.VMEM((n1, n2, w), F32)],
        compiler_params=_params("parallel", "arbitrary"),
        name="fft",
    )(z.reshape(b, n1, n2, w), t1, twc, tws, t2m)
    return out.reshape(b, l, C_WIDTH)


def _fourier_ctx_kernel(x_ref, t_ref, o_ref):
    o_ref[0] = _real_dft(t_ref, x_ref[0]).astype(o_ref.dtype)


def _fourier_ctx(z, *, batch):
    lc = z.shape[1] // batch
    tabs = jnp.concatenate(_dft_tables(lc, 1.0 / np.sqrt(lc)), axis=1)
    return pl.pallas_call(
        _fourier_ctx_kernel,
        grid=(batch,),
        in_specs=[pl.BlockSpec((1, lc, 2 * C_WIDTH), lambda b: (0, b, 0)), _resident(tabs.shape)],
        out_specs=pl.BlockSpec((1, lc, C_WIDTH), lambda b: (0, b, 0)),
        out_shape=jax.ShapeDtypeStruct((1, batch * lc, C_WIDTH), BF16),
        compiler_params=_params("parallel"),
        name="fourier_ctx",
    )(z, tabs)


def _rope_tables(length):
    t = jnp.arange(length)
    row = (t // GRID_W).astype(F32)
    col = (t % GRID_W).astype(F32)
    half = HEAD_DIM // 2
    inv = ROPE_THETA ** (-jnp.arange(0, half, 2, dtype=F32) / half)
    ar, ac = row[:, None] * inv, col[:, None] * inv
    cos = jnp.concatenate([jnp.cos(ar), jnp.cos(ar), jnp.cos(ac), jnp.cos(ac)], axis=-1)
    sin = jnp.concatenate([-jnp.sin(ar), jnp.sin(ar), -jnp.sin(ac), jnp.sin(ac)], axis=-1)
    return jnp.tile(cos, (1, 2)), jnp.tile(sin, (1, 2))


def _channel_dft():
    k = np.arange(C_GROUP_DIM)
    ang = 2.0 * np.pi * ((k[:, None] * k[None, :]) % C_GROUP_DIM) / C_GROUP_DIM
    eye = np.eye(C_GROUPS)
    norm = 1.0 / np.sqrt(C_GROUP_DIM)
    cs = np.concatenate([np.kron(eye, np.cos(ang) * norm), np.kron(eye, -np.sin(ang) * norm)], axis=1)
    return _split3(jnp.asarray(cs, F32))


def _block_diag(w):
    g, c, _ = w.shape
    eye = jnp.eye(g, dtype=w.dtype)
    return (eye[:, None, :, None] * w[:, :, None, :]).reshape(g * c, g * c)


def kernel(x, c, ctx, c_ctx, w_ada, b_ada, g_ffn1, g_mix, g_ffn2, w_in, g_qn, g_kn, sink, w_four,
           w_out, w1_gate, w1_up, w1_down, w2_gate, w2_up, w2_down, g_final):
    batch, length, d = x.shape
    lc = ctx.shape[1]
    depth = w_ada.shape[0]
    assert length % (FFT_N1 * 8) == 0 and length >= BLOCK + 2 * WINDOW and batch + 1 <= 8
    tm = 512
    tm_ctx = batch * lc

    cvec = jnp.zeros((8, d), F32).at[:batch].set(c).at[batch].set(c_ctx)
    mod = _ada(cvec, w_ada, b_ada).reshape(depth, 8, N_MOD, d)

    cos, sin = _rope_tables(length)
    ones = jnp.ones((tm_ctx, LANE), F32)
    msm = _block_diag(jnp.full((2, HEAD_DIM, HEAD_DIM), 1.0 / HEAD_DIM, F32)).astype(BF16)
    wc = _wfold(*_channel_dft(), jax.vmap(_block_diag)(w_four))

    w1 = tuple(w.astype(BF16) for w in (w1_gate, w1_up, w1_down))
    win = w_in.astype(BF16)

    h = x
    hc = ctx.reshape(1, batch * lc, d)
    for l in range(depth):
        last = l == depth - 1
        mod_l, mod_c = mod[l, :batch], mod[l, batch:batch + 1]
        gq = jnp.tile(g_qn[l].reshape(1, HEAD_DIM), (1, 2))
        gk = jnp.tile(g_kn[l].reshape(1, HEAD_DIM), (1, 2))
        proj = dict(g=g_mix[l], w_in=win, gq=gq, gk=gk, msm=msm, wc=wc)

        h, qa, ka, vat, qb, kb, vbt, z, *w2, wout = _ffn(
            h, mod_l, g_ffn1[l], *w1, layer=l, row0=0, tm=tm, proj=dict(proj, cos=cos, sin=sin, rope=True),
            cast=(w2_gate, w2_up, w2_down, w_out))
        hc, qa_c, ka_c, vat_c, qb_c, kb_c, vbt_c, z_c = _ffn(
            hc, mod_c, g_ffn1[l], *w1, layer=l, row0=0, tm=tm_ctx, proj=dict(proj, cos=ones, sin=ones, rope=False))

        tq = 512
        sink_l = jnp.repeat(sink[l].astype(F32) * LOG2E, tq).reshape(1, N_HEADS * tq)
        oa, ob = _attn_latent(qa, ka, vat, qb, kb, vbt, ka_c, vat_c, kb_c, vbt_c, sink_l,
                              tq=tq, tk=_key_tile(length + lc), n_sub=1)
        oc = _fourier_latent(z)
        h = _ffn(h, mod_l, g_ffn2[l], *w2, layer=0, row0=6, tm=tm, mix=(oa, ob, oc, wout),
                 final_g=g_final if last else None)

        if not last:
            sink_c = jnp.repeat(sink[l].astype(F32) * LOG2E, lc).reshape(1, N_HEADS * lc)
            oa_c = _attn_ctx(qa_c, ka_c, vat_c, sink_c, batch=batch)
            ob_c = _attn_ctx(qb_c, kb_c, vbt_c, None, batch=batch)
            oc_c = _fourier_ctx(z_c, batch=batch)
            hc = _ffn(hc, mod_c, g_ffn2[l], *w2, layer=0, row0=6, tm=tm_ctx, mix=(oa_c, ob_c, oc_c, wout))
    return h
```

```python
import functools

import jax
import jax.numpy as jnp
import numpy as np
from jax import lax
from jax.experimental import pallas as pl
from jax.experimental.pallas import tpu as pltpu

HEAD_DIM = 64
A_HEADS = 6
A_KV_HEADS = 2
B_HEADS = 6
B_KV_HEADS = 2
C_GROUPS = 4
C_GROUP_DIM = 64
GRID_W = 64
WINDOW = 128
BLOCK = 128
ROPE_THETA = 10000.0
NORM_EPS = 1e-6
N_MOD = 9
NEG_INF = -1e30

N_HEADS = 6
GROUP = 3
C_WIDTH = C_GROUPS * C_GROUP_DIM
FFT_N1 = 64
WIN_TILE = 256
DEN_ROWS = 16
LOG2E = 1.4426950408889634

LANE = 128
VMEM_LIMIT = 60 * 1024 * 1024

F32 = jnp.float32
BF16 = jnp.bfloat16


def _params(*sem):
    return pltpu.CompilerParams(dimension_semantics=tuple(sem), vmem_limit_bytes=VMEM_LIMIT)


def _resident(shape):
    nd = len(shape)
    return pl.BlockSpec(shape, lambda *_: (0,) * nd, pipeline_mode=pl.Buffered(1))


def _layer_resident(stacked, layer):
    nd = stacked.ndim
    return pl.BlockSpec((None,) + stacked.shape[1:], lambda *_: (layer,) + (0,) * (nd - 1),
                        pipeline_mode=pl.Buffered(1))


def _split3(x):
    hi = x.astype(BF16)
    lo = (x - hi.astype(F32)).astype(BF16)
    return hi, lo


def _dot(a, b):
    return jnp.dot(a, b, preferred_element_type=F32)


def _dot3(a_hi, a_lo, b_hi, b_lo):
    return _dot(a_hi, b_hi) + _dot(a_hi, b_lo) + _dot(a_lo, b_hi)


def _silu(x):
    return x / (1.0 + jnp.exp(-x))


def _norm_mod(x, g, shift, scale):
    ms = jnp.mean(x * x, axis=-1, keepdims=True)
    return (x * lax.rsqrt(ms + NORM_EPS)) * (g * (1.0 + scale)) + shift


def _ada_kernel(c_ref, w_ref, b_ref, o_ref):
    s_hi, s_lo = _split3(_silu(c_ref[...]))
    w_hi, w_lo = _split3(w_ref[0])
    o_ref[0] = _dot3(s_hi, s_lo, w_hi, w_lo) + b_ref[0]


def _ada(cvec, w_ada, b_ada):
    depth, d, nd = w_ada.shape
    tn = 1024
    return pl.pallas_call(
        _ada_kernel,
        grid=(depth, nd // tn),
        in_specs=[
            pl.BlockSpec((8, d), lambda l, j: (0, 0)),
            pl.BlockSpec((1, d, tn), lambda l, j: (l, 0, j)),
            pl.BlockSpec((1, 1, tn), lambda l, j: (l, 0, j)),
        ],
        out_specs=pl.BlockSpec((1, 8, tn), lambda l, j: (l, 0, j)),
        out_shape=jax.ShapeDtypeStruct((depth, 8, nd), F32),
        compiler_params=_params("parallel", "parallel"),
        name="ada",
    )(cvec, w_ada, b_ada.reshape(depth, 1, nd))


def _mix_out(oa_ref, ob_ref, oc_ref, w_ref):
    na = N_HEADS * HEAD_DIM
    return (_dot(oa_ref[0], w_ref[0:na, :]) + _dot(ob_ref[0], w_ref[na:2 * na, :])
            + _dot(oc_ref[0].astype(BF16), w_ref[2 * na:, :]))


N_PROJ_IN = 9


def _ffn_kernel(*refs, row0, final, mixed, rope, n_cast):
    refs = list(refs)
    if n_cast:
        cast_out = refs[-n_cast:]
        del refs[-n_cast:]
    x_ref, mod_ref, g_ref, wg_ref, wu_ref, wd_ref = refs[:6]
    del refs[:6]
    if mixed:
        oa_ref, ob_ref, oc_ref, wo_ref = refs[:4]
        del refs[:4]
    if final:
        gf_ref = refs.pop(0)
    proj_in = refs[:N_PROJ_IN] if rope is not None else []
    del refs[:len(proj_in)]
    cast_in = refs[:n_cast]
    o_ref = refs[n_cast]
    proj_out = refs[n_cast + 1:]
    shift = mod_ref[0, row0:row0 + 1, :]
    scale = mod_ref[0, row0 + 1:row0 + 2, :]
    gate = mod_ref[0, row0 + 2:row0 + 3, :]
    x = x_ref[0]
    if mixed:
        x = x + mod_ref[0, 5:6, :] * _mix_out(oa_ref, ob_ref, oc_ref, wo_ref)
    xm = _norm_mod(x, g_ref[...], shift, scale).astype(BF16)
    a = _dot(xm, wg_ref[...])
    u = _dot(xm, wu_ref[...])
    act = (_silu(a) * u).astype(BF16)
    y = x + (0.5 * gate) * _dot(act, wd_ref[...])
    if final:
        ms = jnp.mean(y * y, axis=-1, keepdims=True)
        y = (y * lax.rsqrt(ms + NORM_EPS)) * gf_ref[...]
    o_ref[0] = y
    if rope is not None:
        _proj_body(y, mod_ref, *proj_in, *proj_out, rope=rope)
    for src, dst in zip(cast_in, cast_out if n_cast else []):
        dst[0] = src[...].astype(BF16)


def _ffn(x, mod, g, wg, wu, wd, *, layer, row0, tm, mix=None, final_g=None, proj=None, cast=()):
    groups, t, d = x.shape
    final = final_g is not None
    tok = lambda b, i: (b, i, 0)
    in_specs = [
        pl.BlockSpec((1, tm, d), tok),
        pl.BlockSpec((1, N_MOD, d), lambda b, i: (b, 0, 0)),
        _resident((1, d)),
        _layer_resident(wg, layer),
        _layer_resident(wu, layer),
        _layer_resident(wd, layer),
    ]
    args = [x, mod, g.reshape(1, d), wg, wu, wd]
    if mix is not None:
        oa, ob, oc, w_out = mix
        in_specs += [pl.BlockSpec((1, tm, oa.shape[2]), tok), pl.BlockSpec((1, tm, ob.shape[2]), tok),
                     pl.BlockSpec((1, tm, oc.shape[2]), tok), _layer_resident(w_out, layer)]
        args += [oa, ob, oc, w_out]
    if final:
        in_specs.append(_resident((1, d)))
        args.append(final_g.reshape(1, d))
    out_specs = [pl.BlockSpec((1, tm, d), tok)]
    out_shape = [jax.ShapeDtypeStruct((groups, t, d), F32)]
    if proj is not None:
        in_specs += [
            _resident((1, d)),
            _layer_resident(proj["w_in"], layer),
            pl.BlockSpec((tm, LANE), lambda b, i: (i, 0)),
            pl.BlockSpec((tm, LANE), lambda b, i: (i, 0)),
            _resident((1, LANE)),
            _resident((1, LANE)),
            _resident((LANE, LANE)),
            _resident((LANE, LANE)),
            _layer_resident(proj["wc"], layer),
        ]
        args += [proj["g"].reshape(1, d), proj["w_in"], proj["cos"], proj["sin"], proj["gq"], proj["gk"],
                 proj["msm"], proj["perm"], proj["wc"]]
        q_spec = pl.BlockSpec((1, N_HEADS, tm, LANE), lambda b, i: (b, 0, i, 0))
        k_spec = pl.BlockSpec((1, tm, LANE), tok)
        vt_spec = pl.BlockSpec((1, LANE, tm), lambda b, i: (b, 0, i))
        q_shape = jax.ShapeDtypeStruct((groups, N_HEADS, t, LANE), BF16)
        k_shape = jax.ShapeDtypeStruct((groups, t, LANE), BF16)
        vt_shape = jax.ShapeDtypeStruct((groups, LANE, t), BF16)
        out_specs += [q_spec, k_spec, vt_spec, q_spec, k_spec, vt_spec, pl.BlockSpec((1, tm, 2 * C_WIDTH), tok)]
        out_shape += [q_shape, k_shape, vt_shape, q_shape, k_shape, vt_shape,
                      jax.ShapeDtypeStruct((groups, t, 2 * C_WIDTH), F32)]
    n_tiles = t // tm
    for w in cast:
        _, r, c = w.shape
        if r % (groups * n_tiles * 16) == 0:
            block, index = (r // (groups * n_tiles), c), lambda b, i: (b * n_tiles + i, 0)
        else:
            block, index = (r // n_tiles, c // groups), lambda b, i: (i, b)
        assert block[0] % 16 == 0 and block[1] % LANE == 0
        in_specs.append(pl.BlockSpec((None,) + block, lambda b, i, index=index: (layer,) + index(b, i)))
        args.append(w)
        out_specs.append(pl.BlockSpec((1,) + block, lambda b, i, index=index: (0,) + index(b, i)))
        out_shape.append(jax.ShapeDtypeStruct((1, r, c), BF16))
    out = pl.pallas_call(
        functools.partial(_ffn_kernel, row0=row0, final=final, mixed=mix is not None,
                          rope=None if proj is None else proj["rope"], n_cast=len(cast)),
        grid=(groups, t // tm),
        in_specs=in_specs,
        out_specs=out_specs,
        out_shape=out_shape,
        compiler_params=_params("parallel", "parallel"),
        name="ffn",
    )(*args)
    return out[0] if len(out) == 1 else out


def _rope_slab(t, cos, sin, perm):
    return t * cos + _dot(t.astype(BF16), perm) * sin


def _wfold_kernel(cs_hi_ref, cs_lo_ref, w_ref, o_ref):
    w_hi, w_lo = _split3(w_ref[0])
    for half in range(2):
        sl = slice(half * C_WIDTH, (half + 1) * C_WIDTH)
        o_ref[0, :, sl] = _dot3(cs_hi_ref[:, sl], cs_lo_ref[:, sl], w_hi, w_lo).astype(BF16)


def _wfold(cs_hi, cs_lo, wbd):
    depth = wbd.shape[0]
    return pl.pallas_call(
        _wfold_kernel,
        grid=(depth,),
        in_specs=[_resident(cs_hi.shape), _resident(cs_lo.shape),
                  pl.BlockSpec((1, C_WIDTH, C_WIDTH), lambda l: (l, 0, 0))],
        out_specs=pl.BlockSpec((1, C_WIDTH, 2 * C_WIDTH), lambda l: (l, 0, 0)),
        out_shape=jax.ShapeDtypeStruct((depth, C_WIDTH, 2 * C_WIDTH), BF16),
        compiler_params=_params("parallel"),
        name="wfold",
    )(cs_hi, cs_lo, wbd)


def _proj_body(x, mod_ref, g_ref, w_ref, cos_ref, sin_ref, gq_ref, gk_ref, msm_ref, perm_ref, wc_ref,
               qa_ref, ka_ref, vat_ref, qb_ref, kb_ref, vbt_ref, z_ref, *, rope):
    tm = x.shape[0]
    shift = mod_ref[0, 3:4, :]
    scale = mod_ref[0, 4:5, :]
    xm = _norm_mod(x, g_ref[...], shift, scale).astype(BF16)

    u_all = _dot(xm, w_ref[...])

    def u(lo, hi):
        return u_all[:, lo:hi]

    lane = lax.broadcasted_iota(jnp.int32, (tm, LANE), 1)
    low = lane < HEAD_DIM
    if rope:
        cos = cos_ref[...]
        sin = sin_ref[...]

    def qk_norm(t, gain):
        ms = _dot((t * t).astype(BF16), msm_ref[...])
        return t * lax.rsqrt(ms + NORM_EPS) * gain

    def finish(t):
        return _rope_slab(t, cos, sin, perm_ref[...]) if rope else t

    def emit_q(q_ref, slabs):
        for h in range(N_HEADS):
            t = slabs[h // 2] * (LOG2E * HEAD_DIM ** -0.5)
            if (h % 2) != (h // GROUP):
                t = pltpu.roll(t, HEAD_DIM, axis=1)
            keep = low if (h // GROUP) == 0 else jnp.logical_not(low)
            q_ref[0, h] = jnp.where(keep, t, 0.0).astype(BF16)

    qa = [finish(u(LANE * i, LANE * (i + 1))) for i in range(3)]
    emit_q(qa_ref, qa)
    ka_ref[0] = finish(u(384, 512)).astype(BF16)
    vat_ref[0] = u(512, 640).T.astype(BF16)

    qb = [finish(qk_norm(u(640 + LANE * i, 640 + LANE * (i + 1)), gq_ref[...])) for i in range(3)]
    emit_q(qb_ref, qb)
    kb_ref[0] = finish(qk_norm(u(1024, 1152), gk_ref[...])).astype(BF16)
    vbt_ref[0] = u(1152, 1280).T.astype(BF16)

    z_ref[0] = _dot(u(1280, 1536).astype(BF16), wc_ref[...])


def _nt(a, b):
    return lax.dot_general(a, b, (((1,), (1,)), ((), ())), preferred_element_type=F32)


def _store_heads(o_ref, ot, r0=0):
    tq = ot[0].shape[1]
    for pair in range(N_HEADS // 2):
        slab = jnp.concatenate([ot[2 * pair], ot[2 * pair + 1]], axis=0)
        o_ref[0, r0:r0 + tq, pair * LANE:(pair + 1) * LANE] = slab.T.astype(o_ref.dtype)


def _attn_latent_kernel(*refs, tq, tk):
    s_a, s_b, p_a, p_b = refs[13:17]
    n_sub = refs[1].shape[2] // tq
    tiles = [_tile_stages(sub, n_sub, *refs, tq=tq, tk=tk) for sub in range(n_sub)]
    nk = tiles[0]["nk"]
    n_steps = nk + 1
    total = n_sub * n_steps
    bufs = ((s_a, p_a), (s_b, p_b))

    def issue_scores(u):
        j, t = divmod(u, n_steps)
        s_buf, slot = bufs[u % 2][0], u % 2
        if t == 0:
            tiles[j]["scores_win"](s_buf, slot)
        else:
            tiles[j]["scores"](t - 1, s_buf, slot)

    def do_softmax(u):
        j, t = divmod(u, n_steps)
        (s_buf, p_buf), slot = bufs[u % 2], u % 2
        tiles[j]["softmax_win" if t == 0 else "softmax"](s_buf, p_buf, slot)

    def do_values(u):
        j, t = divmod(u, n_steps)
        p_buf, slot = bufs[u % 2][1], u % 2
        if t == 0:
            tiles[j]["values_win"](p_buf, slot)
        else:
            tiles[j]["values"](t - 1, p_buf, slot)
        if t == n_steps - 1:
            tiles[j]["finish"]()

    def flat_step(u):
        if u + 1 < total:
            issue_scores(u + 1)
        do_softmax(u)
        if u >= 1:
            do_values(u - 1)

    n_pairs = max(0, (nk - 3) // 2)
    issue_scores(0)
    for j in range(n_sub):
        base = j * n_steps
        flat_step(base)
        flat_step(base + 1)

        def pair(i, carry, tile=tiles[j], base=base):
            t = 2 + 2 * i
            for dt in (0, 1):
                par = (base + dt) % 2
                (s_cur, p_cur), (s_oth, p_oth) = bufs[par], bufs[1 - par]
                tile["scores"](t + dt, s_oth, 1 - par)
                tile["softmax"](s_cur, p_cur, par)
                tile["values"](t + dt - 2, p_oth, 1 - par)
            return carry

        lax.fori_loop(0, n_pairs, pair, 0)
        for t in range(2 + 2 * n_pairs, n_steps):
            flat_step(base + t)
    do_values(total - 1)


def _tile_stages(sub, n_sub, qa_ref, qb_ref, ka_ref, vat_ref, kb_ref, vbt_ref, kac_ref, vatc_ref, kbc_ref,
                 vbtc_ref, sink_ref, oa_ref, ob_ref, s_a, s_b, p_a, p_b, m_sc, bmax_sc, alpha_sc, acc_sc, *, tq, tk):
    length = kb_ref.shape[1]
    lc = kbc_ref.shape[1]
    nk = (length + lc) // tk
    tail = length - (nk - 1) * tk
    wq = min(tq, WIN_TILE)
    span = wq + 2 * WINDOW
    na = span + lc
    tile = pl.program_id(1) * n_sub + sub
    r0 = sub * tq
    m_sc, acc_sc = m_sc.at[sub], acc_sc.at[sub]
    starts = [pl.multiple_of(jnp.clip(tile * tq + i * wq - WINDOW, 0, length - span), LANE)
              for i in range(tq // wq)]
    m_sc[...] = jnp.full_like(m_sc, NEG_INF)
    acc_sc[...] = jnp.zeros_like(acc_sc)

    def with_ones(vt):
        return jnp.concatenate([vt, jnp.ones((DEN_ROWS, vt.shape[1]), BF16)], axis=0)

    def rows(g):
        return slice(g * HEAD_DIM, (g + 1) * HEAD_DIM)

    def sub_cols(h, i):
        return slice(h * tq + i * wq, h * tq + (i + 1) * wq)

    def scores_win(s_buf, slot):
        for i, start in enumerate(starts):
            kw = ka_ref[0, pl.ds(start, span), :]
            kpos = start + lax.broadcasted_iota(jnp.int32, (span, wq), 0)
            qpos = tile * tq + i * wq + lax.broadcasted_iota(jnp.int32, (span, wq), 1)
            valid = jnp.abs(kpos - qpos) <= WINDOW
            for h in range(N_HEADS):
                cs = sub_cols(h, i)
                q = qa_ref[0, h, r0 + i * wq:r0 + (i + 1) * wq, :]
                s_w = jnp.where(valid, _nt(kw, q), NEG_INF)
                s_c = _nt(kac_ref[0], q)
                s_buf[0:span, cs] = s_w
                s_buf[span:na, cs] = s_c
                bmax_sc[slot, :, cs] = jnp.maximum(jnp.max(s_w, axis=0, keepdims=True),
                                                   jnp.max(s_c, axis=0, keepdims=True))

    def softmax_win(s_buf, p_buf, slot):
        for h in range(N_HEADS):
            hc = head_cols(h)
            sink = sink_ref[:, hc]
            m = jnp.maximum(sink, bmax_sc[slot, :, hc])
            alpha_sc[slot, :, hc] = jnp.exp2(sink - m)
            p_buf[0:na, hc] = jnp.exp2(s_buf[0:na, hc] - m).astype(BF16)

    def values_win(p_buf, slot):
        parts = [[None] * len(starts) for _ in range(N_HEADS)]
        for g in range(2):
            for i, start in enumerate(starts):
                vt = with_ones(jnp.concatenate(
                    [vat_ref[0, rows(g), pl.ds(start, span)], vatc_ref[0, rows(g), :]], axis=1))
                for h in range(g * GROUP, (g + 1) * GROUP):
                    cs = sub_cols(h, i)
                    acc = _dot(vt, p_buf[0:na, cs])
                    parts[h][i] = acc[0:HEAD_DIM] / (acc[HEAD_DIM:HEAD_DIM + 1] + alpha_sc[slot, :, cs])
        _store_heads(oa_ref, [jnp.concatenate(p, axis=1) for p in parts], r0)

    def key_block(blk):
        if isinstance(blk, int) and blk == nk - 1:
            return jnp.concatenate([kb_ref[0, length - tail:, :], kbc_ref[0]], axis=0)
        return kb_ref[0, pl.ds(pl.multiple_of(blk * tk, tk), tk), :]

    def value_block(blk, g):
        if isinstance(blk, int) and blk == nk - 1:
            return jnp.concatenate([vbt_ref[0, rows(g), length - tail:], vbtc_ref[0, rows(g), :]], axis=1)
        return vbt_ref[0, rows(g), pl.ds(pl.multiple_of(blk * tk, tk), tk)]

    def head_cols(h):
        return slice(h * tq, (h + 1) * tq)

    def scores(blk, s_buf, slot):
        kb = key_block(blk)
        for h in range(N_HEADS):
            s = _nt(kb, qb_ref[0, h, r0:r0 + tq, :])
            s_buf[0:tk, head_cols(h)] = s
            bmax_sc[slot, :, head_cols(h)] = jnp.max(s, axis=0, keepdims=True)

    def softmax(s_buf, p_buf, slot):
        for h in range(N_HEADS):
            hc = head_cols(h)
            m_old = m_sc[:, hc]
            m_new = jnp.maximum(m_old, bmax_sc[slot, :, hc])
            m_sc[:, hc] = m_new
            alpha_sc[slot, :, hc] = jnp.exp2(m_old - m_new)
            p_buf[0:tk, hc] = jnp.exp2(s_buf[0:tk, hc] - m_new).astype(BF16)

    def values(blk, p_buf, slot):
        for g in range(2):
            vt = with_ones(value_block(blk, g))
            for h in range(g * GROUP, (g + 1) * GROUP):
                hc = head_cols(h)
                acc_sc[h] = acc_sc[h] * alpha_sc[slot, :, hc] + _dot(vt, p_buf[0:tk, hc])

    def finish():
        _store_heads(ob_ref, [acc_sc[h, 0:HEAD_DIM] / acc_sc[h, HEAD_DIM:HEAD_DIM + 1] for h in range(N_HEADS)], r0)

    return dict(nk=nk, scores_win=scores_win, softmax_win=softmax_win, values_win=values_win,
                scores=scores, softmax=softmax, values=values, finish=finish)


def _key_tile(n_keys, cap=1024):
    return max(t for t in range(LANE, cap + 1, LANE) if n_keys % t == 0)


def _attn_latent(qa, ka, vat, qb, kb, vbt, ka_c, vat_c, kb_c, vbt_c, sink_row, *, tq, tk, n_sub):
    b, _, l, _ = qb.shape
    lc = ka_c.shape[1] // b
    assert (l + lc) % tk == 0 and (l + lc) // tk * tk - tk <= l and tk > lc
    buf_rows = max(tk, min(tq, WIN_TILE) + 2 * WINDOW + lc)
    buf_cols = N_HEADS * tq
    q_spec = pl.BlockSpec((1, N_HEADS, n_sub * tq, LANE), lambda b, i: (b, 0, i, 0))
    k_spec = pl.BlockSpec((1, l, LANE), lambda b, i: (b, 0, 0), pipeline_mode=pl.Buffered(1))
    vt_spec = pl.BlockSpec((1, LANE, l), lambda b, i: (b, 0, 0), pipeline_mode=pl.Buffered(1))
    kc_spec = pl.BlockSpec((1, lc, LANE), lambda b, i: (0, b, 0), pipeline_mode=pl.Buffered(1))
    vtc_spec = pl.BlockSpec((1, LANE, lc), lambda b, i: (0, 0, b), pipeline_mode=pl.Buffered(1))
    o_spec = pl.BlockSpec((1, n_sub * tq, N_HEADS * HEAD_DIM), lambda b, i: (b, i, 0))
    o_shape = jax.ShapeDtypeStruct((b, l, N_HEADS * HEAD_DIM), BF16)
    return pl.pallas_call(
        functools.partial(_attn_latent_kernel, tq=tq, tk=tk),
        grid=(b, l // (n_sub * tq)),
        in_specs=[q_spec, q_spec, k_spec, vt_spec, k_spec, vt_spec, kc_spec, vtc_spec, kc_spec, vtc_spec,
                  pl.BlockSpec((1, N_HEADS * tq), lambda b, i: (0, 0))],
        out_specs=[o_spec, o_spec],
        out_shape=[o_shape, o_shape],
        scratch_shapes=[
            pltpu.VMEM((buf_rows, buf_cols), F32),
            pltpu.VMEM((buf_rows, buf_cols), F32),
            pltpu.VMEM((buf_rows, buf_cols), BF16),
            pltpu.VMEM((buf_rows, buf_cols), BF16),
            pltpu.VMEM((n_sub, 1, N_HEADS * tq), F32),
            pltpu.VMEM((2, 1, N_HEADS * tq), F32),
            pltpu.VMEM((2, 1, N_HEADS * tq), F32),
            pltpu.VMEM((n_sub, N_HEADS, HEAD_DIM + DEN_ROWS, tq), F32),
        ],
        compiler_params=_params("parallel", "parallel"),
        name="attn_latent",
    )(qa, qb, ka, vat, kb, vbt, ka_c, vat_c, kb_c, vbt_c, sink_row)


def _softmax_pv(q_ref, ks, vts, extra, mask=None):
    tq = q_ref.shape[2]
    aug = [[jnp.concatenate([vt[g * HEAD_DIM:(g + 1) * HEAD_DIM, :],
                             jnp.ones((DEN_ROWS, vt.shape[1]), BF16)], axis=0) for vt in vts]
           for g in range(2)]
    out = []
    for h in range(N_HEADS):
        ss = [_nt(k, q_ref[0, h]) for k in ks]
        if mask is not None:
            ss[0] = jnp.where(mask, ss[0], NEG_INF)
        m = functools.reduce(jnp.maximum, [jnp.max(s, axis=0, keepdims=True) for s in ss])
        if extra is not None:
            e = extra[:, h * tq:(h + 1) * tq]
            m = jnp.maximum(m, e)
        acc = functools.reduce(
            jnp.add, [_dot(vt, jnp.exp2(s - m).astype(BF16)) for s, vt in zip(ss, aug[h // GROUP])])
        den = acc[HEAD_DIM:HEAD_DIM + 1]
        if extra is not None:
            den = den + jnp.exp2(e - m)
        out.append(acc[0:HEAD_DIM] / den)
    return out


def _attn_ctx_kernel(q_ref, k_ref, vt_ref, *rest, has_sink):
    if has_sink:
        sink_ref, o_ref = rest
    else:
        (o_ref,) = rest
    extra = sink_ref[...] if has_sink else None
    _store_heads(o_ref, _softmax_pv(q_ref, [k_ref[0]], [vt_ref[0]], extra))


def _attn_ctx(q, k, vt, sink_row, *, batch):
    lc = q.shape[2] // batch
    in_specs = [
        pl.BlockSpec((1, N_HEADS, lc, LANE), lambda b: (0, 0, b, 0)),
        pl.BlockSpec((1, lc, LANE), lambda b: (0, b, 0)),
        pl.BlockSpec((1, LANE, lc), lambda b: (0, 0, b)),
    ]
    args = [q, k, vt]
    if sink_row is not None:
        in_specs.append(pl.BlockSpec((1, N_HEADS * lc), lambda b: (0, 0)))
        args.append(sink_row)
    return pl.pallas_call(
        functools.partial(_attn_ctx_kernel, has_sink=sink_row is not None),
        grid=(batch,),
        in_specs=in_specs,
        out_specs=pl.BlockSpec((1, lc, N_HEADS * HEAD_DIM), lambda b: (0, b, 0)),
        out_shape=jax.ShapeDtypeStruct((1, batch * lc, N_HEADS * HEAD_DIM), BF16),
        compiler_params=_params("parallel"),
        name="attn_ctx",
    )(*args)


def _stack3_lhs(hi, lo):
    return jnp.concatenate([hi, hi, lo], axis=1)


def _stack3_rhs(x):
    hi, lo = _split3(x)
    return jnp.concatenate([hi, lo, hi], axis=0)


def _fft1_tile(x, t_ref, twc_ref, tws_ref):
    n1, tn2, _ = x.shape
    xt = jnp.transpose(x, (1, 0, 2))
    y = _dot(t_ref[...], _stack3_rhs(jnp.concatenate([xt[i] for i in range(tn2)], axis=1)))
    cx, sx = y[0:n1], y[n1:]
    w = 2 * C_WIDTH
    outs = []
    for i in range(tn2):
        tc = jnp.concatenate([twc_ref[i]] * (C_WIDTH // LANE), axis=1)
        ts = jnp.concatenate([tws_ref[i]] * (C_WIDTH // LANE), axis=1)
        re, im = slice(i * w, i * w + C_WIDTH), slice(i * w + C_WIDTH, (i + 1) * w)
        ar = cx[:, re] + sx[:, im]
        ai = cx[:, im] - sx[:, re]
        outs.append(jnp.concatenate([ar * tc + ai * ts, ai * tc - ar * ts], axis=1))
    return jnp.transpose(jnp.stack(outs, axis=0), (1, 0, 2))


def _real_dft(t_ref, x):
    rhs = jnp.concatenate([_stack3_rhs(x[:, 0:C_WIDTH]), _stack3_rhs(x[:, C_WIDTH:])], axis=0)
    return _dot(t_ref[...], rhs)


def _fft_kernel(x_ref, t1_ref, twc_ref, tws_ref, t2_ref, o_ref, mid, *, n_a):
    j = pl.program_id(1)
    tn2 = x_ref.shape[2]
    tk1 = o_ref.shape[2]

    @pl.when(j < n_a)
    def _():
        mid[:, pl.ds(pl.multiple_of(j * tn2, tn2), tn2), :] = _fft1_tile(x_ref[0], t1_ref, twc_ref, tws_ref)

    @pl.when(j >= n_a)
    def _():
        off = (j - n_a) * tk1
        outs = [_real_dft(t2_ref, mid[off + i]) for i in range(tk1)]
        o_ref[0] = jnp.transpose(jnp.stack(outs, axis=0), (1, 0, 2))


def _dft_tables(n, norm):
    k = np.arange(n, dtype=np.int64)
    ang = 2.0 * np.pi * ((k[:, None] * k[None, :]) % n).astype(np.float64) / n
    c = jnp.asarray(np.cos(ang) * norm, F32)
    s = jnp.asarray(np.sin(ang) * norm, F32)
    return _stack3_lhs(*_split3(c)), _stack3_lhs(*_split3(s))


def _fourier_latent(z):
    b, l, w = z.shape
    n1, n2 = FFT_N1, l // FFT_N1
    tn2, tk1 = 16, 16
    prod = jnp.arange(n2)[:, None] * jnp.arange(n1)[None, :]
    ang = prod.astype(F32) * (2.0 * np.pi / l)
    twc = jnp.broadcast_to(jnp.cos(ang)[:, :, None], (n2, n1, LANE))
    tws = jnp.broadcast_to(jnp.sin(ang)[:, :, None], (n2, n1, LANE))
    t1 = jnp.concatenate(_dft_tables(n1, 1.0), axis=0)
    t2m = jnp.concatenate(_dft_tables(n2, 1.0 / np.sqrt(l)), axis=1)
    n_a, n_b = n2 // tn2, n1 // tk1
    stage1 = lambda j: jnp.minimum(j, n_a - 1)
    stage2 = lambda j: jnp.maximum(j - n_a, 0)
    out = pl.pallas_call(
        functools.partial(_fft_kernel, n_a=n_a),
        grid=(b, n_a + n_b),
        in_specs=[pl.BlockSpec((1, n1, tn2, w), lambda b, j: (b, 0, stage1(j), 0)),
                  _resident(t1.shape),
                  pl.BlockSpec((tn2, n1, LANE), lambda b, j: (stage1(j), 0, 0)),
                  pl.BlockSpec((tn2, n1, LANE), lambda b, j: (stage1(j), 0, 0)),
                  _resident(t2m.shape)],
        out_specs=pl.BlockSpec((1, n2, tk1, C_WIDTH), lambda b, j: (b, 0, stage2(j), 0)),
        out_shape=jax.ShapeDtypeStruct((b, n2, n1, C_WIDTH), F32),
        scratch_shapes=[pltpu.VMEM((n1, n2, w), F32)],
        compiler_params=_params("parallel", "arbitrary"),
        name="fft",
    )(z.reshape(b, n1, n2, w), t1, twc, tws, t2m)
    return out.reshape(b, l, C_WIDTH)


def _fourier_ctx_kernel(x_ref, t_ref, o_ref):
    o_ref[0] = _real_dft(t_ref, x_ref[0])


def _fourier_ctx(z, *, batch):
    lc = z.shape[1] // batch
    tabs = jnp.concatenate(_dft_tables(lc, 1.0 / np.sqrt(lc)), axis=1)
    return pl.pallas_call(
        _fourier_ctx_kernel,
        grid=(batch,),
        in_specs=[pl.BlockSpec((1, lc, 2 * C_WIDTH), lambda b: (0, b, 0)), _resident(tabs.shape)],
        out_specs=pl.BlockSpec((1, lc, C_WIDTH), lambda b: (0, b, 0)),
        out_shape=jax.ShapeDtypeStruct((1, batch * lc, C_WIDTH), F32),
        compiler_params=_params("parallel"),
        name="fourier_ctx",
    )(z, tabs)


def _rope_tables(length):
    t = jnp.arange(length)
    row = (t // GRID_W).astype(F32)
    col = (t % GRID_W).astype(F32)
    half = HEAD_DIM // 2
    inv = ROPE_THETA ** (-jnp.arange(0, half, 2, dtype=F32) / half)
    ar, ac = row[:, None] * inv, col[:, None] * inv
    cos = jnp.concatenate([jnp.cos(ar), jnp.cos(ar), jnp.cos(ac), jnp.cos(ac)], axis=-1)
    sin = jnp.concatenate([-jnp.sin(ar), jnp.sin(ar), -jnp.sin(ac), jnp.sin(ac)], axis=-1)
    return jnp.tile(cos, (1, 2)), jnp.tile(sin, (1, 2))


def _channel_dft():
    k = np.arange(C_GROUP_DIM)
    ang = 2.0 * np.pi * ((k[:, None] * k[None, :]) % C_GROUP_DIM) / C_GROUP_DIM
    eye = np.eye(C_GROUPS)
    norm = 1.0 / np.sqrt(C_GROUP_DIM)
    cs = np.concatenate([np.kron(eye, np.cos(ang) * norm), np.kron(eye, -np.sin(ang) * norm)], axis=1)
    return _split3(jnp.asarray(cs, F32))


def _block_diag(w):
    g, c, _ = w.shape
    eye = jnp.eye(g, dtype=w.dtype)
    return (eye[:, None, :, None] * w[:, :, None, :]).reshape(g * c, g * c)


def kernel(x, c, ctx, c_ctx, w_ada, b_ada, g_ffn1, g_mix, g_ffn2, w_in, g_qn, g_kn, sink, w_four,
           w_out, w1_gate, w1_up, w1_down, w2_gate, w2_up, w2_down, g_final):
    batch, length, d = x.shape
    lc = ctx.shape[1]
    depth = w_ada.shape[0]
    assert length % (FFT_N1 * 8) == 0 and length >= BLOCK + 2 * WINDOW and batch + 1 <= 8
    tm = 512
    tm_ctx = batch * lc

    cvec = jnp.zeros((8, d), F32).at[:batch].set(c).at[batch].set(c_ctx)
    mod = _ada(cvec, w_ada, b_ada).reshape(depth, 8, N_MOD, d)

    cos, sin = _rope_tables(length)
    ones = jnp.ones((tm_ctx, LANE), F32)
    lanes = np.arange(LANE)
    src = np.where(lanes % 32 < 16, lanes + 16, lanes - 16)
    perm = jnp.asarray(lanes[:, None] == src[None, :], BF16)
    msm = _block_diag(jnp.full((2, HEAD_DIM, HEAD_DIM), 1.0 / HEAD_DIM, F32)).astype(BF16)
    wc = _wfold(*_channel_dft(), jax.vmap(_block_diag)(w_four))

    w1 = tuple(w.astype(BF16) for w in (w1_gate, w1_up, w1_down))
    win = w_in.astype(BF16)

    h = x
    hc = ctx.reshape(1, batch * lc, d)
    for l in range(depth):
        last = l == depth - 1
        mod_l, mod_c = mod[l, :batch], mod[l, batch:batch + 1]
        gq = jnp.tile(g_qn[l].reshape(1, HEAD_DIM), (1, 2))
        gk = jnp.tile(g_kn[l].reshape(1, HEAD_DIM), (1, 2))
        proj = dict(g=g_mix[l], w_in=win, gq=gq, gk=gk, msm=msm, perm=perm, wc=wc)

        h, qa, ka, vat, qb, kb, vbt, z, *w2, wout = _ffn(
            h, mod_l, g_ffn1[l], *w1, layer=l, row0=0, tm=tm, proj=dict(proj, cos=cos, sin=sin, rope=True),
            cast=(w2_gate, w2_up, w2_down, w_out))
        hc, qa_c, ka_c, vat_c, qb_c, kb_c, vbt_c, z_c = _ffn(
            hc, mod_c, g_ffn1[l], *w1, layer=l, row0=0, tm=tm_ctx, proj=dict(proj, cos=ones, sin=ones, rope=False))

        tq = 512
        sink_l = jnp.repeat(sink[l].astype(F32) * LOG2E, tq).reshape(1, N_HEADS * tq)
        oa, ob = _attn_latent(qa, ka, vat, qb, kb, vbt, ka_c, vat_c, kb_c, vbt_c, sink_l,
                              tq=tq, tk=_key_tile(length + lc), n_sub=1)
        oc = _fourier_latent(z)
        h = _ffn(h, mod_l, g_ffn2[l], *w2, layer=0, row0=6, tm=tm, mix=(oa, ob, oc, wout),
                 final_g=g_final if last else None)

        if not last:
            sink_c = jnp.repeat(sink[l].astype(F32) * LOG2E, lc).reshape(1, N_HEADS * lc)
            oa_c = _attn_ctx(qa_c, ka_c, vat_c, sink_c, batch=batch)
            ob_c = _attn_ctx(qb_c, kb_c, vbt_c, None, batch=batch)
            oc_c = _fourier_ctx(z_c, batch=batch)
            hc = _ffn(hc, mod_c, g_ffn2[l], *w2, layer=0, row0=6, tm=tm_ctx, mix=(oa_c, ob_c, oc_c, wout))
    return h
```
